```python
import math
import jax, jax.numpy as jnp
from jax import lax
import numpy as np

D_MODEL = 1024
BATCH = 1
SEQ = 16384
DEPTH = 2

CHUNK = 64
MEM_LEN = 256
NORM_EPS = 1e-6
GN_EPS = 1e-5

LRU_W = 512
LRU_BLOCKS = 8
LRU_BW = LRU_W // LRU_BLOCKS
CONV_W = 4
LRU_C = 8.0
CA_HEADS = 8
CA_HD = 64
CA_PREV = 8
CA_BAND = (CA_PREV + 1) * CHUNK
REL_CLIP = 128
RET_HEADS = 8
RET_HD = 64
RET_VD = 64
DIFF_HEADS = 4
DIFF_HD = 64
DIFF_VD = 2 * DIFF_HD
Q_BLOCK = 128
MX_HEADS = 4
MX_HD = 128
N_GROUPS = 4
EXP_PER_GROUP = 8
N_EXPERTS = N_GROUPS * EXP_PER_GROUP
TOP_K = 2
D_EXPERT = 512
MOE_BLOCK = 256
N_BRANCH = 4
BRANCH_W = 512

SPLIT_WIDTHS = (
    LRU_W, LRU_W,
    CA_HEADS * CA_HD, CA_HEADS * CA_HD, CA_HEADS * CA_HD,
    RET_HEADS * RET_HD, RET_HEADS * RET_HD, RET_HEADS * RET_VD, RET_HEADS * RET_VD,
    DIFF_HEADS * 2 * DIFF_HD, DIFF_HEADS * 2 * DIFF_HD, DIFF_HEADS * DIFF_VD,
    N_BRANCH * D_MODEL,
)
IN_COLS = sum(SPLIT_WIDTHS)

kernel_name = "hybrid_lru_chunkattn_retention_diffattn_hmoe"


def _rmsnorm(x, g, eps=NORM_EPS):
    xf = x.astype(jnp.float32)
    y = xf * lax.rsqrt(jnp.mean(xf * xf, axis=-1, keepdims=True) + eps)
    return (y * g.astype(jnp.float32)).astype(x.dtype)


def _split_columns(z):
    parts, off = [], 0
    for w in SPLIT_WIDTHS:
        parts.append(z[..., off:off + w])
        off += w
    return parts


def _rg_lru_branch(xr, xg, conv_w, conv_b, wa, ba, wx, bx, lam):
    b, n, _ = xr.shape
    f32 = jnp.float32
    xp = jnp.pad(xr, ((0, 0), (CONV_W - 1, 0), (0, 0)))
    xc = conv_b + sum(xp[:, j:j + n] * conv_w[j] for j in range(CONV_W))
    xc32 = xc.astype(f32)
    xblk = xc32.reshape(b, n, LRU_BLOCKS, LRU_BW)
    r = jax.nn.sigmoid(jnp.einsum("bnkc,kcd->bnkd", xblk, wa.astype(f32)).reshape(b, n, LRU_W) + ba.astype(f32))
    i = jax.nn.sigmoid(jnp.einsum("bnkc,kcd->bnkd", xblk, wx.astype(f32)).reshape(b, n, LRU_W) + bx.astype(f32))
    log_a = -LRU_C * r * jax.nn.softplus(-lam.astype(f32))
    a = jnp.exp(log_a)
    u = jnp.sqrt(-jnp.expm1(2.0 * log_a)) * (i * xc32)

    def combine(left, right):
        a1, b1 = left
        a2, b2 = right
        return a1 * a2, a2 * b1 + b2

    _, hseq = lax.associative_scan(combine, (a, u), axis=1)
    return (hseq * jax.nn.gelu(xg.astype(f32))).astype(xr.dtype)


def _chunk_attention(q, k, v, rel_table):
    b, n, _ = q.shape
    nc = n // CHUNK
    q = q.reshape(b, nc, CHUNK, CA_HEADS, CA_HD)
    k = k.reshape(b, nc, CHUNK, CA_HEADS, CA_HD)
    v = v.reshape(b, nc, CHUNK, CA_HEADS, CA_HD)
    pad = ((0, 0), (CA_PREV, 0), (0, 0), (0, 0), (0, 0))
    kp, vp = jnp.pad(k, pad), jnp.pad(v, pad)
    k_band = jnp.concatenate([kp[:, j:j + nc] for j in range(CA_PREV + 1)], axis=2)
    v_band = jnp.concatenate([vp[:, j:j + nc] for j in range(CA_PREV + 1)], axis=2)
    s = jnp.einsum("bcqhd,bckhd->bhcqk", q, k_band).astype(jnp.float32) * (CA_HD ** -0.5)
    qi = np.arange(CHUNK)[:, None]
    kj = np.arange(CA_BAND)[None, :]
    dist = qi + CA_PREV * CHUNK - kj
    idx = np.clip(dist, -REL_CLIP, REL_CLIP) + REL_CLIP
    bias = rel_table[:, idx].astype(jnp.float32)
    key_chunk = np.arange(nc)[:, None] - CA_PREV + (np.arange(CA_BAND) // CHUNK)[None, :]
    valid = key_chunk >= 0
    s = jnp.where(valid[None, None, :, None, :], s + bias[None, :, None], -jnp.inf)
    p = jax.nn.softmax(s, axis=-1).astype(v.dtype)
    o = jnp.einsum("bhcqk,bckhd->bcqhd", p, v_band)
    return o.reshape(b, n, CA_HEADS * CA_HD)


def _retention(q, k, v, g):
    b, n, _ = q.shape
    nc = n // CHUNK
    f32 = jnp.float32
    q = q.reshape(b, nc, CHUNK, RET_HEADS, RET_HD).astype(f32)
    k = k.reshape(b, nc, CHUNK, RET_HEADS, RET_HD).astype(f32) * (RET_HD ** -0.5)
    v = v.reshape(b, nc, CHUNK, RET_HEADS, RET_VD).astype(f32)
    log_g = np.log(1.0 - 2.0 ** (-5.0 - np.arange(RET_HEADS))).astype(np.float32)
    pos = np.arange(CHUNK)
    diff = pos[:, None] - pos[None, :]
    decay = np.where(diff[None] >= 0, np.exp(log_g[:, None, None] * np.maximum(diff, 0)[None]), 0.0).astype(np.float32)
    inner = jnp.einsum("bcihd,bcjhd->bchij", q, k) * decay[None, None]
    o_inner = jnp.einsum("bchij,bcjhe->bcihe", inner, v)
    zeta = np.exp(log_g[None, :] * (CHUNK - 1 - pos)[:, None]).astype(np.float32)
    kv = jnp.einsum("bcjhd,jh,bcjhe->bchde", k, zeta, v)
    g_chunk = np.exp(log_g * CHUNK).astype(np.float32)

    def step(state, kv_c):
        return g_chunk[:, None, None] * state + kv_c, state

    init = jnp.zeros((b, RET_HEADS, RET_HD, RET_VD), f32)
    _, r_prev = lax.scan(step, init, jnp.moveaxis(kv, 1, 0))
    r_prev = jnp.moveaxis(r_prev, 0, 1)
    xi = np.exp(log_g[None, :] * (pos + 1)[:, None]).astype(np.float32)
    o_cross = jnp.einsum("bcihd,bchde->bcihe", q, r_prev) * xi[None, None, :, :, None]
    o = o_inner + o_cross
    mu = jnp.mean(o, axis=-1, keepdims=True)
    var = jnp.mean(jnp.square(o - mu), axis=-1, keepdims=True)
    o = ((o - mu) * lax.rsqrt(var + GN_EPS)).reshape(b, n, RET_HEADS * RET_VD)
    return (jax.nn.silu(g.astype(f32)) * o).astype(g.dtype)


def _diff_attention(q, k, v, lam_vecs, subln_g, lam_init):
    b, n, _ = q.shape
    f32 = jnp.float32
    q = q.reshape(b, n, DIFF_HEADS, 2, DIFF_HD)
    k = k.reshape(b, n, DIFF_HEADS, 2, DIFF_HD)
    v = v.reshape(b, n, DIFF_HEADS, DIFF_VD)
    lv = lam_vecs.astype(f32)
    lam = jnp.exp(jnp.sum(lv[0] * lv[1])) - jnp.exp(jnp.sum(lv[2] * lv[3])) + lam_init
    slopes = (2.0 ** (-8.0 * np.arange(1, DIFF_HEADS + 1) / DIFF_HEADS)).astype(np.float32)
    nb = n // Q_BLOCK
    qb = jnp.moveaxis(q.reshape(b, nb, Q_BLOCK, DIFF_HEADS, 2, DIFF_HD), 1, 0)
    kpos = jnp.arange(n)

    def block(args):
        q_blk, blk = args
        qpos = blk * Q_BLOCK + jnp.arange(Q_BLOCK)
        s = jnp.einsum("bqhmd,bkhmd->bmhqk", q_blk, k).astype(f32) * (DIFF_HD ** -0.5)
        dist = jnp.abs(qpos[:, None] - kpos[None, :]).astype(f32)
        alibi = -slopes[:, None, None] * dist[None]
        allowed = (kpos[None, :] // CHUNK) <= (qpos[:, None] // CHUNK)
        s = jnp.where(allowed, s + alibi, -jnp.inf)
        p = jax.nn.softmax(s, axis=-1)
        attn = p[:, 0] - lam * p[:, 1]
        return jnp.einsum("bhqk,bkhe->bqhe", attn.astype(v.dtype), v)

    o = lax.map(block, (qb, jnp.arange(nb)))
    o = jnp.moveaxis(o, 0, 1).reshape(b, n, DIFF_HEADS, DIFF_VD).astype(f32)
    o = o * lax.rsqrt(jnp.mean(o * o, axis=-1, keepdims=True) + GN_EPS) * subln_g.astype(f32)
    o = o * (1.0 - lam_init)
    return o.reshape(b, n, DIFF_HEADS * DIFF_VD).astype(v.dtype)


def _memory_cross_attention(h, mem_n, wq, wk, wv, wo):
    b, n, _ = h.shape
    m = mem_n.shape[1]
    q = (h @ wq).reshape(b, n, MX_HEADS, MX_HD)
    k = (mem_n @ wk).reshape(b, m, MX_HEADS, MX_HD)
    v = (mem_n @ wv).reshape(b, m, MX_HEADS, MX_HD)
    s = jnp.einsum("bnhd,bmhd->bhnm", q, k).astype(jnp.float32) * (MX_HD ** -0.5)
    p = jax.nn.softmax(s, axis=-1).astype(v.dtype)
    o = jnp.einsum("bhnm,bmhd->bnhd", p, v).reshape(b, n, MX_HEADS * MX_HD)
    return o @ wo


def _hier_moe(h, w_group, b_group, w_router, b_router, w1, w3, w2):
    b, n, d = h.shape
    t = h.reshape(b * n, d)
    n_tok = b * n
    f32 = jnp.float32
    rows = jnp.arange(n_tok)
    g_logits = (t @ w_group).astype(f32) + b_group.astype(f32)
    g_prob = jax.nn.softmax(g_logits, axis=-1)
    g_sel = jnp.argmax(g_logits, axis=-1)
    e_logits = ((t @ w_router).astype(f32) + b_router.astype(f32)).reshape(n_tok, N_GROUPS, EXP_PER_GROUP)
    e_prob = jax.nn.softmax(e_logits[rows, g_sel], axis=-1)
    top_p, top_i = lax.top_k(e_prob, TOP_K)
    top_p = top_p / jnp.sum(top_p, axis=-1, keepdims=True) * g_prob[rows, g_sel][:, None]
    expert = g_sel[:, None] * EXP_PER_GROUP + top_i
    flat_e = expert.reshape(-1).astype(jnp.int32)
    flat_w = top_p.reshape(-1)
    n_asg = n_tok * TOP_K
    order = jnp.argsort(flat_e)
    sorted_e = flat_e[order]
    counts = jnp.bincount(flat_e, length=N_EXPERTS)
    padded = (counts + MOE_BLOCK - 1) // MOE_BLOCK * MOE_BLOCK
    pad_end = jnp.cumsum(padded)
    pad_start = pad_end - padded
    start = jnp.cumsum(counts) - counts
    dest = pad_start[sorted_e] + jnp.arange(n_asg) - start[sorted_e]
    n_blocks = -(-(n_asg + N_EXPERTS * (MOE_BLOCK - 1)) // MOE_BLOCK)
    n_slots = n_blocks * MOE_BLOCK
    slot_tok = jnp.full((n_slots,), n_tok, jnp.int32).at[dest].set((order // TOP_K).astype(jnp.int32))
    slot_w = jnp.zeros((n_slots,), f32).at[dest].set(flat_w[order])
    block_e = jnp.minimum(jnp.searchsorted(pad_end, jnp.arange(n_blocks) * MOE_BLOCK, side="right"), N_EXPERTS - 1)
    t_pad = jnp.concatenate([t, jnp.zeros((1, d), t.dtype)], axis=0)
    xs = t_pad[slot_tok].reshape(n_blocks, MOE_BLOCK, d)

    def expert_block(args):
        xb, e = args
        hid = jax.nn.silu(xb @ w1[e]) * (xb @ w3[e])
        return hid @ w2[e]

    ys = lax.map(expert_block, (xs, block_e)).reshape(n_slots, d)
    out = jax.ops.segment_sum(ys * slot_w[:, None].astype(ys.dtype), slot_tok, num_segments=n_tok + 1)[:n_tok]
    return out.reshape(b, n, d).astype(h.dtype)


def setup_inputs(seed: int = 0) -> dict:
    key = jax.random.key(seed)
    ks = iter(jax.random.split(key, 40))
    L, D = DEPTH, D_MODEL

    def normal(shape, scale):
        return jax.random.normal(next(ks), shape, jnp.float32) * scale

    def gain(shape):
        return 1.0 + 0.05 * jax.random.normal(next(ks), shape, jnp.float32)

    a0 = jax.random.uniform(next(ks), (L, LRU_W), jnp.float32, 0.9, 0.999)
    a_root = a0 ** (1.0 / LRU_C)
    lru_lambda = jnp.log(a_root) - jnp.log1p(-a_root)
    return {
        "x": normal((BATCH, SEQ, D), 1.0),
        "mem": normal((BATCH, MEM_LEN, D), 1.0),
        "norm_mix": gain((L, D)),
        "w_in": normal((L, D, IN_COLS), D ** -0.5),
        "conv_w": normal((L, CONV_W, LRU_W), CONV_W ** -0.5),
        "conv_b": normal((L, LRU_W), 0.01),
        "lru_wa": normal((L, LRU_BLOCKS, LRU_BW, LRU_BW), LRU_BW ** -0.5),
        "lru_ba": normal((L, LRU_W), 0.01),
        "lru_wx": normal((L, LRU_BLOCKS, LRU_BW, LRU_BW), LRU_BW ** -0.5),
        "lru_bx": normal((L, LRU_W), 0.01),
        "lru_lambda": lru_lambda,
        "ca_rel_bias": normal((L, CA_HEADS, 2 * REL_CLIP + 1), 0.1),
        "diff_lambda": normal((L, 4, DIFF_HD), 0.1),
        "diff_subln": gain((L, DIFF_VD)),
        "w_branch": normal((L, N_BRANCH, BRANCH_W, D), BRANCH_W ** -0.5),
        "w_out": normal((L, D, D), D ** -0.5),
        "norm_mem": gain((L, D)),
        "mem_norm": gain((D,)),
        "w_mq": normal((L, D, MX_HEADS * MX_HD), D ** -0.5),
        "w_mk": normal((L, D, MX_HEADS * MX_HD), D ** -0.5),
        "w_mv": normal((L, D, MX_HEADS * MX_HD), D ** -0.5),
        "w_mo": normal((L, MX_HEADS * MX_HD, D), (MX_HEADS * MX_HD) ** -0.5),
        "norm_ffn": gain((L, D)),
        "w_group": normal((L, D, N_GROUPS), D ** -0.5),
        "b_group": normal((L, N_GROUPS), 0.01),
        "w_router": normal((L, D, N_EXPERTS), D ** -0.5),
        "b_router": normal((L, N_EXPERTS), 0.01),
        "w1": normal((L, N_EXPERTS, D, D_EXPERT), D ** -0.5),
        "w3": normal((L, N_EXPERTS, D, D_EXPERT), D ** -0.5),
        "w2": normal((L, N_EXPERTS, D_EXPERT, D), D_EXPERT ** -0.5),
        "final_norm": gain((D,)),
    }


def reference(x, mem, norm_mix, w_in, conv_w, conv_b, lru_wa, lru_ba, lru_wx, lru_bx, lru_lambda,
              ca_rel_bias, diff_lambda, diff_subln, w_branch, w_out, norm_mem, mem_norm,
              w_mq, w_mk, w_mv, w_mo, norm_ffn, w_group, b_group, w_router, b_router,
              w1, w3, w2, final_norm):
    b, n, d = x.shape
    mem_n = _rmsnorm(mem, mem_norm)
    for l in range(DEPTH):
        lam_init = 0.8 - 0.6 * math.exp(-0.3 * l)
        h = _rmsnorm(x, norm_mix[l])
        z = h @ w_in[l]
        (a_x, a_g, b_q, b_k, b_v, c_q, c_k, c_v, c_g, d_q, d_k, d_v, gates) = _split_columns(z)
        y_a = _rg_lru_branch(a_x, a_g, conv_w[l], conv_b[l], lru_wa[l], lru_ba[l], lru_wx[l], lru_bx[l], lru_lambda[l])
        y_b = _chunk_attention(b_q, b_k, b_v, ca_rel_bias[l])
        y_c = _retention(c_q, c_k, c_v, c_g)
        y_d = _diff_attention(d_q, d_k, d_v, diff_lambda[l], diff_subln[l], lam_init)
        ys = jnp.stack([y_a, y_b, y_c, y_d], axis=2)
        proj = jnp.einsum("bnkc,kcd->bnkd", ys, w_branch[l])
        g = jax.nn.sigmoid(gates.reshape(b, n, N_BRANCH, d))
        x = x + jnp.sum(g * proj, axis=2) @ w_out[l]
        x = x + _memory_cross_attention(_rmsnorm(x, norm_mem[l]), mem_n, w_mq[l], w_mk[l], w_mv[l], w_mo[l])
        x = x + _hier_moe(_rmsnorm(x, norm_ffn[l]), w_group[l], b_group[l], w_router[l], b_router[l], w1[l], w3[l], w2[l])
    return _rmsnorm(x, final_norm)
```

```python
import functools
import math

import numpy as np
import jax
import jax.numpy as jnp
from jax import lax
from jax.experimental import pallas as pl
from jax.experimental.pallas import tpu as pltpu

F32 = jnp.float32
BF16 = jnp.bfloat16

D_MODEL = 1024
DEPTH = 2
CHUNK = 64
NORM_EPS = 1e-6
GN_EPS = 1e-5
LRU_W = 512
LRU_BLOCKS = 8
LRU_BW = LRU_W // LRU_BLOCKS
CONV_W = 4
LRU_C = 8.0
CA_HEADS = 8
CA_HD = 64
CA_PREV = 8
REL_CLIP = 128
RET_HEADS = 8
RET_HD = 64
DIFF_HEADS = 4
DIFF_HD = 64
DIFF_VD = 2 * DIFF_HD
MX_HEADS = 4
MX_HD = 128
N_GROUPS = 4
EXP_PER_GROUP = 8
N_EXPERTS = N_GROUPS * EXP_PER_GROUP
D_EXPERT = 512
MOE_BLOCK = 256
N_BRANCH = 4
BRANCH_W = 512

COL_AX, COL_AG = 0, 512
COL_BQ, COL_BK, COL_BV = 1024, 1536, 2048
COL_CQ, COL_CK, COL_CV, COL_CG = 2560, 3072, 3584, 4096
COL_DQ, COL_DK, COL_DV = 4608, 5120, 5632
COL_GATE = 6144
IN_COLS = 10240

LANES = 128
NEG_BIG = -1e30
VMEM_LIMIT = 56 * 1024 * 1024


def _cparams(n_axes):
    return pltpu.CompilerParams(dimension_semantics=("arbitrary",) * n_axes, vmem_limit_bytes=VMEM_LIMIT)


def _rms(x, g):
    return x * lax.rsqrt(jnp.mean(x * x, axis=-1, keepdims=True) + NORM_EPS) * g


def _dot(a, b):
    return jnp.dot(a, b, preferred_element_type=F32)


def _dot_nt(a, b):
    return lax.dot_general(a, b, (((1,), (1,)), ((), ())), preferred_element_type=F32)


def _dot_tn(a, b):
    return lax.dot_general(a, b, (((0,), (0,)), ((), ())), preferred_element_type=F32)


def _inproj_kernel(x_ref, g_ref, w_ref, o_ref, h_scr):
    @pl.when(pl.program_id(1) == 0)
    def _():
        h_scr[...] = _rms(x_ref[...], g_ref[...]).astype(BF16)

    o_ref[...] = _dot(h_scr[...], w_ref[...]).astype(o_ref.dtype)


def _inproj(x, g, w, tm=512, tn=1024):
    n = x.shape[0]
    tm = min(tm, n)
    return pl.pallas_call(
        _inproj_kernel,
        grid=(n // tm, IN_COLS // tn),
        in_specs=[
            pl.BlockSpec((tm, D_MODEL), lambda i, j: (i, 0)),
            pl.BlockSpec((1, D_MODEL), lambda i, j: (0, 0)),
            pl.BlockSpec((D_MODEL, tn), lambda i, j: (0, j)),
        ],
        out_specs=pl.BlockSpec((tm, tn), lambda i, j: (i, j)),
        out_shape=jax.ShapeDtypeStruct((n, IN_COLS), BF16),
        scratch_shapes=[pltpu.VMEM((tm, D_MODEL), BF16)],
        compiler_params=_cparams(2),
        name="inproj",
    )(x, g, w)


def _gelu_tanh(x):
    return 0.5 * x * (1.0 + jnp.tanh(math.sqrt(2.0 / math.pi) * (x + 0.044715 * x * x * x)))


def _lru_kernel(ax_ref, ag_ref, cw_ref, cb_ref, w_ref, b_ref, lam_ref, o_ref, xbuf, a_scr, u_scr, h_scr, hc):
    t = ax_ref.shape[0]

    @pl.when(pl.program_id(0) == 0)
    def _():
        xbuf[0:8, :] = jnp.zeros((8, LRU_W), F32)
        hc[...] = jnp.zeros_like(hc)

    x = ax_ref[...].astype(F32)
    xbuf[8:8 + t, :] = x
    xc = (cb_ref[...] + cw_ref[3:4, :] * x + cw_ref[2:3, :] * xbuf[7:7 + t, :]
          + cw_ref[1:2, :] * xbuf[6:6 + t, :] + cw_ref[0:1, :] * xbuf[5:5 + t, :])
    xbuf[0:8, :] = xbuf[t:t + 8, :]
    ri = _dot(xc.astype(BF16), w_ref[...]) + b_ref[...]
    r = jax.nn.sigmoid(ri[:, :LRU_W])
    ig = jax.nn.sigmoid(ri[:, LRU_W:])
    nl = -lam_ref[...]
    softplus = jnp.maximum(nl, 0.0) + jnp.log(1.0 + jnp.exp(-jnp.abs(nl)))
    log_a = -LRU_C * r * softplus
    a = jnp.exp(log_a)
    a_scr[...] = a
    u_scr[...] = jnp.sqrt(-jnp.tanh(log_a) * (a * a + 1.0)) * (ig * xc)

    def body(s, h):
        h = a_scr[pl.ds(s, 1), :] * h + u_scr[pl.ds(s, 1), :]
        h_scr[pl.ds(s, 1), :] = h
        return h

    hc[0:1, :] = lax.fori_loop(0, t, body, hc[0:1, :], unroll=8)
    o_ref[...] = (h_scr[...] * _gelu_tanh(ag_ref[...].astype(F32))).astype(o_ref.dtype)


def _lru(z, cw, cb, wcat, bcat, lam, t=256):
    n = z.shape[0]
    t = min(t, n)
    full = lambda shape: pl.BlockSpec(shape, lambda i: (0,) * len(shape))
    return pl.pallas_call(
        _lru_kernel,
        grid=(n // t,),
        in_specs=[
            pl.BlockSpec((t, LRU_W), lambda i: (i, COL_AX // LRU_W)),
            pl.BlockSpec((t, LRU_W), lambda i: (i, COL_AG // LRU_W)),
            full((CONV_W, LRU_W)), full((1, LRU_W)), full((LRU_W, 2 * LRU_W)), full((1, 2 * LRU_W)), full((1, LRU_W)),
        ],
        out_specs=pl.BlockSpec((t, LRU_W), lambda i: (i, 0)),
        out_shape=jax.ShapeDtypeStruct((n, LRU_W), BF16),
        scratch_shapes=[pltpu.VMEM((t + 8, LRU_W), F32), pltpu.VMEM((t, LRU_W), F32), pltpu.VMEM((t, LRU_W), F32),
                        pltpu.VMEM((t, LRU_W), F32), pltpu.VMEM((8, LRU_W), F32)],
        compiler_params=_cparams(1),
        name="lru",
    )(z, z, cw, cb, wcat, bcat, lam)


CA_QB = 256
CA_KW = CA_PREV * CHUNK + CA_QB


def _chunkattn_kernel(q_ref, k0_ref, k1_ref, k2_ref, v0_ref, v1_ref, v2_ref, bias_ref, o_ref):
    i = pl.program_id(0)
    q = q_ref[...]
    k = jnp.concatenate([k0_ref[...], k1_ref[...], k2_ref[...]], axis=0)
    v = jnp.concatenate([v0_ref[...], v1_ref[...], v2_ref[...]], axis=0)
    col = lax.broadcasted_iota(jnp.int32, (CA_QB, CA_KW), 1)
    in_seq = col >= (CA_KW - CA_QB) - CA_QB * i
    outs = []
    for h in range(CA_HEADS):
        sl = slice(h * CA_HD, (h + 1) * CA_HD)
        s = _dot_nt(q[:, sl], k[:, sl]) * (CA_HD ** -0.5) + bias_ref[h]
        s = jnp.where(in_seq, s, NEG_BIG)
        p = jnp.exp(s - jnp.max(s, axis=-1, keepdims=True))
        l = jnp.sum(p, axis=-1, keepdims=True)
        outs.append(_dot(p.astype(BF16), v[:, sl]) / l)
    o_ref[...] = jnp.concatenate(outs, axis=1).astype(o_ref.dtype)


def _chunk_bias(rel_table):
    r = np.arange(CA_QB)[:, None]
    c = np.arange(CA_KW)[None, :]
    idx = np.clip(r + CA_PREV * CHUNK - c, -REL_CLIP, REL_CLIP) + REL_CLIP
    cq, ck = r // CHUNK, c // CHUNK
    band = (ck >= cq) & (ck <= cq + CA_PREV)
    return jnp.where(band[None], rel_table[:, idx].astype(F32), NEG_BIG)


def _chunkattn(z, bias):
    n = z.shape[0]
    w = CA_HEADS * CA_HD
    nb = n // CA_QB
    kv = lambda col, back: pl.BlockSpec((CA_QB, w), lambda i: (jnp.maximum(i - back, 0), col // w))
    return pl.pallas_call(
        _chunkattn_kernel,
        grid=(nb,),
        in_specs=[
            pl.BlockSpec((CA_QB, w), lambda i: (i, COL_BQ // w)),
            kv(COL_BK, 2), kv(COL_BK, 1), kv(COL_BK, 0),
            kv(COL_BV, 2), kv(COL_BV, 1), kv(COL_BV, 0),
            pl.BlockSpec((CA_HEADS, CA_QB, CA_KW), lambda i: (0, 0, 0)),
        ],
        out_specs=pl.BlockSpec((CA_QB, w), lambda i: (i, 0)),
        out_shape=jax.ShapeDtypeStruct((n, w), BF16),
        compiler_params=_cparams(1),
        name="chunkattn",
    )(z, z, z, z, z, z, z, bias)


RET_T = 256
_RET_LOG_G = np.log(1.0 - 2.0 ** (-5.0 - np.arange(RET_HEADS)))


def _retention_consts(t):
    pos = np.arange(t)
    diff = pos[:, None] - pos[None, :]
    dmat = np.where(diff[None] >= 0, np.exp(_RET_LOG_G[:, None, None] * np.maximum(diff, 0)[None]), 0.0)
    dmat = dmat * (RET_HD ** -0.5)
    zeta = np.exp(_RET_LOG_G[None, :] * (t - 1 - pos)[:, None]) * (RET_HD ** -0.5)
    xi = np.exp(_RET_LOG_G[None, :] * (pos + 1)[:, None])
    rep = lambda m: np.repeat(m, RET_HD, axis=1)
    return (jnp.asarray(dmat, F32), jnp.asarray(rep(zeta), F32), jnp.asarray(rep(xi), F32))


def _retention_kernel(q_ref, k_ref, v_ref, g_ref, dmat_ref, zeta_ref, xi_ref, o_ref, s_scr):
    t = q_ref.shape[0]

    @pl.when(pl.program_id(0) == 0)
    def _():
        s_scr[...] = jnp.zeros_like(s_scr)

    q = q_ref[...]
    k = k_ref[...]
    v = v_ref[...]
    kz = (k.astype(F32) * zeta_ref[...]).astype(BF16)
    xi = xi_ref[...]
    g = g_ref[...].astype(F32)
    outs = []
    for h in range(RET_HEADS):
        sl = slice(h * RET_HD, (h + 1) * RET_HD)
        qh, vh = q[:, sl], v[:, sl]
        inner = _dot_nt(qh, k[:, sl]) * dmat_ref[h]
        state = s_scr[h]
        o = _dot(inner.astype(BF16), vh) + _dot(qh, state.astype(BF16)) * xi[:, sl]
        s_scr[h] = float(np.exp(_RET_LOG_G[h] * t)) * state + _dot_tn(kz[:, sl], vh)
        mu = jnp.mean(o, axis=-1, keepdims=True)
        d = o - mu
        var = jnp.mean(d * d, axis=-1, keepdims=True)
        outs.append(d * lax.rsqrt(var + GN_EPS))
    o_ref[...] = (jax.nn.silu(g) * jnp.concatenate(outs, axis=1)).astype(o_ref.dtype)


def _retention(z):
    n = z.shape[0]
    t = min(RET_T, n)
    w = RET_HEADS * RET_HD
    dmat, zeta, xi = _retention_consts(t)
    col = lambda c: pl.BlockSpec((t, w), lambda i: (i, c // w))
    return pl.pallas_call(
        _retention_kernel,
        grid=(n // t,),
        in_specs=[col(COL_CQ), col(COL_CK), col(COL_CV), col(COL_CG),
                  pl.BlockSpec((RET_HEADS, t, t), lambda i: (0, 0, 0)),
                  pl.BlockSpec((t, w), lambda i: (0, 0)), pl.BlockSpec((t, w), lambda i: (0, 0))],
        out_specs=pl.BlockSpec((t, w), lambda i: (i, 0)),
        out_shape=jax.ShapeDtypeStruct((n, w), BF16),
        scratch_shapes=[pltpu.VMEM((RET_HEADS, RET_HD, RET_HD), F32)],
        compiler_params=_cparams(1),
        name="retention",
    )(z, z, z, z, dmat, zeta, xi)


DA_QB = 256
DA_KB = 256
_DA_SLOPES = (2.0 ** (-8.0 * np.arange(1, DIFF_HEADS + 1) / DIFF_HEADS)).astype(np.float32)


def _diffattn_kernel(slope_ref, q_ref, k_ref, v_ref, lam_ref, g_ref, o_ref, qq_scr, m_scr, l_scr, acc_scr, *, lam_init):
    h = pl.program_id(0)
    i = pl.program_id(1)
    slope = slope_ref[h]
    q = q_ref[...].astype(F32) * (DIFF_HD ** -0.5)
    lane = lax.broadcasted_iota(jnp.int32, (DA_QB, DIFF_VD), 1)
    qq_scr[0:DA_QB, :] = jnp.where(lane < DIFF_HD, q, 0.0).astype(BF16)
    qq_scr[DA_QB:, :] = jnp.where(lane >= DIFF_HD, q, 0.0).astype(BF16)
    m_scr[...] = jnp.full_like(m_scr, NEG_BIG)
    l_scr[...] = jnp.zeros_like(l_scr)
    acc_scr[...] = jnp.zeros_like(acc_scr)
    row = lax.broadcasted_iota(jnp.int32, (2 * DA_QB, 1), 0)
    qpos = i * DA_QB + jnp.where(row >= DA_QB, row - DA_QB, row)

    def step(j, masked):
        start = pl.multiple_of(j * DA_KB, DA_KB)
        k = k_ref[pl.ds(start, DA_KB), :]
        v = v_ref[pl.ds(start, DA_KB), :]
        kpos = j * DA_KB + lax.broadcasted_iota(jnp.int32, (1, DA_KB), 1)
        s = _dot_nt(qq_scr[...], k) - slope * jnp.abs(qpos - kpos).astype(F32)
        if masked:
            s = jnp.where((kpos // CHUNK) <= (qpos // CHUNK), s, NEG_BIG)
        m_old = m_scr[...]
        m_new = jnp.maximum(m_old, jnp.max(s, axis=-1, keepdims=True))
        alpha = jnp.exp(m_old - m_new)
        p = jnp.exp(s - m_new)
        l_scr[...] = alpha * l_scr[...] + jnp.sum(p, axis=-1, keepdims=True)
        acc_scr[...] = alpha * acc_scr[...] + _dot(p.astype(BF16), v)
        m_scr[...] = m_new

    def body(j, c):
        step(j, False)
        return c

    lax.fori_loop(0, i, body, 0)
    step(i, True)

    o = acc_scr[...] / l_scr[...]
    lv = lam_ref[...]
    lam = (jnp.exp(jnp.sum(lv[0:1, :] * lv[1:2, :], axis=-1, keepdims=True))
           - jnp.exp(jnp.sum(lv[2:3, :] * lv[3:4, :], axis=-1, keepdims=True)) + lam_init)
    od = o[0:DA_QB, :] - lam * o[DA_QB:, :]
    od = od * lax.rsqrt(jnp.mean(od * od, axis=-1, keepdims=True) + GN_EPS) * g_ref[...]
    o_ref[...] = (od * (1.0 - lam_init)).astype(o_ref.dtype)


def _diffattn(z, lam_vecs, subln_g, lam_init):
    n = z.shape[0]
    assert DA_QB == DA_KB and n % DA_QB == 0
    w = DIFF_VD
    return pl.pallas_call(
        functools.partial(_diffattn_kernel, lam_init=lam_init),
        grid_spec=pltpu.PrefetchScalarGridSpec(
            num_scalar_prefetch=1,
            grid=(DIFF_HEADS, n // DA_QB),
            in_specs=[
                pl.BlockSpec((DA_QB, w), lambda h, i, s: (i, COL_DQ // w + h)),
                pl.BlockSpec((n, w), lambda h, i, s: (0, COL_DK // w + h)),
                pl.BlockSpec((n, w), lambda h, i, s: (0, COL_DV // w + h)),
                pl.BlockSpec((4, DIFF_HD), lambda h, i, s: (0, 0)),
                pl.BlockSpec((1, w), lambda h, i, s: (0, 0)),
            ],
            out_specs=pl.BlockSpec((DA_QB, w), lambda h, i, s: (i, h)),
            scratch_shapes=[pltpu.VMEM((2 * DA_QB, w), BF16), pltpu.VMEM((2 * DA_QB, 1), F32),
                            pltpu.VMEM((2 * DA_QB, 1), F32), pltpu.VMEM((2 * DA_QB, w), F32)],
        ),
        out_shape=jax.ShapeDtypeStruct((n, DIFF_HEADS * w), BF16),
        compiler_params=_cparams(2),
        name="diffattn",
    )(jnp.asarray(_DA_SLOPES), z, z, z, lam_vecs, subln_g)


def _merge_kernel(x_ref, ya_ref, yb_ref, yc_ref, yd_ref, g0_ref, g1_ref, g2_ref, g3_ref, wb_ref, wo_ref, o_ref):
    mixed = None
    for y_ref, g_ref, b in ((ya_ref, g0_ref, 0), (yb_ref, g1_ref, 1), (yc_ref, g2_ref, 2), (yd_ref, g3_ref, 3)):
        term = jax.nn.sigmoid(g_ref[...].astype(F32)) * _dot(y_ref[...], wb_ref[b])
        mixed = term if mixed is None else mixed + term
    o_ref[...] = x_ref[...] + _dot(mixed.astype(BF16), wo_ref[...])


def _merge(x, z, ya, yb, yc, yd, wb, wo, tm=512):
    n = x.shape[0]
    tm = min(tm, n)
    row = lambda w: pl.BlockSpec((tm, w), lambda i: (i, 0))
    gate = lambda b: pl.BlockSpec((tm, D_MODEL), lambda i: (i, COL_GATE // D_MODEL + b))
    return pl.pallas_call(
        _merge_kernel,
        grid=(n // tm,),
        in_specs=[row(D_MODEL), row(BRANCH_W), row(BRANCH_W), row(BRANCH_W), row(BRANCH_W),
                  gate(0), gate(1), gate(2), gate(3),
                  pl.BlockSpec((N_BRANCH, BRANCH_W, D_MODEL), lambda i: (0, 0, 0)),
                  pl.BlockSpec((D_MODEL, D_MODEL), lambda i: (0, 0))],
        out_specs=row(D_MODEL),
        out_shape=jax.ShapeDtypeStruct((n, D_MODEL), F32),
        compiler_params=_cparams(1),
        name="merge",
    )(x, ya, yb, yc, yd, z, z, z, z, wb, wo)


def _memkv_kernel(mem_ref, g_ref, wk_ref, wv_ref, k_ref, v_ref):
    mn = _rms(mem_ref[...], g_ref[...]).astype(BF16)
    k_ref[...] = _dot(mn, wk_ref[...]).astype(BF16)
    v_ref[...] = _dot(mn, wv_ref[...]).astype(BF16)


def _memkv(mem, g, wk, wv):
    m = mem.shape[0]
    w = MX_HEADS * MX_HD
    out = jax.ShapeDtypeStruct((m, w), BF16)
    return pl.pallas_call(_memkv_kernel, out_shape=(out, out), name="memkv",
                          compiler_params=pltpu.CompilerParams(vmem_limit_bytes=VMEM_LIMIT))(mem, g, wk, wv)


ROUTE_LANES = LANES


def _route(logits, carry):
    t = logits.shape[0]
    lane_i = lax.broadcasted_iota(jnp.int32, (t, ROUTE_LANES), 1)
    lane = lane_i.astype(F32)
    big = float(ROUTE_LANES)
    gl = jnp.where(lane_i < N_GROUPS, logits, NEG_BIG)
    gmax = jnp.max(gl, axis=-1, keepdims=True)
    gsum = jnp.sum(jnp.exp(gl - gmax), axis=-1, keepdims=True)
    g_sel = jnp.min(jnp.where(gl == gmax, lane, big), axis=-1, keepdims=True)
    g_prob = 1.0 / gsum
    lo = N_GROUPS + EXP_PER_GROUP * g_sel
    el = jnp.where((lane >= lo) & (lane < lo + EXP_PER_GROUP), logits, NEG_BIG)
    e1 = jnp.max(el, axis=-1, keepdims=True)
    i1 = jnp.min(jnp.where(el == e1, lane, big), axis=-1, keepdims=True)
    el2 = jnp.where(lane == i1, NEG_BIG, el)
    e2 = jnp.max(el2, axis=-1, keepdims=True)
    i2 = jnp.min(jnp.where(el2 == e2, lane, big), axis=-1, keepdims=True)
    esum = jnp.sum(jnp.exp(el - e1), axis=-1, keepdims=True)
    p1 = 1.0 / esum
    p2 = jnp.exp(e2 - e1) / esum
    w1 = p1 / (p1 + p2) * g_prob
    w2 = p2 / (p1 + p2) * g_prob
    hot1 = lane == i1
    hot2 = lane == i2
    cnt = jnp.where(hot1 | hot2, 1.0, 0.0)
    r = lax.broadcasted_iota(jnp.int32, (t, t), 0)
    c = lax.broadcasted_iota(jnp.int32, (t, t), 1)
    before = jnp.where(c < r, 1.0, 0.0).astype(BF16)
    prefix = _dot(before, cnt.astype(BF16)) + carry
    rank1 = jnp.sum(jnp.where(hot1, prefix, 0.0), axis=-1, keepdims=True)
    rank2 = jnp.sum(jnp.where(hot2, prefix, 0.0), axis=-1, keepdims=True)
    rec = jnp.zeros((t, ROUTE_LANES), F32)
    for idx, val in enumerate((i1 - N_GROUPS, i2 - N_GROUPS, w1, w2, rank1, rank2)):
        rec = jnp.where(lane_i == idx, val, rec)
    return rec, carry + jnp.sum(cnt, axis=0, keepdims=True)


def _post_kernel(x_ref, gm_ref, km_ref, vm_ref, wq_ref, wo_ref, gf_ref, wr_ref, br_ref,
                 xo_ref, t_ref, rec_ref, cnt_ref, carry):
    @pl.when(pl.program_id(0) == 0)
    def _():
        carry[...] = jnp.zeros_like(carry)

    x = x_ref[...]
    q = _dot(_rms(x, gm_ref[...]).astype(BF16), wq_ref[...]).astype(BF16)
    km = km_ref[...]
    vm = vm_ref[...]
    outs = []
    for h in range(MX_HEADS):
        sl = slice(h * MX_HD, (h + 1) * MX_HD)
        s = _dot_nt(q[:, sl], km[:, sl]) * (MX_HD ** -0.5)
        p = jnp.exp(s - jnp.max(s, axis=-1, keepdims=True))
        p = p / jnp.sum(p, axis=-1, keepdims=True)
        outs.append(_dot(p.astype(BF16), vm[:, sl]))
    x = x + _dot(jnp.concatenate(outs, axis=1).astype(BF16), wo_ref[...])
    xo_ref[...] = x
    t = _rms(x, gf_ref[...])
    t_ref[...] = t
    logits = jnp.dot(t, wr_ref[...], preferred_element_type=F32, precision=lax.Precision.HIGHEST) + br_ref[...]
    rec, new_carry = _route(logits, carry[0:1, :])
    rec_ref[...] = rec
    carry[0:1, :] = new_carry
    cnt_ref[...] = jnp.broadcast_to(new_carry, cnt_ref.shape)


def _post(x, gm, km, vm, wq, wo, gf, wr, br, tm=256):
    n = x.shape[0]
    tm = min(tm, n)
    w = MX_HEADS * MX_HD
    m = km.shape[0]
    row = lambda wd: pl.BlockSpec((tm, wd), lambda i: (i, 0))
    full = lambda shape: pl.BlockSpec(shape, lambda i: (0,) * len(shape))
    return pl.pallas_call(
        _post_kernel,
        grid=(n // tm,),
        in_specs=[row(D_MODEL), full((1, D_MODEL)), full((m, w)), full((m, w)), full((D_MODEL, w)), full((w, D_MODEL)),
                  full((1, D_MODEL)), full((D_MODEL, ROUTE_LANES)), full((1, ROUTE_LANES))],
        out_specs=[row(D_MODEL), row(D_MODEL), row(ROUTE_LANES), full((8, ROUTE_LANES))],
        out_shape=[jax.ShapeDtypeStruct((n, D_MODEL), F32), jax.ShapeDtypeStruct((n, D_MODEL), F32),
                   jax.ShapeDtypeStruct((n, ROUTE_LANES), F32), jax.ShapeDtypeStruct((8, ROUTE_LANES), F32)],
        scratch_shapes=[pltpu.VMEM((8, ROUTE_LANES), F32)],
        compiler_params=_cparams(1),
        name="post",
    )(x, gm, km, vm, wq, wo, gf, wr, br)


MOE_TB = 256


def _row_copy(src, src_row, dst, dst_row, sem):
    return pltpu.make_async_copy(src.at[pl.ds(src_row, 1), :], dst.at[pl.ds(dst_row, 1), :], sem)


def _dispatch_kernel(dest_ref, t_ref, xs_in_ref, xs_ref, sem):
    del xs_in_ref
    base = pl.program_id(0) * MOE_TB

    def issue(r, c):
        for k in range(2):
            _row_copy(t_ref, r, xs_ref, dest_ref[2 * (base + r) + k], sem).start()
        return c

    lax.fori_loop(0, MOE_TB, issue, 0)

    def drain(r, c):
        for k in range(2):
            _row_copy(t_ref, r, xs_ref, dest_ref[2 * (base + r) + k], sem).wait()
        return c

    lax.fori_loop(0, MOE_TB, drain, 0)


def _dispatch(dest, t, n_slots):
    n = t.shape[0]
    xs0 = jnp.zeros((n_slots, D_MODEL), F32)
    return pl.pallas_call(
        _dispatch_kernel,
        grid_spec=pltpu.PrefetchScalarGridSpec(
            num_scalar_prefetch=1,
            grid=(n // MOE_TB,),
            in_specs=[pl.BlockSpec((MOE_TB, D_MODEL), lambda i, d: (i, 0)), pl.BlockSpec(memory_space=pl.ANY)],
            out_specs=pl.BlockSpec(memory_space=pl.ANY),
            scratch_shapes=[pltpu.SemaphoreType.DMA(())],
        ),
        out_shape=jax.ShapeDtypeStruct((n_slots, D_MODEL), F32),
        input_output_aliases={2: 0},
        compiler_params=_cparams(1),
        name="dispatch",
    )(dest, t, xs0)


def _expert_kernel(be_ref, na_ref, xs_ref, w1_ref, w3_ref, w2_ref, ys_ref):
    active = pl.program_id(0) < na_ref[0]

    @pl.when(active)
    def _():
        x = xs_ref[...].astype(BF16)
        a = _dot(x, w1_ref[0].astype(BF16))
        b = _dot(x, w3_ref[0].astype(BF16))
        ys_ref[...] = _dot((jax.nn.silu(a) * b).astype(BF16), w2_ref[0].astype(BF16))

    @pl.when(jnp.logical_not(active))
    def _():
        ys_ref[...] = jnp.zeros_like(ys_ref)


def _experts(block_e, n_active, xs, w1, w3, w2):
    n_blocks = xs.shape[0] // MOE_BLOCK
    blk = lambda i, be, na: jnp.minimum(i, na[0] - 1)
    return pl.pallas_call(
        _expert_kernel,
        grid_spec=pltpu.PrefetchScalarGridSpec(
            num_scalar_prefetch=2,
            grid=(n_blocks,),
            in_specs=[
                pl.BlockSpec((MOE_BLOCK, D_MODEL), lambda i, be, na: (blk(i, be, na), 0)),
                pl.BlockSpec((1, D_MODEL, D_EXPERT), lambda i, be, na: (be[blk(i, be, na)], 0, 0)),
                pl.BlockSpec((1, D_MODEL, D_EXPERT), lambda i, be, na: (be[blk(i, be, na)], 0, 0)),
                pl.BlockSpec((1, D_EXPERT, D_MODEL), lambda i, be, na: (be[blk(i, be, na)], 0, 0)),
            ],
            out_specs=pl.BlockSpec((MOE_BLOCK, D_MODEL), lambda i, be, na: (i, 0)),
        ),
        out_shape=jax.ShapeDtypeStruct(xs.shape, F32),
        compiler_params=_cparams(1),
        name="experts",
    )(block_e, n_active, xs, w1, w3, w2)


def _combine_kernel(dest_ref, x_ref, rec_ref, g_ref, ys_ref, o_ref, rows, sem, *, final):
    base = pl.program_id(0) * MOE_TB

    def issue(r, c):
        for k in range(2):
            _row_copy(ys_ref, dest_ref[2 * (base + r) + k], rows.at[k], r, sem).start()
        return c

    lax.fori_loop(0, MOE_TB, issue, 0)

    def drain(r, c):
        for k in range(2):
            _row_copy(ys_ref, dest_ref[2 * (base + r) + k], rows.at[k], r, sem).wait()
        return c

    lax.fori_loop(0, MOE_TB, drain, 0)
    rec = rec_ref[...]
    x = x_ref[...] + rec[:, 2:3] * rows[0] + rec[:, 3:4] * rows[1]
    o_ref[...] = _rms(x, g_ref[...]) if final else x


def _combine(dest, x, rec, ys, g, final):
    n = x.shape[0]
    row = lambda wd: pl.BlockSpec((MOE_TB, wd), lambda i, d: (i, 0))
    return pl.pallas_call(
        functools.partial(_combine_kernel, final=final),
        grid_spec=pltpu.PrefetchScalarGridSpec(
            num_scalar_prefetch=1,
            grid=(n // MOE_TB,),
            in_specs=[row(D_MODEL), row(ROUTE_LANES), pl.BlockSpec((1, D_MODEL), lambda i, d: (0, 0)),
                      pl.BlockSpec(memory_space=pl.ANY)],
            out_specs=row(D_MODEL),
            scratch_shapes=[pltpu.VMEM((2, MOE_TB, D_MODEL), F32), pltpu.SemaphoreType.DMA(())],
        ),
        out_shape=jax.ShapeDtypeStruct((n, D_MODEL), F32),
        compiler_params=_cparams(1),
        name="combine",
    )(dest, x, rec, g, ys)


def _moe(x, t, rec, cnt, w1, w3, w2, g_final, final):
    n = x.shape[0]
    n_asg = 2 * n
    n_blocks = -(-(n_asg + N_EXPERTS * (MOE_BLOCK - 1)) // MOE_BLOCK)
    counts = cnt[0, N_GROUPS:N_GROUPS + N_EXPERTS].astype(jnp.int32)
    padded = (counts + MOE_BLOCK - 1) // MOE_BLOCK * MOE_BLOCK
    pad_end = jnp.cumsum(padded)
    pad_start = pad_end - padded
    expert = rec[:, 0:2].astype(jnp.int32)
    dest = (pad_start[expert] + rec[:, 4:6].astype(jnp.int32)).reshape(-1)
    block_e = jnp.minimum(jnp.searchsorted(pad_end, jnp.arange(n_blocks) * MOE_BLOCK, side="right"),
                          N_EXPERTS - 1).astype(jnp.int32)
    n_active = (pad_end[-1:] // MOE_BLOCK).astype(jnp.int32)
    xs = _dispatch(dest, t, n_blocks * MOE_BLOCK)
    ys = _experts(block_e, n_active, xs, w1, w3, w2)
    return _combine(dest, x, rec, ys, g_final, final)


def _block_diag(w):
    eye = jnp.eye(LRU_BLOCKS, dtype=w.dtype)
    return jnp.einsum("kcd,kj->kcjd", w, eye).reshape(LRU_W, LRU_W)


def _router_weights(w_group, b_group, w_router, b_router):
    pad = ROUTE_LANES - N_GROUPS - N_EXPERTS
    wr = jnp.concatenate([w_group, w_router, jnp.zeros((D_MODEL, pad), F32)], axis=1)
    br = jnp.concatenate([b_group, b_router, jnp.zeros((pad,), F32)])[None, :]
    return wr, br


def kernel(x, mem, norm_mix, w_in, conv_w, conv_b, lru_wa, lru_ba, lru_wx, lru_bx, lru_lambda, ca_rel_bias, diff_lambda, diff_subln, w_branch, w_out, norm_mem, mem_norm, w_mq, w_mk, w_mv, w_mo, norm_ffn, w_group, b_group, w_router, b_router, w1, w3, w2, final_norm):
    b, n, d = x.shape
    assert b == 1 and d == D_MODEL
    xs = x[0]
    for l in range(DEPTH):
        lam_init = 0.8 - 0.6 * math.exp(-0.3 * l)
        z = _inproj(xs, norm_mix[l][None, :], w_in[l].astype(BF16))
        wcat = jnp.concatenate([_block_diag(lru_wa[l]), _block_diag(lru_wx[l])], axis=1).astype(BF16)
        bcat = jnp.concatenate([lru_ba[l], lru_bx[l]])[None, :]
        ya = _lru(z, conv_w[l], conv_b[l][None, :], wcat, bcat, lru_lambda[l][None, :])
        yb = _chunkattn(z, _chunk_bias(ca_rel_bias[l]))
        yc = _retention(z)
        yd = _diffattn(z, diff_lambda[l], diff_subln[l][None, :], lam_init)
        x1 = _merge(xs, z, ya, yb, yc, yd, w_branch[l].astype(BF16), w_out[l].astype(BF16))
        km, vm = _memkv(mem[0], mem_norm[None, :], w_mk[l].astype(BF16), w_mv[l].astype(BF16))
        wr, br = _router_weights(w_group[l], b_group[l], w_router[l], b_router[l])
        x2, t, rec, cnt = _post(x1, norm_mem[l][None, :], km, vm, w_mq[l].astype(BF16), w_mo[l].astype(BF16),
                                norm_ffn[l][None, :], wr, br)
        xs = _moe(x2, t, rec, cnt, w1[l], w3[l], w2[l], final_norm[None, :], final=(l == DEPTH - 1))
    return xs[None]
```

```python
import functools
import math

import numpy as np
import jax
import jax.numpy as jnp
from jax import lax
from jax.experimental import pallas as pl
from jax.experimental.pallas import tpu as pltpu

F32 = jnp.float32
BF16 = jnp.bfloat16

D_MODEL = 1024
DEPTH = 2
CHUNK = 64
NORM_EPS = 1e-6
GN_EPS = 1e-5
LRU_W = 512
LRU_BLOCKS = 8
LRU_BW = LRU_W // LRU_BLOCKS
CONV_W = 4
LRU_C = 8.0
CA_HEADS = 8
CA_HD = 64
CA_PREV = 8
REL_CLIP = 128
RET_HEADS = 8
RET_HD = 64
DIFF_HEADS = 4
DIFF_HD = 64
DIFF_VD = 2 * DIFF_HD
MX_HEADS = 4
MX_HD = 128
N_GROUPS = 4
EXP_PER_GROUP = 8
N_EXPERTS = N_GROUPS * EXP_PER_GROUP
D_EXPERT = 512
MOE_BLOCK = 256
N_BRANCH = 4
BRANCH_W = 512

COL_AX, COL_AG = 0, 512
COL_BQ, COL_BK, COL_BV = 1024, 1536, 2048
COL_CQ, COL_CK, COL_CV, COL_CG = 2560, 3072, 3584, 4096
COL_DQ, COL_DK, COL_DV = 4608, 5120, 5632
COL_GATE = 6144
IN_COLS = 10240

LANES = 128
NEG_BIG = -1e30
LOG2E = 1.0 / math.log(2.0)
VMEM_LIMIT = 56 * 1024 * 1024


def _cparams(n_axes):
    return pltpu.CompilerParams(dimension_semantics=("arbitrary",) * n_axes, vmem_limit_bytes=VMEM_LIMIT)


def _rms(x, g):
    return x * lax.rsqrt(jnp.mean(x * x, axis=-1, keepdims=True) + NORM_EPS) * g


def _dot(a, b):
    return jnp.dot(a, b, preferred_element_type=F32)


def _dot_nt(a, b):
    return lax.dot_general(a, b, (((1,), (1,)), ((), ())), preferred_element_type=F32)


def _dot_tn(a, b):
    return lax.dot_general(a, b, (((0,), (0,)), ((), ())), preferred_element_type=F32)


def _inproj_kernel(x_ref, g_ref, w_ref, o_ref, h_scr):
    @pl.when(pl.program_id(1) == 0)
    def _():
        h_scr[...] = _rms(x_ref[...], g_ref[...]).astype(BF16)

    o_ref[...] = _dot(h_scr[...], w_ref[...]).astype(o_ref.dtype)


def _inproj(x, g, w, tm=512, tn=1024):
    n = x.shape[0]
    tm = min(tm, n)
    return pl.pallas_call(
        _inproj_kernel,
        grid=(n // tm, IN_COLS // tn),
        in_specs=[
            pl.BlockSpec((tm, D_MODEL), lambda i, j: (i, 0)),
            pl.BlockSpec((1, D_MODEL), lambda i, j: (0, 0)),
            pl.BlockSpec((D_MODEL, tn), lambda i, j: (0, j)),
        ],
        out_specs=pl.BlockSpec((tm, tn), lambda i, j: (i, j)),
        out_shape=jax.ShapeDtypeStruct((n, IN_COLS), BF16),
        scratch_shapes=[pltpu.VMEM((tm, D_MODEL), BF16)],
        compiler_params=_cparams(2),
        name="inproj",
    )(x, g, w)


def _gelu_tanh(x):
    return 0.5 * x * (1.0 + jnp.tanh(math.sqrt(2.0 / math.pi) * (x + 0.044715 * x * x * x)))


def _lru_kernel(ax_ref, ag_ref, cw_ref, cb_ref, w_ref, b_ref, lam_ref, o_ref, xbuf, a_scr, u_scr, h_scr, hc):
    t = ax_ref.shape[0]

    @pl.when(pl.program_id(0) == 0)
    def _():
        xbuf[0:8, :] = jnp.zeros((8, LRU_W), F32)
        hc[...] = jnp.zeros_like(hc)

    x = ax_ref[...].astype(F32)
    xbuf[8:8 + t, :] = x
    xc = (cb_ref[...] + cw_ref[3:4, :] * x + cw_ref[2:3, :] * xbuf[7:7 + t, :]
          + cw_ref[1:2, :] * xbuf[6:6 + t, :] + cw_ref[0:1, :] * xbuf[5:5 + t, :])
    xbuf[0:8, :] = xbuf[t:t + 8, :]
    ri = _dot(xc.astype(BF16), w_ref[...]) + b_ref[...]
    r = jax.nn.sigmoid(ri[:, :LRU_W])
    ig = jax.nn.sigmoid(ri[:, LRU_W:])
    nl = -lam_ref[...]
    softplus = jnp.maximum(nl, 0.0) + jnp.log(1.0 + jnp.exp(-jnp.abs(nl)))
    log_a = -LRU_C * r * softplus
    a = jnp.exp(log_a)
    a_scr[...] = a
    u_scr[...] = jnp.sqrt(-jnp.tanh(log_a) * (a * a + 1.0)) * (ig * xc)

    def body(s, h):
        h = a_scr[pl.ds(s, 1), :] * h + u_scr[pl.ds(s, 1), :]
        h_scr[pl.ds(s, 1), :] = h
        return h

    hc[0:1, :] = lax.fori_loop(0, t, body, hc[0:1, :], unroll=8)
    o_ref[...] = (h_scr[...] * _gelu_tanh(ag_ref[...].astype(F32))).astype(o_ref.dtype)


def _lru(z, cw, cb, wcat, bcat, lam, t=256):
    n = z.shape[0]
    t = min(t, n)
    full = lambda shape: pl.BlockSpec(shape, lambda i: (0,) * len(shape))
    return pl.pallas_call(
        _lru_kernel,
        grid=(n // t,),
        in_specs=[
            pl.BlockSpec((t, LRU_W), lambda i: (i, COL_AX // LRU_W)),
            pl.BlockSpec((t, LRU_W), lambda i: (i, COL_AG // LRU_W)),
            full((CONV_W, LRU_W)), full((1, LRU_W)), full((LRU_W, 2 * LRU_W)), full((1, 2 * LRU_W)), full((1, LRU_W)),
        ],
        out_specs=pl.BlockSpec((t, LRU_W), lambda i: (i, 0)),
        out_shape=jax.ShapeDtypeStruct((n, LRU_W), BF16),
        scratch_shapes=[pltpu.VMEM((t + 8, LRU_W), F32), pltpu.VMEM((t, LRU_W), F32), pltpu.VMEM((t, LRU_W), F32),
                        pltpu.VMEM((t, LRU_W), F32), pltpu.VMEM((8, LRU_W), F32)],
        compiler_params=_cparams(1),
        name="lru",
    )(z, z, cw, cb, wcat, bcat, lam)


CA_QB = 256
CA_KW = CA_PREV * CHUNK + CA_QB


def _chunkattn_kernel(q_ref, k0_ref, k1_ref, k2_ref, v0_ref, v1_ref, v2_ref, bias_ref, o_ref):
    i = pl.program_id(0)
    q = q_ref[...]
    k = jnp.concatenate([k0_ref[...], k1_ref[...], k2_ref[...]], axis=0)
    v = jnp.concatenate([v0_ref[...], v1_ref[...], v2_ref[...]], axis=0)
    col = lax.broadcasted_iota(jnp.int32, (CA_QB, CA_KW), 1)
    in_seq = col >= (CA_KW - CA_QB) - CA_QB * i
    outs = []
    for h in range(CA_HEADS):
        sl = slice(h * CA_HD, (h + 1) * CA_HD)
        s = _dot_nt(q[:, sl], k[:, sl]) * (CA_HD ** -0.5) + bias_ref[h]
        s = jnp.where(in_seq, s, NEG_BIG)
        p = jnp.exp(s - jnp.max(s, axis=-1, keepdims=True))
        l = jnp.sum(p, axis=-1, keepdims=True)
        outs.append(_dot(p.astype(BF16), v[:, sl]) / l)
    o_ref[...] = jnp.concatenate(outs, axis=1).astype(o_ref.dtype)


def _chunk_bias(rel_table):
    r = np.arange(CA_QB)[:, None]
    c = np.arange(CA_KW)[None, :]
    idx = np.clip(r + CA_PREV * CHUNK - c, -REL_CLIP, REL_CLIP) + REL_CLIP
    cq, ck = r // CHUNK, c // CHUNK
    band = (ck >= cq) & (ck <= cq + CA_PREV)
    return jnp.where(band[None], rel_table[:, idx].astype(F32), NEG_BIG)


def _chunkattn(z, bias):
    n = z.shape[0]
    w = CA_HEADS * CA_HD
    nb = n // CA_QB
    kv = lambda col, back: pl.BlockSpec((CA_QB, w), lambda i: (jnp.maximum(i - back, 0), col // w))
    return pl.pallas_call(
        _chunkattn_kernel,
        grid=(nb,),
        in_specs=[
            pl.BlockSpec((CA_QB, w), lambda i: (i, COL_BQ // w)),
            kv(COL_BK, 2), kv(COL_BK, 1), kv(COL_BK, 0),
            kv(COL_BV, 2), kv(COL_BV, 1), kv(COL_BV, 0),
            pl.BlockSpec((CA_HEADS, CA_QB, CA_KW), lambda i: (0, 0, 0)),
        ],
        out_specs=pl.BlockSpec((CA_QB, w), lambda i: (i, 0)),
        out_shape=jax.ShapeDtypeStruct((n, w), BF16),
        compiler_params=_cparams(1),
        name="chunkattn",
    )(z, z, z, z, z, z, z, bias)


RET_T = 256
_RET_LOG_G = np.log(1.0 - 2.0 ** (-5.0 - np.arange(RET_HEADS)))


def _retention_consts(t):
    pos = np.arange(t)
    diff = pos[:, None] - pos[None, :]
    dmat = np.where(diff[None] >= 0, np.exp(_RET_LOG_G[:, None, None] * np.maximum(diff, 0)[None]), 0.0)
    dmat = dmat * (RET_HD ** -0.5)
    zeta = np.exp(_RET_LOG_G[None, :] * (t - 1 - pos)[:, None]) * (RET_HD ** -0.5)
    xi = np.exp(_RET_LOG_G[None, :] * (pos + 1)[:, None])
    rep = lambda m: np.repeat(m, RET_HD, axis=1)
    return (jnp.asarray(dmat, F32), jnp.asarray(rep(zeta), F32), jnp.asarray(rep(xi), F32))


def _retention_kernel(q_ref, k_ref, v_ref, g_ref, dmat_ref, zeta_ref, xi_ref, o_ref, s_scr):
    t = q_ref.shape[0]

    @pl.when(pl.program_id(0) == 0)
    def _():
        s_scr[...] = jnp.zeros_like(s_scr)

    q = q_ref[...]
    k = k_ref[...]
    v = v_ref[...]
    kz = (k.astype(F32) * zeta_ref[...]).astype(BF16)
    xi = xi_ref[...]
    g = g_ref[...].astype(F32)
    outs = []
    for h in range(RET_HEADS):
        sl = slice(h * RET_HD, (h + 1) * RET_HD)
        qh, vh = q[:, sl], v[:, sl]
        inner = _dot_nt(qh, k[:, sl]) * dmat_ref[h]
        state = s_scr[h]
        o = _dot(inner.astype(BF16), vh) + _dot(qh, state.astype(BF16)) * xi[:, sl]
        s_scr[h] = float(np.exp(_RET_LOG_G[h] * t)) * state + _dot_tn(kz[:, sl], vh)
        mu = jnp.mean(o, axis=-1, keepdims=True)
        d = o - mu
        var = jnp.mean(d * d, axis=-1, keepdims=True)
        outs.append(d * lax.rsqrt(var + GN_EPS))
    o_ref[...] = (jax.nn.silu(g) * jnp.concatenate(outs, axis=1)).astype(o_ref.dtype)


def _retention(z):
    n = z.shape[0]
    t = min(RET_T, n)
    w = RET_HEADS * RET_HD
    dmat, zeta, xi = _retention_consts(t)
    col = lambda c: pl.BlockSpec((t, w), lambda i: (i, c // w))
    return pl.pallas_call(
        _retention_kernel,
        grid=(n // t,),
        in_specs=[col(COL_CQ), col(COL_CK), col(COL_CV), col(COL_CG),
                  pl.BlockSpec((RET_HEADS, t, t), lambda i: (0, 0, 0)),
                  pl.BlockSpec((t, w), lambda i: (0, 0)), pl.BlockSpec((t, w), lambda i: (0, 0))],
        out_specs=pl.BlockSpec((t, w), lambda i: (i, 0)),
        out_shape=jax.ShapeDtypeStruct((n, w), BF16),
        scratch_shapes=[pltpu.VMEM((RET_HEADS, RET_HD, RET_HD), F32)],
        compiler_params=_cparams(1),
        name="retention",
    )(z, z, z, z, dmat, zeta, xi)


DA_B = 256
_DA_SLOPES = (2.0 ** (-8.0 * np.arange(1, DIFF_HEADS + 1) / DIFF_HEADS)).astype(np.float32)


def _diffattn_consts():
    kloc = np.arange(DA_B)[:, None]
    qloc = (np.arange(2 * DA_B) % DA_B)[None, :]
    slopes = _DA_SLOPES[:, None, None].astype(np.float64)
    base = np.broadcast_to(LOG2E * slopes * kloc, (DIFF_HEADS, DA_B, 2 * DA_B))
    diag = np.where((kloc // CHUNK) <= (qloc // CHUNK), LOG2E * slopes * (qloc - np.abs(qloc - kloc)), NEG_BIG)
    return jnp.asarray(base, F32), jnp.asarray(diag, F32)


def _diffattn_kernel(slope_ref, q_ref, k_ref, v_ref, base_ref, diag_ref, lam_ref, g_ref, o_ref,
                     qq_scr, m_scr, l_scr, acc_scr, s_scr, p_scr, a_scr, *, lam_init):
    h = pl.program_id(0)
    i = pl.program_id(1)
    blk = DA_B
    qt = (q_ref[...].astype(F32) * (LOG2E * DIFF_HD ** -0.5)).T
    feat = lax.broadcasted_iota(jnp.int32, (DIFF_VD, blk), 0)
    qq_scr[:, 0:blk] = jnp.where(feat < DIFF_HD, qt, 0.0).astype(BF16)
    qq_scr[:, blk:] = jnp.where(feat >= DIFF_HD, qt, 0.0).astype(BF16)
    m_scr[...] = jnp.full_like(m_scr, NEG_BIG)
    l_scr[...] = jnp.zeros_like(l_scr)
    acc_scr[...] = jnp.zeros_like(acc_scr)

    def rows(ref, j):
        return ref[pl.ds(pl.multiple_of(j * blk, blk), blk), :]

    def scores(j):
        return _dot(rows(k_ref, j), qq_scr[...])

    def softmax_step(j, s, bias):
        offset = slope_ref[h] * LOG2E * (j * blk).astype(F32)
        t = s + bias
        m_old = m_scr[...]
        m_new = jnp.maximum(m_old, jnp.max(t, axis=0, keepdims=True) + offset)
        alpha = jnp.exp2(m_old - m_new)
        p = jnp.exp2(t - (m_new - offset))
        l_scr[...] = alpha * l_scr[...] + jnp.sum(p, axis=0, keepdims=True)
        m_scr[...] = m_new
        return p.astype(BF16), alpha

    def accumulate(j, p, alpha):
        acc_scr[...] = alpha * acc_scr[...] + _dot_tn(rows(v_ref, j), p)

    def stage(j, slot):
        s_scr[1 - slot] = scores(j + 1)
        accumulate(jnp.maximum(j - 1, 0), p_scr[1 - slot], a_scr[1 - slot])
        p_scr[slot], a_scr[slot] = softmax_step(j, s_scr[slot], base_ref[0])

    def last_stage(slot):
        accumulate(jnp.maximum(i - 1, 0), p_scr[1 - slot], a_scr[1 - slot])
        p, alpha = softmax_step(i, s_scr[slot], diag_ref[0])
        accumulate(i, p, alpha)

    s_scr[0] = scores(0)
    p_scr[1] = jnp.zeros((blk, 2 * blk), BF16)
    a_scr[1] = jnp.ones((1, 2 * blk), F32)

    def body(jj, c):
        stage(2 * jj, 0)
        stage(2 * jj + 1, 1)
        return c

    lax.fori_loop(0, jnp.right_shift(i, 1), body, 0)
    odd = jnp.bitwise_and(i, 1) == 1

    @pl.when(odd)
    def _():
        stage(i - 1, 0)
        last_stage(1)

    @pl.when(jnp.logical_not(odd))
    def _():
        last_stage(0)

    o = acc_scr[...] / l_scr[...]
    lv = lam_ref[...]
    lam = (jnp.exp(jnp.sum(lv[0:1, :] * lv[1:2, :], axis=-1, keepdims=True))
           - jnp.exp(jnp.sum(lv[2:3, :] * lv[3:4, :], axis=-1, keepdims=True)) + lam_init)
    od = o[:, 0:blk] - lam * o[:, blk:]
    od = od * lax.rsqrt(jnp.mean(od * od, axis=0, keepdims=True) + GN_EPS)
    o_ref[...] = (od.T * g_ref[...] * (1.0 - lam_init)).astype(o_ref.dtype)


def _diffattn(z, lam_vecs, subln_g, lam_init):
    n = z.shape[0]
    assert n % DA_B == 0
    w = DIFF_VD
    base, diag = _diffattn_consts()
    return pl.pallas_call(
        functools.partial(_diffattn_kernel, lam_init=lam_init),
        grid_spec=pltpu.PrefetchScalarGridSpec(
            num_scalar_prefetch=1,
            grid=(DIFF_HEADS, n // DA_B),
            in_specs=[
                pl.BlockSpec((DA_B, w), lambda h, i, s: (i, COL_DQ // w + h)),
                pl.BlockSpec((n, w), lambda h, i, s: (0, COL_DK // w + h)),
                pl.BlockSpec((n, w), lambda h, i, s: (0, COL_DV // w + h)),
                pl.BlockSpec((1, DA_B, 2 * DA_B), lambda h, i, s: (h, 0, 0)),
                pl.BlockSpec((1, DA_B, 2 * DA_B), lambda h, i, s: (h, 0, 0)),
                pl.BlockSpec((4, DIFF_HD), lambda h, i, s: (0, 0)),
                pl.BlockSpec((1, w), lambda h, i, s: (0, 0)),
            ],
            out_specs=pl.BlockSpec((DA_B, w), lambda h, i, s: (i, h)),
            scratch_shapes=[pltpu.VMEM((w, 2 * DA_B), BF16), pltpu.VMEM((1, 2 * DA_B), F32),
                            pltpu.VMEM((1, 2 * DA_B), F32), pltpu.VMEM((w, 2 * DA_B), F32),
                            pltpu.VMEM((2, DA_B, 2 * DA_B), F32), pltpu.VMEM((2, DA_B, 2 * DA_B), BF16),
                            pltpu.VMEM((2, 1, 2 * DA_B), F32)],
        ),
        out_shape=jax.ShapeDtypeStruct((n, DIFF_HEADS * w), BF16),
        compiler_params=_cparams(2),
        name="diffattn",
    )(jnp.asarray(_DA_SLOPES), z, z, z, base, diag, lam_vecs, subln_g)


def _merge_kernel(x_ref, ya_ref, yb_ref, yc_ref, yd_ref, g0_ref, g1_ref, g2_ref, g3_ref, wb_ref, wo_ref, o_ref):
    mixed = None
    for y_ref, g_ref, b in ((ya_ref, g0_ref, 0), (yb_ref, g1_ref, 1), (yc_ref, g2_ref, 2), (yd_ref, g3_ref, 3)):
        term = jax.nn.sigmoid(g_ref[...].astype(F32)) * _dot(y_ref[...], wb_ref[b])
        mixed = term if mixed is None else mixed + term
    o_ref[...] = x_ref[...] + _dot(mixed.astype(BF16), wo_ref[...])


def _merge(x, z, ya, yb, yc, yd, wb, wo, tm=512):
    n = x.shape[0]
    tm = min(tm, n)
    row = lambda w: pl.BlockSpec((tm, w), lambda i: (i, 0))
    gate = lambda b: pl.BlockSpec((tm, D_MODEL), lambda i: (i, COL_GATE // D_MODEL + b))
    return pl.pallas_call(
        _merge_kernel,
        grid=(n // tm,),
        in_specs=[row(D_MODEL), row(BRANCH_W), row(BRANCH_W), row(BRANCH_W), row(BRANCH_W),
                  gate(0), gate(1), gate(2), gate(3),
                  pl.BlockSpec((N_BRANCH, BRANCH_W, D_MODEL), lambda i: (0, 0, 0)),
                  pl.BlockSpec((D_MODEL, D_MODEL), lambda i: (0, 0))],
        out_specs=row(D_MODEL),
        out_shape=jax.ShapeDtypeStruct((n, D_MODEL), F32),
        compiler_params=_cparams(1),
        name="merge",
    )(x, ya, yb, yc, yd, z, z, z, z, wb, wo)


def _memkv_kernel(mem_ref, g_ref, wk_ref, wv_ref, k_ref, v_ref):
    mn = _rms(mem_ref[...], g_ref[...]).astype(BF16)
    k_ref[...] = _dot(mn, wk_ref[...]).astype(BF16)
    v_ref[...] = _dot(mn, wv_ref[...]).astype(BF16)


def _memkv(mem, g, wk, wv):
    m = mem.shape[0]
    w = MX_HEADS * MX_HD
    out = jax.ShapeDtypeStruct((m, w), BF16)
    return pl.pallas_call(_memkv_kernel, out_shape=(out, out), name="memkv",
                          compiler_params=pltpu.CompilerParams(vmem_limit_bytes=VMEM_LIMIT))(mem, g, wk, wv)


ROUTE_LANES = LANES


def _route(logits, carry):
    t = logits.shape[0]
    lane_i = lax.broadcasted_iota(jnp.int32, (t, ROUTE_LANES), 1)
    lane = lane_i.astype(F32)
    big = float(ROUTE_LANES)
    gl = jnp.where(lane_i < N_GROUPS, logits, NEG_BIG)
    gmax = jnp.max(gl, axis=-1, keepdims=True)
    gsum = jnp.sum(jnp.exp(gl - gmax), axis=-1, keepdims=True)
    g_sel = jnp.min(jnp.where(gl == gmax, lane, big), axis=-1, keepdims=True)
    g_prob = 1.0 / gsum
    lo = N_GROUPS + EXP_PER_GROUP * g_sel
    el = jnp.where((lane >= lo) & (lane < lo + EXP_PER_GROUP), logits, NEG_BIG)
    e1 = jnp.max(el, axis=-1, keepdims=True)
    i1 = jnp.min(jnp.where(el == e1, lane, big), axis=-1, keepdims=True)
    el2 = jnp.where(lane == i1, NEG_BIG, el)
    e2 = jnp.max(el2, axis=-1, keepdims=True)
    i2 = jnp.min(jnp.where(el2 == e2, lane, big), axis=-1, keepdims=True)
    esum = jnp.sum(jnp.exp(el - e1), axis=-1, keepdims=True)
    p1 = 1.0 / esum
    p2 = jnp.exp(e2 - e1) / esum
    w1 = p1 / (p1 + p2) * g_prob
    w2 = p2 / (p1 + p2) * g_prob
    hot1 = lane == i1
    hot2 = lane == i2
    cnt = jnp.where(hot1 | hot2, 1.0, 0.0)
    r = lax.broadcasted_iota(jnp.int32, (t, t), 0)
    c = lax.broadcasted_iota(jnp.int32, (t, t), 1)
    before = jnp.where(c < r, 1.0, 0.0).astype(BF16)
    prefix = _dot(before, cnt.astype(BF16)) + carry
    rank1 = jnp.sum(jnp.where(hot1, prefix, 0.0), axis=-1, keepdims=True)
    rank2 = jnp.sum(jnp.where(hot2, prefix, 0.0), axis=-1, keepdims=True)
    rec = jnp.zeros((t, ROUTE_LANES), F32)
    for idx, val in enumerate((i1 - N_GROUPS, i2 - N_GROUPS, w1, w2, rank1, rank2)):
        rec = jnp.where(lane_i == idx, val, rec)
    return rec, carry + jnp.sum(cnt, axis=0, keepdims=True)


def _post_kernel(x_ref, gm_ref, km_ref, vm_ref, wq_ref, wo_ref, gf_ref, wr_ref, br_ref,
                 xo_ref, t_ref, rec_ref, cnt_ref, carry):
    @pl.when(pl.program_id(0) == 0)
    def _():
        carry[...] = jnp.zeros_like(carry)

    x = x_ref[...]
    q = _dot(_rms(x, gm_ref[...]).astype(BF16), wq_ref[...]).astype(BF16)
    km = km_ref[...]
    vm = vm_ref[...]
    outs = []
    for h in range(MX_HEADS):
        sl = slice(h * MX_HD, (h + 1) * MX_HD)
        s = _dot_nt(q[:, sl], km[:, sl]) * (MX_HD ** -0.5)
        p = jnp.exp(s - jnp.max(s, axis=-1, keepdims=True))
        p = p / jnp.sum(p, axis=-1, keepdims=True)
        outs.append(_dot(p.astype(BF16), vm[:, sl]))
    x = x + _dot(jnp.concatenate(outs, axis=1).astype(BF16), wo_ref[...])
    xo_ref[...] = x
    t = _rms(x, gf_ref[...])
    t_ref[...] = t
    logits = jnp.dot(t, wr_ref[...], preferred_element_type=F32, precision=lax.Precision.HIGHEST) + br_ref[...]
    rec, new_carry = _route(logits, carry[0:1, :])
    rec_ref[...] = rec
    carry[0:1, :] = new_carry
    cnt_ref[...] = jnp.broadcast_to(new_carry, cnt_ref.shape)


def _post(x, gm, km, vm, wq, wo, gf, wr, br, tm=256):
    n = x.shape[0]
    tm = min(tm, n)
    w = MX_HEADS * MX_HD
    m = km.shape[0]
    row = lambda wd: pl.BlockSpec((tm, wd), lambda i: (i, 0))
    full = lambda shape: pl.BlockSpec(shape, lambda i: (0,) * len(shape))
    return pl.pallas_call(
        _post_kernel,
        grid=(n // tm,),
        in_specs=[row(D_MODEL), full((1, D_MODEL)), full((m, w)), full((m, w)), full((D_MODEL, w)), full((w, D_MODEL)),
                  full((1, D_MODEL)), full((D_MODEL, ROUTE_LANES)), full((1, ROUTE_LANES))],
        out_specs=[row(D_MODEL), row(D_MODEL), row(ROUTE_LANES), full((8, ROUTE_LANES))],
        out_shape=[jax.ShapeDtypeStruct((n, D_MODEL), F32), jax.ShapeDtypeStruct((n, D_MODEL), F32),
                   jax.ShapeDtypeStruct((n, ROUTE_LANES), F32), jax.ShapeDtypeStruct((8, ROUTE_LANES), F32)],
        scratch_shapes=[pltpu.VMEM((8, ROUTE_LANES), F32)],
        compiler_params=_cparams(1),
        name="post",
    )(x, gm, km, vm, wq, wo, gf, wr, br)


MOE_TB = 256


def _row_copy(src, src_row, dst, dst_row, sem):
    return pltpu.make_async_copy(src.at[pl.ds(src_row, 1), :], dst.at[pl.ds(dst_row, 1), :], sem)


def _dispatch_kernel(dest_ref, t_ref, xs_in_ref, xs_ref, sem):
    del xs_in_ref
    base = pl.program_id(0) * MOE_TB

    def issue(r, c):
        for k in range(2):
            _row_copy(t_ref, r, xs_ref, dest_ref[2 * (base + r) + k], sem).start()
        return c

    lax.fori_loop(0, MOE_TB, issue, 0)

    def drain(r, c):
        for k in range(2):
            _row_copy(t_ref, r, xs_ref, dest_ref[2 * (base + r) + k], sem).wait()
        return c

    lax.fori_loop(0, MOE_TB, drain, 0)


def _dispatch(dest, t, n_slots):
    n = t.shape[0]
    xs0 = jnp.zeros((n_slots, D_MODEL), F32)
    return pl.pallas_call(
        _dispatch_kernel,
        grid_spec=pltpu.PrefetchScalarGridSpec(
            num_scalar_prefetch=1,
            grid=(n // MOE_TB,),
            in_specs=[pl.BlockSpec((MOE_TB, D_MODEL), lambda i, d: (i, 0)), pl.BlockSpec(memory_space=pl.ANY)],
            out_specs=pl.BlockSpec(memory_space=pl.ANY),
            scratch_shapes=[pltpu.SemaphoreType.DMA(())],
        ),
        out_shape=jax.ShapeDtypeStruct((n_slots, D_MODEL), F32),
        input_output_aliases={2: 0},
        compiler_params=_cparams(1),
        name="dispatch",
    )(dest, t, xs0)


def _expert_kernel(be_ref, na_ref, xs_ref, w1_ref, w3_ref, w2_ref, ys_ref):
    active = pl.program_id(0) < na_ref[0]

    @pl.when(active)
    def _():
        x = xs_ref[...].astype(BF16)
        a = _dot(x, w1_ref[0].astype(BF16))
        b = _dot(x, w3_ref[0].astype(BF16))
        ys_ref[...] = _dot((jax.nn.silu(a) * b).astype(BF16), w2_ref[0].astype(BF16))

    @pl.when(jnp.logical_not(active))
    def _():
        ys_ref[...] = jnp.zeros_like(ys_ref)


def _experts(block_e, n_active, xs, w1, w3, w2):
    n_blocks = xs.shape[0] // MOE_BLOCK
    blk = lambda i, be, na: jnp.minimum(i, na[0] - 1)
    return pl.pallas_call(
        _expert_kernel,
        grid_spec=pltpu.PrefetchScalarGridSpec(
            num_scalar_prefetch=2,
            grid=(n_blocks,),
            in_specs=[
                pl.BlockSpec((MOE_BLOCK, D_MODEL), lambda i, be, na: (blk(i, be, na), 0)),
                pl.BlockSpec((1, D_MODEL, D_EXPERT), lambda i, be, na: (be[blk(i, be, na)], 0, 0)),
                pl.BlockSpec((1, D_MODEL, D_EXPERT), lambda i, be, na: (be[blk(i, be, na)], 0, 0)),
                pl.BlockSpec((1, D_EXPERT, D_MODEL), lambda i, be, na: (be[blk(i, be, na)], 0, 0)),
            ],
            out_specs=pl.BlockSpec((MOE_BLOCK, D_MODEL), lambda i, be, na: (i, 0)),
        ),
        out_shape=jax.ShapeDtypeStruct(xs.shape, F32),
        compiler_params=_cparams(1),
        name="experts",
    )(block_e, n_active, xs, w1, w3, w2)


def _combine_kernel(dest_ref, x_ref, rec_ref, g_ref, ys_ref, o_ref, rows, sem, *, final):
    base = pl.program_id(0) * MOE_TB

    def issue(r, c):
        for k in range(2):
            _row_copy(ys_ref, dest_ref[2 * (base + r) + k], rows.at[k], r, sem).start()
        return c

    lax.fori_loop(0, MOE_TB, issue, 0)

    def drain(r, c):
        for k in range(2):
            _row_copy(ys_ref, dest_ref[2 * (base + r) + k], rows.at[k], r, sem).wait()
        return c

    lax.fori_loop(0, MOE_TB, drain, 0)
    rec = rec_ref[...]
    x = x_ref[...] + rec[:, 2:3] * rows[0] + rec[:, 3:4] * rows[1]
    o_ref[...] = _rms(x, g_ref[...]) if final else x


def _combine(dest, x, rec, ys, g, final):
    n = x.shape[0]
    row = lambda wd: pl.BlockSpec((MOE_TB, wd), lambda i, d: (i, 0))
    return pl.pallas_call(
        functools.partial(_combine_kernel, final=final),
        grid_spec=pltpu.PrefetchScalarGridSpec(
            num_scalar_prefetch=1,
            grid=(n // MOE_TB,),
            in_specs=[row(D_MODEL), row(ROUTE_LANES), pl.BlockSpec((1, D_MODEL), lambda i, d: (0, 0)),
                      pl.BlockSpec(memory_space=pl.ANY)],
            out_specs=row(D_MODEL),
            scratch_shapes=[pltpu.VMEM((2, MOE_TB, D_MODEL), F32), pltpu.SemaphoreType.DMA(())],
        ),
        out_shape=jax.ShapeDtypeStruct((n, D_MODEL), F32),
        compiler_params=_cparams(1),
        name="combine",
    )(dest, x, rec, g, ys)


def _moe(x, t, rec, cnt, w1, w3, w2, g_final, final):
    n = x.shape[0]
    n_asg = 2 * n
    n_blocks = -(-(n_asg + N_EXPERTS * (MOE_BLOCK - 1)) // MOE_BLOCK)
    counts = cnt[0, N_GROUPS:N_GROUPS + N_EXPERTS].astype(jnp.int32)
    padded = (counts + MOE_BLOCK - 1) // MOE_BLOCK * MOE_BLOCK
    pad_end = jnp.cumsum(padded)
    pad_start = pad_end - padded
    expert = rec[:, 0:2].astype(jnp.int32)
    dest = (pad_start[expert] + rec[:, 4:6].astype(jnp.int32)).reshape(-1)
    block_e = jnp.minimum(jnp.searchsorted(pad_end, jnp.arange(n_blocks) * MOE_BLOCK, side="right"),
                          N_EXPERTS - 1).astype(jnp.int32)
    n_active = (pad_end[-1:] // MOE_BLOCK).astype(jnp.int32)
    xs = _dispatch(dest, t, n_blocks * MOE_BLOCK)
    ys = _experts(block_e, n_active, xs, w1, w3, w2)
    return _combine(dest, x, rec, ys, g_final, final)


def _block_diag(w):
    eye = jnp.eye(LRU_BLOCKS, dtype=w.dtype)
    return jnp.einsum("kcd,kj->kcjd", w, eye).reshape(LRU_W, LRU_W)


def _router_weights(w_group, b_group, w_router, b_router):
    pad = ROUTE_LANES - N_GROUPS - N_EXPERTS
    wr = jnp.concatenate([w_group, w_router, jnp.zeros((D_MODEL, pad), F32)], axis=1)
    br = jnp.concatenate([b_group, b_router, jnp.zeros((pad,), F32)])[None, :]
    return wr, br


def kernel(x, mem, norm_mix, w_in, conv_w, conv_b, lru_wa, lru_ba, lru_wx, lru_bx, lru_lambda, ca_rel_bias, diff_lambda, diff_subln, w_branch, w_out, norm_mem, mem_norm, w_mq, w_mk, w_mv, w_mo, norm_ffn, w_group, b_group, w_router, b_router, w1, w3, w2, final_norm):
    b, n, d = x.shape
    assert b == 1 and d == D_MODEL
    xs = x[0]
    for l in range(DEPTH):
        lam_init = 0.8 - 0.6 * math.exp(-0.3 * l)
        z = _inproj(xs, norm_mix[l][None, :], w_in[l].astype(BF16))
        wcat = jnp.concatenate([_block_diag(lru_wa[l]), _block_diag(lru_wx[l])], axis=1).astype(BF16)
        bcat = jnp.concatenate([lru_ba[l], lru_bx[l]])[None, :]
        ya = _lru(z, conv_w[l], conv_b[l][None, :], wcat, bcat, lru_lambda[l][None, :])
        yb = _chunkattn(z, _chunk_bias(ca_rel_bias[l]))
        yc = _retention(z)
        yd = _diffattn(z, diff_lambda[l], diff_subln[l][None, :], lam_init)
        x1 = _merge(xs, z, ya, yb, yc, yd, w_branch[l].astype(BF16), w_out[l].astype(BF16))
        km, vm = _memkv(mem[0], mem_norm[None, :], w_mk[l].astype(BF16), w_mv[l].astype(BF16))
        wr, br = _router_weights(w_group[l], b_group[l], w_router[l], b_router[l])
        x2, t, rec, cnt = _post(x1, norm_mem[l][None, :], km, vm, w_mq[l].astype(BF16), w_mo[l].astype(BF16),
                                norm_ffn[l][None, :], wr, br)
        xs = _moe(x2, t, rec, cnt, w1[l], w3[l], w2[l], final_norm[None, :], final=(l == DEPTH - 1))
    return xs[None]
```

```python
import functools
import math

import numpy as np
import jax
import jax.numpy as jnp
from jax import lax
from jax.experimental import pallas as pl
from jax.experimental.pallas import tpu as pltpu

F32 = jnp.float32
BF16 = jnp.bfloat16

D_MODEL = 1024
DEPTH = 2
CHUNK = 64
NORM_EPS = 1e-6
GN_EPS = 1e-5
LRU_W = 512
LRU_BLOCKS = 8
LRU_BW = LRU_W // LRU_BLOCKS
CONV_W = 4
LRU_C = 8.0
CA_HEADS = 8
CA_HD = 64
CA_PREV = 8
REL_CLIP = 128
RET_HEADS = 8
RET_HD = 64
DIFF_HEADS = 4
DIFF_HD = 64
DIFF_VD = 2 * DIFF_HD
MX_HEADS = 4
MX_HD = 128
N_GROUPS = 4
EXP_PER_GROUP = 8
N_EXPERTS = N_GROUPS * EXP_PER_GROUP
D_EXPERT = 512
MOE_BLOCK = 256
N_BRANCH = 4
BRANCH_W = 512

COL_AX, COL_AG = 0, 512
COL_BQ, COL_BK, COL_BV = 1024, 1536, 2048
COL_CQ, COL_CK, COL_CV, COL_CG = 2560, 3072, 3584, 4096
COL_DQ, COL_DK, COL_DV = 4608, 5120, 5632
COL_GATE = 6144
IN_COLS = 10240

LANES = 128
NEG_BIG = -1e30
LOG2E = 1.0 / math.log(2.0)
VMEM_LIMIT = 56 * 1024 * 1024


def _cparams(n_axes):
    return pltpu.CompilerParams(dimension_semantics=("arbitrary",) * n_axes, vmem_limit_bytes=VMEM_LIMIT)


def _rms(x, g):
    return x * lax.rsqrt(jnp.mean(x * x, axis=-1, keepdims=True) + NORM_EPS) * g


def _dot(a, b):
    return jnp.dot(a, b, preferred_element_type=F32)


def _dot_nt(a, b):
    return lax.dot_general(a, b, (((1,), (1,)), ((), ())), preferred_element_type=F32)


def _dot_tn(a, b):
    return lax.dot_general(a, b, (((0,), (0,)), ((), ())), preferred_element_type=F32)


def _inproj_kernel(x_ref, g_ref, w_ref, o_ref, h_scr):
    @pl.when(pl.program_id(1) == 0)
    def _():
        h_scr[...] = _rms(x_ref[...], g_ref[...]).astype(BF16)

    tn = o_ref.shape[1]
    w = w_ref[:, pl.ds(pl.multiple_of(pl.program_id(1) * tn, tn), tn)]
    o_ref[...] = _dot(h_scr[...], w).astype(o_ref.dtype)


def _inproj(x, g, w, tm=512, tn=1024):
    n = x.shape[0]
    tm = min(tm, n)
    return pl.pallas_call(
        _inproj_kernel,
        grid=(n // tm, IN_COLS // tn),
        in_specs=[
            pl.BlockSpec((tm, D_MODEL), lambda i, j: (i, 0)),
            pl.BlockSpec((1, D_MODEL), lambda i, j: (0, 0)),
            pl.BlockSpec((D_MODEL, IN_COLS), lambda i, j: (0, 0), pipeline_mode=pl.Buffered(1)),
        ],
        out_specs=pl.BlockSpec((tm, tn), lambda i, j: (i, j)),
        out_shape=jax.ShapeDtypeStruct((n, IN_COLS), BF16),
        scratch_shapes=[pltpu.VMEM((tm, D_MODEL), BF16)],
        compiler_params=_cparams(2),
        name="inproj",
    )(x, g, w)


def _gelu_tanh(x):
    return 0.5 * x * (1.0 + jnp.tanh(math.sqrt(2.0 / math.pi) * (x + 0.044715 * x * x * x)))


def _lru_kernel(ax_ref, ag_ref, cw_ref, cb_ref, w_ref, b_ref, lam_ref, o_ref, xbuf, a_scr, u_scr, h_scr, hc):
    t = ax_ref.shape[0]

    @pl.when(pl.program_id(0) == 0)
    def _():
        xbuf[0:8, :] = jnp.zeros((8, LRU_W), F32)
        hc[...] = jnp.zeros_like(hc)

    x = ax_ref[...].astype(F32)
    xbuf[8:8 + t, :] = x
    xc = (cb_ref[...] + cw_ref[3:4, :] * x + cw_ref[2:3, :] * xbuf[7:7 + t, :]
          + cw_ref[1:2, :] * xbuf[6:6 + t, :] + cw_ref[0:1, :] * xbuf[5:5 + t, :])
    xbuf[0:8, :] = xbuf[t:t + 8, :]
    ri = _dot(xc.astype(BF16), w_ref[...]) + b_ref[...]
    r = jax.nn.sigmoid(ri[:, :LRU_W])
    ig = jax.nn.sigmoid(ri[:, LRU_W:])
    nl = -lam_ref[...]
    softplus = jnp.maximum(nl, 0.0) + jnp.log(1.0 + jnp.exp(-jnp.abs(nl)))
    log_a = -LRU_C * r * softplus
    a = jnp.exp(log_a)
    a_scr[...] = a
    u_scr[...] = jnp.sqrt(-jnp.tanh(log_a) * (a * a + 1.0)) * (ig * xc)

    def body(s, h):
        h = a_scr[pl.ds(s, 1), :] * h + u_scr[pl.ds(s, 1), :]
        h_scr[pl.ds(s, 1), :] = h
        return h

    hc[0:1, :] = lax.fori_loop(0, t, body, hc[0:1, :], unroll=8)
    o_ref[...] = (h_scr[...] * _gelu_tanh(ag_ref[...].astype(F32))).astype(o_ref.dtype)


def _lru(z, cw, cb, wcat, bcat, lam, t=256):
    n = z.shape[0]
    t = min(t, n)
    full = lambda shape: pl.BlockSpec(shape, lambda i: (0,) * len(shape))
    return pl.pallas_call(
        _lru_kernel,
        grid=(n // t,),
        in_specs=[
            pl.BlockSpec((t, LRU_W), lambda i: (i, COL_AX // LRU_W)),
            pl.BlockSpec((t, LRU_W), lambda i: (i, COL_AG // LRU_W)),
            full((CONV_W, LRU_W)), full((1, LRU_W)), full((LRU_W, 2 * LRU_W)), full((1, 2 * LRU_W)), full((1, LRU_W)),
        ],
        out_specs=pl.BlockSpec((t, LRU_W), lambda i: (i, 0)),
        out_shape=jax.ShapeDtypeStruct((n, LRU_W), BF16),
        scratch_shapes=[pltpu.VMEM((t + 8, LRU_W), F32), pltpu.VMEM((t, LRU_W), F32), pltpu.VMEM((t, LRU_W), F32),
                        pltpu.VMEM((t, LRU_W), F32), pltpu.VMEM((8, LRU_W), F32)],
        compiler_params=_cparams(1),
        name="lru",
    )(z, z, cw, cb, wcat, bcat, lam)


CA_QB = 256
CA_KW = CA_PREV * CHUNK + CA_QB


def _chunkattn_kernel(q_ref, k0_ref, k1_ref, k2_ref, v0_ref, v1_ref, v2_ref, bias_ref, o_ref):
    i = pl.program_id(0)
    q = q_ref[...]
    k = jnp.concatenate([k0_ref[...], k1_ref[...], k2_ref[...]], axis=0)
    v = jnp.concatenate([v0_ref[...], v1_ref[...], v2_ref[...]], axis=0)
    col = lax.broadcasted_iota(jnp.int32, (CA_QB, CA_KW), 1)
    in_seq = col >= (CA_KW - CA_QB) - CA_QB * i
    outs = []
    for h in range(CA_HEADS):
        sl = slice(h * CA_HD, (h + 1) * CA_HD)
        s = _dot_nt(q[:, sl], k[:, sl]) * (CA_HD ** -0.5) + bias_ref[h]
        s = jnp.where(in_seq, s, NEG_BIG)
        p = jnp.exp(s - jnp.max(s, axis=-1, keepdims=True))
        l = jnp.sum(p, axis=-1, keepdims=True)
        outs.append(_dot(p.astype(BF16), v[:, sl]) / l)
    o_ref[...] = jnp.concatenate(outs, axis=1).astype(o_ref.dtype)


def _chunk_bias(rel_table):
    span = CA_QB + CA_KW - 1
    n_hi = CA_KW - 1 - REL_CLIP
    n_lo = CA_QB - 1 - REL_CLIP
    e = jnp.concatenate([jnp.broadcast_to(rel_table[:, -1:], (CA_HEADS, n_hi)), rel_table[:, ::-1],
                         jnp.broadcast_to(rel_table[:, :1], (CA_HEADS, n_lo + 1))], axis=1)
    w = jnp.tile(e, (1, CA_QB))[:, :CA_QB * span].reshape(CA_HEADS, CA_QB, span)
    toeplitz = w[:, :, CA_QB - 1:CA_QB - 1 + CA_KW]
    cq = np.arange(CA_QB)[:, None] // CHUNK
    ck = np.arange(CA_KW)[None, :] // CHUNK
    band = (ck >= cq) & (ck <= cq + CA_PREV)
    return jnp.where(band[None], toeplitz.astype(F32), NEG_BIG)


def _chunkattn(z, bias):
    n = z.shape[0]
    w = CA_HEADS * CA_HD
    nb = n // CA_QB
    kv = lambda col, back: pl.BlockSpec((CA_QB, w), lambda i: (jnp.maximum(i - back, 0), col // w))
    return pl.pallas_call(
        _chunkattn_kernel,
        grid=(nb,),
        in_specs=[
            pl.BlockSpec((CA_QB, w), lambda i: (i, COL_BQ // w)),
            kv(COL_BK, 2), kv(COL_BK, 1), kv(COL_BK, 0),
            kv(COL_BV, 2), kv(COL_BV, 1), kv(COL_BV, 0),
            pl.BlockSpec((CA_HEADS, CA_QB, CA_KW), lambda i: (0, 0, 0)),
        ],
        out_specs=pl.BlockSpec((CA_QB, w), lambda i: (i, 0)),
        out_shape=jax.ShapeDtypeStruct((n, w), BF16),
        compiler_params=_cparams(1),
        name="chunkattn",
    )(z, z, z, z, z, z, z, bias)


RET_T = 256
_RET_LOG_G = np.log(1.0 - 2.0 ** (-5.0 - np.arange(RET_HEADS)))


def _retention_consts(t):
    pos = np.arange(t)
    diff = pos[:, None] - pos[None, :]
    dmat = np.where(diff[None] >= 0, np.exp(_RET_LOG_G[:, None, None] * np.maximum(diff, 0)[None]), 0.0)
    dmat = dmat * (RET_HD ** -0.5)
    zeta = np.exp(_RET_LOG_G[None, :] * (t - 1 - pos)[:, None]) * (RET_HD ** -0.5)
    xi = np.exp(_RET_LOG_G[None, :] * (pos + 1)[:, None])
    rep = lambda m: np.repeat(m, RET_HD, axis=1)
    return (jnp.asarray(dmat, F32), jnp.asarray(rep(zeta), F32), jnp.asarray(rep(xi), F32))


def _retention_kernel(q_ref, k_ref, v_ref, g_ref, dmat_ref, zeta_ref, xi_ref, o_ref, s_scr):
    t = q_ref.shape[0]

    @pl.when(pl.program_id(0) == 0)
    def _():
        s_scr[...] = jnp.zeros_like(s_scr)

    q = q_ref[...]
    k = k_ref[...]
    v = v_ref[...]
    kz = (k.astype(F32) * zeta_ref[...]).astype(BF16)
    xi = xi_ref[...]
    g = g_ref[...].astype(F32)
    outs = []
    for h in range(RET_HEADS):
        sl = slice(h * RET_HD, (h + 1) * RET_HD)
        qh, vh = q[:, sl], v[:, sl]
        inner = _dot_nt(qh, k[:, sl]) * dmat_ref[h]
        state = s_scr[h]
        o = _dot(inner.astype(BF16), vh) + _dot(qh, state.astype(BF16)) * xi[:, sl]
        s_scr[h] = float(np.exp(_RET_LOG_G[h] * t)) * state + _dot_tn(kz[:, sl], vh)
        mu = jnp.mean(o, axis=-1, keepdims=True)
        d = o - mu
        var = jnp.mean(d * d, axis=-1, keepdims=True)
        outs.append(d * lax.rsqrt(var + GN_EPS))
    o_ref[...] = (jax.nn.silu(g) * jnp.concatenate(outs, axis=1)).astype(o_ref.dtype)


def _retention(z):
    n = z.shape[0]
    t = min(RET_T, n)
    w = RET_HEADS * RET_HD
    dmat, zeta, xi = _retention_consts(t)
    col = lambda c: pl.BlockSpec((t, w), lambda i: (i, c // w))
    return pl.pallas_call(
        _retention_kernel,
        grid=(n // t,),
        in_specs=[col(COL_CQ), col(COL_CK), col(COL_CV), col(COL_CG),
                  pl.BlockSpec((RET_HEADS, t, t), lambda i: (0, 0, 0)),
                  pl.BlockSpec((t, w), lambda i: (0, 0)), pl.BlockSpec((t, w), lambda i: (0, 0))],
        out_specs=pl.BlockSpec((t, w), lambda i: (i, 0)),
        out_shape=jax.ShapeDtypeStruct((n, w), BF16),
        scratch_shapes=[pltpu.VMEM((RET_HEADS, RET_HD, RET_HD), F32)],
        compiler_params=_cparams(1),
        name="retention",
    )(z, z, z, z, dmat, zeta, xi)


DA_B = 256
DA_UNROLL = 4
DA_ONES = 16
_DA_SLOPES = (2.0 ** (-8.0 * np.arange(1, DIFF_HEADS + 1) / DIFF_HEADS)).astype(np.float32)


def _bf16_parts(x, n):
    parts = []
    for _ in range(n):
        p = float(np.asarray(x, np.float32).astype(jnp.bfloat16).astype(np.float32))
        parts.append(p)
        x = x - p
    return parts


_LOG2E_PARTS = _bf16_parts(LOG2E, 3)


def _diffattn_consts():
    kloc = np.arange(DA_B)[:, None]
    qloc = (np.arange(2 * DA_B) % DA_B)[None, :]
    slopes = _DA_SLOPES[:, None, None].astype(np.float64)
    kfeat = np.zeros((DIFF_HEADS, DA_B, DIFF_VD))
    kfeat[:, :, :len(_LOG2E_PARTS)] = slopes * kloc
    diag = np.where((kloc // CHUNK) <= (qloc // CHUNK), LOG2E * slopes * (qloc - np.abs(qloc - kloc) - kloc), NEG_BIG)
    bias = np.stack([np.zeros_like(diag), diag, np.full_like(diag, NEG_BIG)], axis=1)
    return jnp.asarray(kfeat, BF16), jnp.asarray(bias, F32)


def _diffattn_kernel(slope_ref, q_ref, k_ref, vt_ref, kfeat_ref, bias_ref, lam_ref, g_ref, o_ref,
                     qq_scr, m_scr, acc_scr, s_scr, p_scr, a_scr, *, lam_init):
    h = pl.program_id(0)
    i = pl.program_id(1)
    blk = DA_B
    qt = (q_ref[...].astype(F32) * (LOG2E * DIFF_HD ** -0.5)).T
    feat = lax.broadcasted_iota(jnp.int32, (DIFF_VD, blk), 0)
    qq_scr[0:DIFF_VD, 0:blk] = jnp.where(feat < DIFF_HD, qt, 0.0).astype(BF16)
    qq_scr[0:DIFF_VD, blk:] = jnp.where(feat >= DIFF_HD, qt, 0.0).astype(BF16)
    frow = lax.broadcasted_iota(jnp.int32, (DIFF_VD, 2 * blk), 0)
    qfeat = jnp.zeros((DIFF_VD, 2 * blk), F32)
    for idx, part in enumerate(_LOG2E_PARTS):
        qfeat = jnp.where(frow == idx, part, qfeat)
    qq_scr[DIFF_VD:, :] = qfeat.astype(BF16)
    m_scr[...] = jnp.full_like(m_scr, NEG_BIG)
    acc_scr[...] = jnp.zeros_like(acc_scr)
    kfeat = kfeat_ref[0]

    def scores(j):
        k = k_ref[pl.ds(pl.multiple_of(j * blk, blk), blk), :]
        return _dot(jnp.concatenate([k, kfeat], axis=1), qq_scr[...])

    def softmax_step(j, t):
        offset = slope_ref[h] * LOG2E * (j * blk).astype(F32)
        m_old = m_scr[...]
        m_new = jnp.maximum(m_old, jnp.max(t, axis=0, keepdims=True) + offset)
        m_scr[...] = m_new
        return jnp.exp2(t - (m_new - offset)).astype(BF16), jnp.exp2(m_old - m_new)

    def accumulate(j, p, alpha):
        vt = vt_ref[0, :, pl.ds(pl.multiple_of(j * blk, blk), blk)]
        acc_scr[...] = alpha * acc_scr[...] + _dot(vt, p)

    def stage(j, slot):
        s_scr[1 - slot] = scores(jnp.minimum(j + 1, i))
        accumulate(jnp.clip(j - 1, 0, i), p_scr[1 - slot], a_scr[1 - slot])
        sel = jnp.where(j < i, 0, jnp.where(j == i, 1, 2))
        p_scr[slot], a_scr[slot] = softmax_step(j, s_scr[slot] + bias_ref[0, sel])

    s_scr[0] = scores(0)
    p_scr[1] = jnp.zeros((blk, 2 * blk), BF16)
    a_scr[1] = jnp.ones((1, 2 * blk), F32)

    def body(jj, c):
        for u in range(DA_UNROLL):
            stage(DA_UNROLL * jj + u, u % 2)
        return c

    lax.fori_loop(0, (i + DA_UNROLL) // DA_UNROLL, body, 0)
    accumulate(i, p_scr[1], a_scr[1])

    o = acc_scr[0:DIFF_VD, :] / acc_scr[DIFF_VD:DIFF_VD + 1, :]
    lv = lam_ref[...]
    lam = (jnp.exp(jnp.sum(lv[0:1, :] * lv[1:2, :], axis=-1, keepdims=True))
           - jnp.exp(jnp.sum(lv[2:3, :] * lv[3:4, :], axis=-1, keepdims=True)) + lam_init)
    od = o[:, 0:blk] - lam * o[:, blk:]
    od = od * lax.rsqrt(jnp.mean(od * od, axis=0, keepdims=True) + GN_EPS)
    o_ref[...] = (od.T * g_ref[...] * (1.0 - lam_init)).astype(o_ref.dtype)


def _diffattn(z, lam_vecs, subln_g, lam_init):
    n = z.shape[0]
    assert n % DA_B == 0
    w = DIFF_VD
    kfeat, bias = _diffattn_consts()
    vt = z[:, COL_DV:COL_DV + DIFF_HEADS * w].reshape(n, DIFF_HEADS, w).transpose(1, 2, 0)
    vt = jnp.concatenate([vt, jnp.ones((DIFF_HEADS, DA_ONES, n), BF16)], axis=1)
    return pl.pallas_call(
        functools.partial(_diffattn_kernel, lam_init=lam_init),
        grid_spec=pltpu.PrefetchScalarGridSpec(
            num_scalar_prefetch=1,
            grid=(DIFF_HEADS, n // DA_B),
            in_specs=[
                pl.BlockSpec((DA_B, w), lambda h, i, s: (i, COL_DQ // w + h)),
                pl.BlockSpec((n, w), lambda h, i, s: (0, COL_DK // w + h)),
                pl.BlockSpec((1, w + DA_ONES, n), lambda h, i, s: (h, 0, 0)),
                pl.BlockSpec((1, DA_B, w), lambda h, i, s: (h, 0, 0)),
                pl.BlockSpec((1, 3, DA_B, 2 * DA_B), lambda h, i, s: (h, 0, 0, 0)),
                pl.BlockSpec((4, DIFF_HD), lambda h, i, s: (0, 0)),
                pl.BlockSpec((1, w), lambda h, i, s: (0, 0)),
            ],
            out_specs=pl.BlockSpec((DA_B, w), lambda h, i, s: (i, h)),
            scratch_shapes=[pltpu.VMEM((2 * w, 2 * DA_B), BF16), pltpu.VMEM((1, 2 * DA_B), F32),
                            pltpu.VMEM((w + DA_ONES, 2 * DA_B), F32),
                            pltpu.VMEM((2, DA_B, 2 * DA_B), F32), pltpu.VMEM((2, DA_B, 2 * DA_B), BF16),
                            pltpu.VMEM((2, 1, 2 * DA_B), F32)],
        ),
        out_shape=jax.ShapeDtypeStruct((n, DIFF_HEADS * w), BF16),
        compiler_params=_cparams(2),
        name="diffattn",
    )(jnp.asarray(_DA_SLOPES), z, z, vt, kfeat, bias, lam_vecs, subln_g)


def _merge_kernel(x_ref, ya_ref, yb_ref, yc_ref, yd_ref, g0_ref, g1_ref, g2_ref, g3_ref, wb_ref, wo_ref, o_ref):
    mixed = None
    for y_ref, g_ref, b in ((ya_ref, g0_ref, 0), (yb_ref, g1_ref, 1), (yc_ref, g2_ref, 2), (yd_ref, g3_ref, 3)):
        term = jax.nn.sigmoid(g_ref[...].astype(F32)) * _dot(y_ref[...], wb_ref[b])
        mixed = term if mixed is None else mixed + term
    o_ref[...] = x_ref[...] + _dot(mixed.astype(BF16), wo_ref[...])


def _merge(x, z, ya, yb, yc, yd, wb, wo, tm=512):
    n = x.shape[0]
    tm = min(tm, n)
    row = lambda w: pl.BlockSpec((tm, w), lambda i: (i, 0))
    gate = lambda b: pl.BlockSpec((tm, D_MODEL), lambda i: (i, COL_GATE // D_MODEL + b))
    return pl.pallas_call(
        _merge_kernel,
        grid=(n // tm,),
        in_specs=[row(D_MODEL), row(BRANCH_W), row(BRANCH_W), row(BRANCH_W), row(BRANCH_W),
                  gate(0), gate(1), gate(2), gate(3),
                  pl.BlockSpec((N_BRANCH, BRANCH_W, D_MODEL), lambda i: (0, 0, 0)),
                  pl.BlockSpec((D_MODEL, D_MODEL), lambda i: (0, 0))],
        out_specs=row(D_MODEL),
        out_shape=jax.ShapeDtypeStruct((n, D_MODEL), F32),
        compiler_params=_cparams(1),
        name="merge",
    )(x, ya, yb, yc, yd, z, z, z, z, wb, wo)


def _memkv_kernel(mem_ref, g_ref, wk_ref, wv_ref, k_ref, v_ref):
    mn = _rms(mem_ref[...], g_ref[...]).astype(BF16)
    k_ref[...] = _dot(mn, wk_ref[...]).astype(BF16)
    v_ref[...] = _dot(mn, wv_ref[...]).astype(BF16)


def _memkv(mem, g, wk, wv):
    m = mem.shape[0]
    w = MX_HEADS * MX_HD
    out = jax.ShapeDtypeStruct((m, w), BF16)
    return pl.pallas_call(_memkv_kernel, out_shape=(out, out), name="memkv",
                          compiler_params=pltpu.CompilerParams(vmem_limit_bytes=VMEM_LIMIT))(mem, g, wk, wv)


ROUTE_LANES = LANES


def _route(logits, carry):
    t = logits.shape[0]
    lane_i = lax.broadcasted_iota(jnp.int32, (t, ROUTE_LANES), 1)
    lane = lane_i.astype(F32)
    big = float(ROUTE_LANES)
    gl = jnp.where(lane_i < N_GROUPS, logits, NEG_BIG)
    gmax = jnp.max(gl, axis=-1, keepdims=True)
    gsum = jnp.sum(jnp.exp(gl - gmax), axis=-1, keepdims=True)
    g_sel = jnp.min(jnp.where(gl == gmax, lane, big), axis=-1, keepdims=True)
    g_prob = 1.0 / gsum
    lo = N_GROUPS + EXP_PER_GROUP * g_sel
    el = jnp.where((lane >= lo) & (lane < lo + EXP_PER_GROUP), logits, NEG_BIG)
    e1 = jnp.max(el, axis=-1, keepdims=True)
    i1 = jnp.min(jnp.where(el == e1, lane, big), axis=-1, keepdims=True)
    el2 = jnp.where(lane == i1, NEG_BIG, el)
    e2 = jnp.max(el2, axis=-1, keepdims=True)
    i2 = jnp.min(jnp.where(el2 == e2, lane, big), axis=-1, keepdims=True)
    esum = jnp.sum(jnp.exp(el - e1), axis=-1, keepdims=True)
    p1 = 1.0 / esum
    p2 = jnp.exp(e2 - e1) / esum
    w1 = p1 / (p1 + p2) * g_prob
    w2 = p2 / (p1 + p2) * g_prob
    hot1 = lane == i1
    hot2 = lane == i2
    cnt = jnp.where(hot1 | hot2, 1.0, 0.0)
    r = lax.broadcasted_iota(jnp.int32, (t, t), 0)
    c = lax.broadcasted_iota(jnp.int32, (t, t), 1)
    before = jnp.where(c < r, 1.0, 0.0).astype(BF16)
    prefix = _dot(before, cnt.astype(BF16)) + carry
    rank1 = jnp.sum(jnp.where(hot1, prefix, 0.0), axis=-1, keepdims=True)
    rank2 = jnp.sum(jnp.where(hot2, prefix, 0.0), axis=-1, keepdims=True)
    rec = jnp.zeros((t, ROUTE_LANES), F32)
    for idx, val in enumerate((i1 - N_GROUPS, i2 - N_GROUPS, w1, w2, rank1, rank2)):
        rec = jnp.where(lane_i == idx, val, rec)
    return rec, carry + jnp.sum(cnt, axis=0, keepdims=True)


def _post_kernel(x_ref, gm_ref, km_ref, vm_ref, wq_ref, wo_ref, gf_ref, wr_ref, br_ref,
                 xo_ref, t_ref, rec_ref, cnt_ref, carry):
    @pl.when(pl.program_id(0) == 0)
    def _():
        carry[...] = jnp.zeros_like(carry)

    x = x_ref[...]
    q = _dot(_rms(x, gm_ref[...]).astype(BF16), wq_ref[...]).astype(BF16)
    km = km_ref[...]
    vm = vm_ref[...]
    outs = []
    for h in range(MX_HEADS):
        sl = slice(h * MX_HD, (h + 1) * MX_HD)
        s = _dot_nt(q[:, sl], km[:, sl]) * (MX_HD ** -0.5)
        p = jnp.exp(s - jnp.max(s, axis=-1, keepdims=True))
        p = p / jnp.sum(p, axis=-1, keepdims=True)
        outs.append(_dot(p.astype(BF16), vm[:, sl]))
    x = x + _dot(jnp.concatenate(outs, axis=1).astype(BF16), wo_ref[...])
    xo_ref[...] = x
    t = _rms(x, gf_ref[...])
    t_ref[...] = t
    logits = jnp.dot(t, wr_ref[...], preferred_element_type=F32, precision=lax.Precision.HIGHEST) + br_ref[...]
    rec, new_carry = _route(logits, carry[0:1, :])
    rec_ref[...] = rec
    carry[0:1, :] = new_carry
    cnt_ref[...] = jnp.broadcast_to(new_carry, cnt_ref.shape)


def _post(x, gm, km, vm, wq, wo, gf, wr, br, tm=256):
    n = x.shape[0]
    tm = min(tm, n)
    w = MX_HEADS * MX_HD
    m = km.shape[0]
    row = lambda wd: pl.BlockSpec((tm, wd), lambda i: (i, 0))
    full = lambda shape: pl.BlockSpec(shape, lambda i: (0,) * len(shape))
    return pl.pallas_call(
        _post_kernel,
        grid=(n // tm,),
        in_specs=[row(D_MODEL), full((1, D_MODEL)), full((m, w)), full((m, w)), full((D_MODEL, w)), full((w, D_MODEL)),
                  full((1, D_MODEL)), full((D_MODEL, ROUTE_LANES)), full((1, ROUTE_LANES))],
        out_specs=[row(D_MODEL), row(D_MODEL), row(ROUTE_LANES), full((8, ROUTE_LANES))],
        out_shape=[jax.ShapeDtypeStruct((n, D_MODEL), F32), jax.ShapeDtypeStruct((n, D_MODEL), F32),
                   jax.ShapeDtypeStruct((n, ROUTE_LANES), F32), jax.ShapeDtypeStruct((8, ROUTE_LANES), F32)],
        scratch_shapes=[pltpu.VMEM((8, ROUTE_LANES), F32)],
        compiler_params=_cparams(1),
        name="post",
    )(x, gm, km, vm, wq, wo, gf, wr, br)


MOE_TB = 256


def _row_copy(src, src_row, dst, dst_row, sem):
    return pltpu.make_async_copy(src.at[pl.ds(src_row, 1), :], dst.at[pl.ds(dst_row, 1), :], sem)


def _dispatch_kernel(dest_ref, t_ref, xs_in_ref, xs_ref, sem):
    del xs_in_ref
    base = pl.program_id(0) * MOE_TB

    def issue(r, c):
        for k in range(2):
            _row_copy(t_ref, r, xs_ref, dest_ref[2 * (base + r) + k], sem).start()
        return c

    lax.fori_loop(0, MOE_TB, issue, 0)

    def drain(r, c):
        for k in range(2):
            _row_copy(t_ref, r, xs_ref, dest_ref[2 * (base + r) + k], sem).wait()
        return c

    lax.fori_loop(0, MOE_TB, drain, 0)


def _dispatch(dest, t, n_slots):
    n = t.shape[0]
    xs0 = jnp.zeros((n_slots, D_MODEL), F32)
    return pl.pallas_call(
        _dispatch_kernel,
        grid_spec=pltpu.PrefetchScalarGridSpec(
            num_scalar_prefetch=1,
            grid=(n // MOE_TB,),
            in_specs=[pl.BlockSpec((MOE_TB, D_MODEL), lambda i, d: (i, 0)), pl.BlockSpec(memory_space=pl.ANY)],
            out_specs=pl.BlockSpec(memory_space=pl.ANY),
            scratch_shapes=[pltpu.SemaphoreType.DMA(())],
        ),
        out_shape=jax.ShapeDtypeStruct((n_slots, D_MODEL), F32),
        input_output_aliases={2: 0},
        compiler_params=_cparams(1),
        name="dispatch",
    )(dest, t, xs0)


def _expert_kernel(be_ref, na_ref, xs_ref, w1_ref, w3_ref, w2_ref, ys_ref):
    active = pl.program_id(0) < na_ref[0]

    @pl.when(active)
    def _():
        x = xs_ref[...].astype(BF16)
        a = _dot(x, w1_ref[...].astype(BF16))
        b = _dot(x, w3_ref[...].astype(BF16))
        ys_ref[...] = _dot((jax.nn.silu(a) * b).astype(BF16), w2_ref[...].astype(BF16))

    @pl.when(jnp.logical_not(active))
    def _():
        ys_ref[...] = jnp.zeros_like(ys_ref)


def _experts(block_e, n_active, xs, w1, w3, w2, layer):
    n_blocks = xs.shape[0] // MOE_BLOCK
    blk = lambda i, na: jnp.minimum(i, jnp.maximum(na[0] - 1, 0))
    wspec = lambda rows, cols: pl.BlockSpec((None, None, rows, cols), lambda i, be, na: (layer, be[blk(i, na)], 0, 0))
    return pl.pallas_call(
        _expert_kernel,
        grid_spec=pltpu.PrefetchScalarGridSpec(
            num_scalar_prefetch=2,
            grid=(n_blocks,),
            in_specs=[
                pl.BlockSpec((MOE_BLOCK, D_MODEL), lambda i, be, na: (blk(i, na), 0)),
                wspec(D_MODEL, D_EXPERT), wspec(D_MODEL, D_EXPERT), wspec(D_EXPERT, D_MODEL),
            ],
            out_specs=pl.BlockSpec((MOE_BLOCK, D_MODEL), lambda i, be, na: (i, 0)),
        ),
        out_shape=jax.ShapeDtypeStruct(xs.shape, F32),
        compiler_params=_cparams(1),
        name="experts",
    )(block_e, n_active, xs, w1, w3, w2)


def _combine_kernel(dest_ref, x_ref, rec_ref, g_ref, ys_ref, o_ref, rows, sem, *, final):
    base = pl.program_id(0) * MOE_TB

    def issue(r, c):
        for k in range(2):
            _row_copy(ys_ref, dest_ref[2 * (base + r) + k], rows.at[k], r, sem).start()
        return c

    lax.fori_loop(0, MOE_TB, issue, 0)

    def drain(r, c):
        for k in range(2):
            _row_copy(ys_ref, dest_ref[2 * (base + r) + k], rows.at[k], r, sem).wait()
        return c

    lax.fori_loop(0, MOE_TB, drain, 0)
    rec = rec_ref[...]
    x = x_ref[...] + rec[:, 2:3] * rows[0] + rec[:, 3:4] * rows[1]
    o_ref[...] = _rms(x, g_ref[...]) if final else x


def _combine(dest, x, rec, ys, g, final):
    n = x.shape[0]
    row = lambda wd: pl.BlockSpec((MOE_TB, wd), lambda i, d: (i, 0))
    return pl.pallas_call(
        functools.partial(_combine_kernel, final=final),
        grid_spec=pltpu.PrefetchScalarGridSpec(
            num_scalar_prefetch=1,
            grid=(n // MOE_TB,),
            in_specs=[row(D_MODEL), row(ROUTE_LANES), pl.BlockSpec((1, D_MODEL), lambda i, d: (0, 0)),
                      pl.BlockSpec(memory_space=pl.ANY)],
            out_specs=row(D_MODEL),
            scratch_shapes=[pltpu.VMEM((2, MOE_TB, D_MODEL), F32), pltpu.SemaphoreType.DMA(())],
        ),
        out_shape=jax.ShapeDtypeStruct((n, D_MODEL), F32),
        compiler_params=_cparams(1),
        name="combine",
    )(dest, x, rec, g, ys)


def _moe(x, t, rec, cnt, w1, w3, w2, layer, g_final, final):
    n = x.shape[0]
    n_asg = 2 * n
    n_blocks = -(-(n_asg + N_EXPERTS * (MOE_BLOCK - 1)) // MOE_BLOCK)
    counts = cnt[0, N_GROUPS:N_GROUPS + N_EXPERTS].astype(jnp.int32)
    padded = (counts + MOE_BLOCK - 1) // MOE_BLOCK * MOE_BLOCK
    pad_end = jnp.cumsum(padded)
    pad_start = pad_end - padded
    expert = rec[:, 0:2].astype(jnp.int32)
    dest = (pad_start[expert] + rec[:, 4:6].astype(jnp.int32)).reshape(-1)
    starts = jnp.arange(n_blocks, dtype=jnp.int32) * MOE_BLOCK
    block_e = jnp.minimum(jnp.sum(pad_end[None, :] <= starts[:, None], axis=1), N_EXPERTS - 1).astype(jnp.int32)
    n_active = (pad_end[-1:] // MOE_BLOCK).astype(jnp.int32)
    xs = _dispatch(dest, t, n_blocks * MOE_BLOCK)
    ys = _experts(block_e, n_active, xs, w1, w3, w2, layer)
    return _combine(dest, x, rec, ys, g_final, final)


def _block_diag(w):
    eye = jnp.eye(LRU_BLOCKS, dtype=w.dtype)
    return jnp.einsum("kcd,kj->kcjd", w, eye).reshape(LRU_W, LRU_W)


def _router_weights(w_group, b_group, w_router, b_router):
    pad = ROUTE_LANES - N_GROUPS - N_EXPERTS
    wr = jnp.concatenate([w_group, w_router, jnp.zeros((D_MODEL, pad), F32)], axis=1)
    br = jnp.concatenate([b_group, b_router, jnp.zeros((pad,), F32)])[None, :]
    return wr, br


def kernel(x, mem, norm_mix, w_in, conv_w, conv_b, lru_wa, lru_ba, lru_wx, lru_bx, lru_lambda, ca_rel_bias, diff_lambda, diff_subln, w_branch, w_out, norm_mem, mem_norm, w_mq, w_mk, w_mv, w_mo, norm_ffn, w_group, b_group, w_router, b_router, w1, w3, w2, final_norm):
    b, n, d = x.shape
    assert b == 1 and d == D_MODEL
    xs = x[0]
    for l in range(DEPTH):
        lam_init = 0.8 - 0.6 * math.exp(-0.3 * l)
        z = _inproj(xs, norm_mix[l][None, :], w_in[l].astype(BF16))
        wcat = jnp.concatenate([_block_diag(lru_wa[l]), _block_diag(lru_wx[l])], axis=1).astype(BF16)
        bcat = jnp.concatenate([lru_ba[l], lru_bx[l]])[None, :]
        ya = _lru(z, conv_w[l], conv_b[l][None, :], wcat, bcat, lru_lambda[l][None, :])
        yb = _chunkattn(z, _chunk_bias(ca_rel_bias[l]))
        yc = _retention(z)
        yd = _diffattn(z, diff_lambda[l], diff_subln[l][None, :], lam_init)
        x1 = _merge(xs, z, ya, yb, yc, yd, w_branch[l].astype(BF16), w_out[l].astype(BF16))
        km, vm = _memkv(mem[0], mem_norm[None, :], w_mk[l].astype(BF16), w_mv[l].astype(BF16))
        wr, br = _router_weights(w_group[l], b_group[l], w_router[l], b_router[l])
        x2, t, rec, cnt = _post(x1, norm_mem[l][None, :], km, vm, w_mq[l].astype(BF16), w_mo[l].astype(BF16),
                                norm_ffn[l][None, :], wr, br)
        xs = _moe(x2, t, rec, cnt, w1, w3, w2, l, final_norm[None, :], final=(l == DEPTH - 1))
    return xs[None]
```

```python
import functools
import math

import numpy as np
import jax
import jax.numpy as jnp
from jax import lax
from jax.experimental import pallas as pl
from jax.experimental.pallas import tpu as pltpu

F32 = jnp.float32
BF16 = jnp.bfloat16

D_MODEL = 1024
DEPTH = 2
CHUNK = 64
NORM_EPS = 1e-6
GN_EPS = 1e-5
LRU_W = 512
LRU_BLOCKS = 8
LRU_BW = LRU_W // LRU_BLOCKS
CONV_W = 4
LRU_C = 8.0
CA_HEADS = 8
CA_HD = 64
CA_PREV = 8
REL_CLIP = 128
RET_HEADS = 8
RET_HD = 64
DIFF_HEADS = 4
DIFF_HD = 64
DIFF_VD = 2 * DIFF_HD
MX_HEADS = 4
MX_HD = 128
N_GROUPS = 4
EXP_PER_GROUP = 8
N_EXPERTS = N_GROUPS * EXP_PER_GROUP
D_EXPERT = 512
MOE_BLOCK = 256
N_BRANCH = 4
BRANCH_W = 512

COL_AX, COL_AG = 0, 512
COL_BQ, COL_BK, COL_BV = 1024, 1536, 2048
COL_CQ, COL_CK, COL_CV, COL_CG = 2560, 3072, 3584, 4096
COL_DQ, COL_DK, COL_DV = 4608, 5120, 5632
COL_GATE = 6144
IN_COLS = 10240

LANES = 128
NEG_BIG = -1e30
LOG2E = 1.0 / math.log(2.0)
VMEM_LIMIT = 56 * 1024 * 1024


def _cparams(n_axes):
    return pltpu.CompilerParams(dimension_semantics=("arbitrary",) * n_axes, vmem_limit_bytes=VMEM_LIMIT)


def _rms(x, g):
    return x * lax.rsqrt(jnp.mean(x * x, axis=-1, keepdims=True) + NORM_EPS) * g


def _dot(a, b):
    return jnp.dot(a, b, preferred_element_type=F32)


def _dot_nt(a, b):
    return lax.dot_general(a, b, (((1,), (1,)), ((), ())), preferred_element_type=F32)


def _dot_tn(a, b):
    return lax.dot_general(a, b, (((0,), (0,)), ((), ())), preferred_element_type=F32)


def _inproj_kernel(x_ref, g_ref, w_ref, o_ref, h_scr):
    @pl.when(pl.program_id(1) == 0)
    def _():
        h_scr[...] = _rms(x_ref[...], g_ref[...]).astype(BF16)

    tn = o_ref.shape[1]
    w = w_ref[:, pl.ds(pl.multiple_of(pl.program_id(1) * tn, tn), tn)]
    o_ref[...] = _dot(h_scr[...], w).astype(o_ref.dtype)


def _inproj(x, g, w, tm=512, tn=1024):
    n = x.shape[0]
    tm = min(tm, n)
    return pl.pallas_call(
        _inproj_kernel,
        grid=(n // tm, IN_COLS // tn),
        in_specs=[
            pl.BlockSpec((tm, D_MODEL), lambda i, j: (i, 0)),
            pl.BlockSpec((1, D_MODEL), lambda i, j: (0, 0)),
            pl.BlockSpec((D_MODEL, IN_COLS), lambda i, j: (0, 0), pipeline_mode=pl.Buffered(1)),
        ],
        out_specs=pl.BlockSpec((tm, tn), lambda i, j: (i, j)),
        out_shape=jax.ShapeDtypeStruct((n, IN_COLS), BF16),
        scratch_shapes=[pltpu.VMEM((tm, D_MODEL), BF16)],
        compiler_params=_cparams(2),
        name="inproj",
    )(x, g, w)


def _gelu_tanh(x):
    return 0.5 * x * (1.0 + jnp.tanh(math.sqrt(2.0 / math.pi) * (x + 0.044715 * x * x * x)))


def _lru_kernel(ax_ref, ag_ref, cw_ref, cb_ref, w_ref, b_ref, lam_ref, o_ref, xbuf, a_scr, u_scr, h_scr, hc):
    t = ax_ref.shape[0]

    @pl.when(pl.program_id(0) == 0)
    def _():
        xbuf[0:8, :] = jnp.zeros((8, LRU_W), F32)
        hc[...] = jnp.zeros_like(hc)

    x = ax_ref[...].astype(F32)
    xbuf[8:8 + t, :] = x
    xc = (cb_ref[...] + cw_ref[3:4, :] * x + cw_ref[2:3, :] * xbuf[7:7 + t, :]
          + cw_ref[1:2, :] * xbuf[6:6 + t, :] + cw_ref[0:1, :] * xbuf[5:5 + t, :])
    xbuf[0:8, :] = xbuf[t:t + 8, :]
    ri = _dot(xc.astype(BF16), w_ref[...]) + b_ref[...]
    r = jax.nn.sigmoid(ri[:, :LRU_W])
    ig = jax.nn.sigmoid(ri[:, LRU_W:])
    nl = -lam_ref[...]
    softplus = jnp.maximum(nl, 0.0) + jnp.log(1.0 + jnp.exp(-jnp.abs(nl)))
    log_a = -LRU_C * r * softplus
    a = jnp.exp(log_a)
    a_scr[...] = a
    u_scr[...] = jnp.sqrt(-jnp.tanh(log_a) * (a * a + 1.0)) * (ig * xc)

    def body(s, h):
        h = a_scr[pl.ds(s, 1), :] * h + u_scr[pl.ds(s, 1), :]
        h_scr[pl.ds(s, 1), :] = h
        return h

    hc[0:1, :] = lax.fori_loop(0, t, body, hc[0:1, :], unroll=8)
    o_ref[...] = (h_scr[...] * _gelu_tanh(ag_ref[...].astype(F32))).astype(o_ref.dtype)


def _lru(z, cw, cb, wcat, bcat, lam, t=256):
    n = z.shape[0]
    t = min(t, n)
    full = lambda shape: pl.BlockSpec(shape, lambda i: (0,) * len(shape))
    return pl.pallas_call(
        _lru_kernel,
        grid=(n // t,),
        in_specs=[
            pl.BlockSpec((t, LRU_W), lambda i: (i, COL_AX // LRU_W)),
            pl.BlockSpec((t, LRU_W), lambda i: (i, COL_AG // LRU_W)),
            full((CONV_W, LRU_W)), full((1, LRU_W)), full((LRU_W, 2 * LRU_W)), full((1, 2 * LRU_W)), full((1, LRU_W)),
        ],
        out_specs=pl.BlockSpec((t, LRU_W), lambda i: (i, 0)),
        out_shape=jax.ShapeDtypeStruct((n, LRU_W), BF16),
        scratch_shapes=[pltpu.VMEM((t + 8, LRU_W), F32), pltpu.VMEM((t, LRU_W), F32), pltpu.VMEM((t, LRU_W), F32),
                        pltpu.VMEM((t, LRU_W), F32), pltpu.VMEM((8, LRU_W), F32)],
        compiler_params=_cparams(1),
        name="lru",
    )(z, z, cw, cb, wcat, bcat, lam)


CA_QB = 256
CA_KW = CA_PREV * CHUNK + CA_QB


def _chunkattn_kernel(q_ref, k0_ref, k1_ref, k2_ref, v0_ref, v1_ref, v2_ref, bias_ref, o_ref):
    i = pl.program_id(0)
    q = q_ref[...]
    k = jnp.concatenate([k0_ref[...], k1_ref[...], k2_ref[...]], axis=0)
    v = jnp.concatenate([v0_ref[...], v1_ref[...], v2_ref[...]], axis=0)
    col = lax.broadcasted_iota(jnp.int32, (CA_QB, CA_KW), 1)
    in_seq = col >= (CA_KW - CA_QB) - CA_QB * i
    outs = []
    for h in range(CA_HEADS):
        sl = slice(h * CA_HD, (h + 1) * CA_HD)
        s = _dot_nt(q[:, sl], k[:, sl]) * (CA_HD ** -0.5) + bias_ref[h]
        s = jnp.where(in_seq, s, NEG_BIG)
        p = jnp.exp(s - jnp.max(s, axis=-1, keepdims=True))
        l = jnp.sum(p, axis=-1, keepdims=True)
        outs.append(_dot(p.astype(BF16), v[:, sl]) / l)
    o_ref[...] = jnp.concatenate(outs, axis=1).astype(o_ref.dtype)


def _chunk_bias(rel_table):
    span = CA_QB + CA_KW - 1
    n_hi = CA_KW - 1 - REL_CLIP
    n_lo = CA_QB - 1 - REL_CLIP
    e = jnp.concatenate([jnp.broadcast_to(rel_table[:, -1:], (CA_HEADS, n_hi)), rel_table[:, ::-1],
                         jnp.broadcast_to(rel_table[:, :1], (CA_HEADS, n_lo + 1))], axis=1)
    w = jnp.tile(e, (1, CA_QB))[:, :CA_QB * span].reshape(CA_HEADS, CA_QB, span)
    toeplitz = w[:, :, CA_QB - 1:CA_QB - 1 + CA_KW]
    cq = np.arange(CA_QB)[:, None] // CHUNK
    ck = np.arange(CA_KW)[None, :] // CHUNK
    band = (ck >= cq) & (ck <= cq + CA_PREV)
    return jnp.where(band[None], toeplitz.astype(F32), NEG_BIG)


def _chunkattn(z, bias):
    n = z.shape[0]
    w = CA_HEADS * CA_HD
    nb = n // CA_QB
    kv = lambda col, back: pl.BlockSpec((CA_QB, w), lambda i: (jnp.maximum(i - back, 0), col // w))
    return pl.pallas_call(
        _chunkattn_kernel,
        grid=(nb,),
        in_specs=[
            pl.BlockSpec((CA_QB, w), lambda i: (i, COL_BQ // w)),
            kv(COL_BK, 2), kv(COL_BK, 1), kv(COL_BK, 0),
            kv(COL_BV, 2), kv(COL_BV, 1), kv(COL_BV, 0),
            pl.BlockSpec((CA_HEADS, CA_QB, CA_KW), lambda i: (0, 0, 0)),
        ],
        out_specs=pl.BlockSpec((CA_QB, w), lambda i: (i, 0)),
        out_shape=jax.ShapeDtypeStruct((n, w), BF16),
        compiler_params=_cparams(1),
        name="chunkattn",
    )(z, z, z, z, z, z, z, bias)


RET_T = 256
_RET_LOG_G = np.log(1.0 - 2.0 ** (-5.0 - np.arange(RET_HEADS)))


def _retention_consts(t):
    pos = np.arange(t)
    diff = pos[:, None] - pos[None, :]
    dmat = np.where(diff[None] >= 0, np.exp(_RET_LOG_G[:, None, None] * np.maximum(diff, 0)[None]), 0.0)
    dmat = dmat * (RET_HD ** -0.5)
    zeta = np.exp(_RET_LOG_G[None, :] * (t - 1 - pos)[:, None]) * (RET_HD ** -0.5)
    xi = np.exp(_RET_LOG_G[None, :] * (pos + 1)[:, None])
    rep = lambda m: np.repeat(m, RET_HD, axis=1)
    return (jnp.asarray(dmat, F32), jnp.asarray(rep(zeta), F32), jnp.asarray(rep(xi), F32))


def _retention_kernel(q_ref, k_ref, v_ref, g_ref, dmat_ref, zeta_ref, xi_ref, o_ref, s_scr):
    t = q_ref.shape[0]

    @pl.when(pl.program_id(0) == 0)
    def _():
        s_scr[...] = jnp.zeros_like(s_scr)

    q = q_ref[...]
    k = k_ref[...]
    v = v_ref[...]
    kz = (k.astype(F32) * zeta_ref[...]).astype(BF16)
    xi = xi_ref[...]
    g = g_ref[...].astype(F32)
    outs = []
    for h in range(RET_HEADS):
        sl = slice(h * RET_HD, (h + 1) * RET_HD)
        qh, vh = q[:, sl], v[:, sl]
        inner = _dot_nt(qh, k[:, sl]) * dmat_ref[h]
        state = s_scr[h]
        o = _dot(inner.astype(BF16), vh) + _dot(qh, state.astype(BF16)) * xi[:, sl]
        s_scr[h] = float(np.exp(_RET_LOG_G[h] * t)) * state + _dot_tn(kz[:, sl], vh)
        mu = jnp.mean(o, axis=-1, keepdims=True)
        d = o - mu
        var = jnp.mean(d * d, axis=-1, keepdims=True)
        outs.append(d * lax.rsqrt(var + GN_EPS))
    o_ref[...] = (jax.nn.silu(g) * jnp.concatenate(outs, axis=1)).astype(o_ref.dtype)


def _retention(z):
    n = z.shape[0]
    t = min(RET_T, n)
    w = RET_HEADS * RET_HD
    dmat, zeta, xi = _retention_consts(t)
    col = lambda c: pl.BlockSpec((t, w), lambda i: (i, c // w))
    return pl.pallas_call(
        _retention_kernel,
        grid=(n // t,),
        in_specs=[col(COL_CQ), col(COL_CK), col(COL_CV), col(COL_CG),
                  pl.BlockSpec((RET_HEADS, t, t), lambda i: (0, 0, 0)),
                  pl.BlockSpec((t, w), lambda i: (0, 0)), pl.BlockSpec((t, w), lambda i: (0, 0))],
        out_specs=pl.BlockSpec((t, w), lambda i: (i, 0)),
        out_shape=jax.ShapeDtypeStruct((n, w), BF16),
        scratch_shapes=[pltpu.VMEM((RET_HEADS, RET_HD, RET_HD), F32)],
        compiler_params=_cparams(1),
        name="retention",
    )(z, z, z, z, dmat, zeta, xi)


DA_B = 256
DA_UNROLL = 8
DA_TAIL = 2
DA_ONES = 16
_DA_SLOPES = (2.0 ** (-8.0 * np.arange(1, DIFF_HEADS + 1) / DIFF_HEADS)).astype(np.float32)


def _bf16_parts(x, n):
    parts = []
    for _ in range(n):
        p = float(np.asarray(x, np.float32).astype(jnp.bfloat16).astype(np.float32))
        parts.append(p)
        x = x - p
    return parts


_LOG2E_PARTS = _bf16_parts(LOG2E, 3)


def _diffattn_consts():
    kloc = np.arange(DA_B)[:, None]
    qloc = (np.arange(2 * DA_B) % DA_B)[None, :]
    slopes = _DA_SLOPES[:, None, None].astype(np.float64)
    kfeat = np.zeros((DIFF_HEADS, DA_B, DIFF_VD))
    kfeat[:, :, :len(_LOG2E_PARTS)] = slopes * kloc
    diag = np.where((kloc // CHUNK) <= (qloc // CHUNK), LOG2E * slopes * (qloc - np.abs(qloc - kloc) - kloc), NEG_BIG)
    bias = np.stack([np.zeros_like(diag), diag, np.full_like(diag, NEG_BIG)], axis=1)
    return jnp.asarray(kfeat, BF16), jnp.asarray(bias, F32)


def _diffattn_kernel(slope_ref, q_ref, k_ref, vt_ref, kfeat_ref, bias_ref, lam_ref, g_ref, o_ref,
                     qq_scr, m_scr, acc_scr, s_scr, p_scr, a_scr, *, lam_init):
    h = pl.program_id(0)
    i = pl.program_id(1)
    blk = DA_B
    qt = (q_ref[...].astype(F32) * (LOG2E * DIFF_HD ** -0.5)).T
    feat = lax.broadcasted_iota(jnp.int32, (DIFF_VD, blk), 0)
    qq_scr[0:DIFF_VD, 0:blk] = jnp.where(feat < DIFF_HD, qt, 0.0).astype(BF16)
    qq_scr[0:DIFF_VD, blk:] = jnp.where(feat >= DIFF_HD, qt, 0.0).astype(BF16)

    @pl.when(i == 0)
    def _():
        frow = lax.broadcasted_iota(jnp.int32, (DIFF_VD, 2 * blk), 0)
        qfeat = jnp.zeros((DIFF_VD, 2 * blk), F32)
        for idx, part in enumerate(_LOG2E_PARTS):
            qfeat = jnp.where(frow == idx, part, qfeat)
        qq_scr[DIFF_VD:, :] = qfeat.astype(BF16)

    m_scr[...] = jnp.full_like(m_scr, NEG_BIG)
    acc_scr[...] = jnp.zeros_like(acc_scr)
    kfeat = kfeat_ref[0]

    def scores(j):
        k = k_ref[pl.ds(pl.multiple_of(j * blk, blk), blk), :]
        return _dot(jnp.concatenate([k, kfeat], axis=1), qq_scr[...])

    def softmax_step(j, t):
        offset = slope_ref[h] * LOG2E * (j * blk).astype(F32)
        m_old = m_scr[...]
        m_new = jnp.maximum(m_old, jnp.max(t, axis=0, keepdims=True) + offset)
        m_scr[...] = m_new
        return jnp.exp2(t - (m_new - offset)).astype(BF16), jnp.exp2(m_old - m_new)

    def accumulate(j, p, alpha):
        vt = vt_ref[0, :, pl.ds(pl.multiple_of(j * blk, blk), blk)]
        acc_scr[...] = alpha * acc_scr[...] + _dot(vt, p)

    def stage(j, slot, tail):
        s_scr[1 - slot] = scores(jnp.minimum(j + 1, i))
        accumulate(jnp.clip(j - 1, 0, i), p_scr[1 - slot], a_scr[1 - slot])
        t = s_scr[slot]
        if tail:
            t = t + bias_ref[0, jnp.where(j < i, 0, jnp.where(j == i, 1, 2))]
        p_scr[slot], a_scr[slot] = softmax_step(j, t)

    s_scr[0] = scores(0)
    p_scr[1] = jnp.zeros((blk, 2 * blk), BF16)
    a_scr[1] = jnp.ones((1, 2 * blk), F32)
    n_main = i // DA_UNROLL

    def main_body(jj, c):
        for u in range(DA_UNROLL):
            stage(DA_UNROLL * jj + u, u % 2, False)
        return c

    def tail_body(jj, c):
        for u in range(DA_TAIL):
            stage(DA_UNROLL * n_main + DA_TAIL * jj + u, u % 2, True)
        return c

    lax.fori_loop(0, n_main, main_body, 0)
    lax.fori_loop(0, (i - DA_UNROLL * n_main + DA_TAIL) // DA_TAIL, tail_body, 0)
    accumulate(i, p_scr[1], a_scr[1])

    o = acc_scr[0:DIFF_VD, :] / acc_scr[DIFF_VD:DIFF_VD + 1, :]
    lv = lam_ref[...]
    lam = (jnp.exp(jnp.sum(lv[0:1, :] * lv[1:2, :], axis=-1, keepdims=True))
           - jnp.exp(jnp.sum(lv[2:3, :] * lv[3:4, :], axis=-1, keepdims=True)) + lam_init)
    od = o[:, 0:blk] - lam * o[:, blk:]
    od = od * lax.rsqrt(jnp.mean(od * od, axis=0, keepdims=True) + GN_EPS)
    o_ref[...] = (od.T * g_ref[...] * (1.0 - lam_init)).astype(o_ref.dtype)


def _diffattn(z, lam_vecs, subln_g, lam_init):
    n = z.shape[0]
    assert n % DA_B == 0
    w = DIFF_VD
    kfeat, bias = _diffattn_consts()
    vt = z[:, COL_DV:COL_DV + DIFF_HEADS * w].reshape(n, DIFF_HEADS, w).transpose(1, 2, 0)
    vt = jnp.concatenate([vt, jnp.ones((DIFF_HEADS, DA_ONES, n), BF16)], axis=1)
    return pl.pallas_call(
        functools.partial(_diffattn_kernel, lam_init=lam_init),
        grid_spec=pltpu.PrefetchScalarGridSpec(
            num_scalar_prefetch=1,
            grid=(DIFF_HEADS, n // DA_B),
            in_specs=[
                pl.BlockSpec((DA_B, w), lambda h, i, s: (i, COL_DQ // w + h)),
                pl.BlockSpec((n, w), lambda h, i, s: (0, COL_DK // w + h)),
                pl.BlockSpec((1, w + DA_ONES, n), lambda h, i, s: (h, 0, 0)),
                pl.BlockSpec((1, DA_B, w), lambda h, i, s: (h, 0, 0)),
                pl.BlockSpec((1, 3, DA_B, 2 * DA_B), lambda h, i, s: (h, 0, 0, 0)),
                pl.BlockSpec((4, DIFF_HD), lambda h, i, s: (0, 0)),
                pl.BlockSpec((1, w), lambda h, i, s: (0, 0)),
            ],
            out_specs=pl.BlockSpec((DA_B, w), lambda h, i, s: (i, h)),
            scratch_shapes=[pltpu.VMEM((2 * w, 2 * DA_B), BF16), pltpu.VMEM((1, 2 * DA_B), F32),
                            pltpu.VMEM((w + DA_ONES, 2 * DA_B), F32),
                            pltpu.VMEM((2, DA_B, 2 * DA_B), F32), pltpu.VMEM((2, DA_B, 2 * DA_B), BF16),
                            pltpu.VMEM((2, 1, 2 * DA_B), F32)],
        ),
        out_shape=jax.ShapeDtypeStruct((n, DIFF_HEADS * w), BF16),
        compiler_params=_cparams(2),
        name="diffattn",
    )(jnp.asarray(_DA_SLOPES), z, z, vt, kfeat, bias, lam_vecs, subln_g)


def _merge_kernel(x_ref, ya_ref, yb_ref, yc_ref, yd_ref, g0_ref, g1_ref, g2_ref, g3_ref, wb_ref, wo_ref, o_ref):
    mixed = None
    for y_ref, g_ref, b in ((ya_ref, g0_ref, 0), (yb_ref, g1_ref, 1), (yc_ref, g2_ref, 2), (yd_ref, g3_ref, 3)):
        term = jax.nn.sigmoid(g_ref[...].astype(F32)) * _dot(y_ref[...], wb_ref[b])
        mixed = term if mixed is None else mixed + term
    o_ref[...] = x_ref[...] + _dot(mixed.astype(BF16), wo_ref[...])


def _merge(x, z, ya, yb, yc, yd, wb, wo, tm=512):
    n = x.shape[0]
    tm = min(tm, n)
    row = lambda w: pl.BlockSpec((tm, w), lambda i: (i, 0))
    gate = lambda b: pl.BlockSpec((tm, D_MODEL), lambda i: (i, COL_GATE // D_MODEL + b))
    return pl.pallas_call(
        _merge_kernel,
        grid=(n // tm,),
        in_specs=[row(D_MODEL), row(BRANCH_W), row(BRANCH_W), row(BRANCH_W), row(BRANCH_W),
                  gate(0), gate(1), gate(2), gate(3),
                  pl.BlockSpec((N_BRANCH, BRANCH_W, D_MODEL), lambda i: (0, 0, 0)),
                  pl.BlockSpec((D_MODEL, D_MODEL), lambda i: (0, 0))],
        out_specs=row(D_MODEL),
        out_shape=jax.ShapeDtypeStruct((n, D_MODEL), F32),
        compiler_params=_cparams(1),
        name="merge",
    )(x, ya, yb, yc, yd, z, z, z, z, wb, wo)


def _memkv_kernel(mem_ref, g_ref, wk_ref, wv_ref, k_ref, v_ref):
    mn = _rms(mem_ref[...], g_ref[...]).astype(BF16)
    k_ref[...] = _dot(mn, wk_ref[...]).astype(BF16)
    v_ref[...] = _dot(mn, wv_ref[...]).astype(BF16)


def _memkv(mem, g, wk, wv):
    m = mem.shape[0]
    w = MX_HEADS * MX_HD
    out = jax.ShapeDtypeStruct((m, w), BF16)
    return pl.pallas_call(_memkv_kernel, out_shape=(out, out), name="memkv",
                          compiler_params=pltpu.CompilerParams(vmem_limit_bytes=VMEM_LIMIT))(mem, g, wk, wv)


ROUTE_LANES = LANES


def _route(logits, carry):
    t = logits.shape[0]
    lane_i = lax.broadcasted_iota(jnp.int32, (t, ROUTE_LANES), 1)
    lane = lane_i.astype(F32)
    big = float(ROUTE_LANES)
    gl = jnp.where(lane_i < N_GROUPS, logits, NEG_BIG)
    gmax = jnp.max(gl, axis=-1, keepdims=True)
    gsum = jnp.sum(jnp.exp(gl - gmax), axis=-1, keepdims=True)
    g_sel = jnp.min(jnp.where(gl == gmax, lane, big), axis=-1, keepdims=True)
    g_prob = 1.0 / gsum
    lo = N_GROUPS + EXP_PER_GROUP * g_sel
    el = jnp.where((lane >= lo) & (lane < lo + EXP_PER_GROUP), logits, NEG_BIG)
    e1 = jnp.max(el, axis=-1, keepdims=True)
    i1 = jnp.min(jnp.where(el == e1, lane, big), axis=-1, keepdims=True)
    el2 = jnp.where(lane == i1, NEG_BIG, el)
    e2 = jnp.max(el2, axis=-1, keepdims=True)
    i2 = jnp.min(jnp.where(el2 == e2, lane, big), axis=-1, keepdims=True)
    esum = jnp.sum(jnp.exp(el - e1), axis=-1, keepdims=True)
    p1 = 1.0 / esum
    p2 = jnp.exp(e2 - e1) / esum
    w1 = p1 / (p1 + p2) * g_prob
    w2 = p2 / (p1 + p2) * g_prob
    hot1 = lane == i1
    hot2 = lane == i2
    cnt = jnp.where(hot1 | hot2, 1.0, 0.0)
    r = lax.broadcasted_iota(jnp.int32, (t, t), 0)
    c = lax.broadcasted_iota(jnp.int32, (t, t), 1)
    before = jnp.where(c < r, 1.0, 0.0).astype(BF16)
    prefix = _dot(before, cnt.astype(BF16)) + carry
    rank1 = jnp.sum(jnp.where(hot1, prefix, 0.0), axis=-1, keepdims=True)
    rank2 = jnp.sum(jnp.where(hot2, prefix, 0.0), axis=-1, keepdims=True)
    rec = jnp.zeros((t, ROUTE_LANES), F32)
    for idx, val in enumerate((i1 - N_GROUPS, i2 - N_GROUPS, w1, w2, rank1, rank2)):
        rec = jnp.where(lane_i == idx, val, rec)
    return rec, carry + jnp.sum(cnt, axis=0, keepdims=True)


POST_SUB = 256


def _post_kernel(x_ref, gm_ref, km_ref, vm_ref, wq_ref, wo_ref, gf_ref, wrh_ref, wrl_ref, br_ref,
                 xo_ref, t_ref, rec_ref, cnt_ref, carry):
    @pl.when(pl.program_id(0) == 0)
    def _():
        carry[...] = jnp.zeros_like(carry)

    km = km_ref[...]
    vm = vm_ref[...]
    logits = []
    for sub in range(x_ref.shape[0] // POST_SUB):
        rows = slice(sub * POST_SUB, (sub + 1) * POST_SUB)
        x = x_ref[rows, :]
        q = _dot(_rms(x, gm_ref[...]).astype(BF16), wq_ref[...]).astype(BF16)
        outs = []
        for h in range(MX_HEADS):
            sl = slice(h * MX_HD, (h + 1) * MX_HD)
            s = _dot_nt(q[:, sl], km[:, sl]) * (MX_HD ** -0.5)
            p = jnp.exp(s - jnp.max(s, axis=-1, keepdims=True))
            p = p / jnp.sum(p, axis=-1, keepdims=True)
            outs.append(_dot(p.astype(BF16), vm[:, sl]))
        x = x + _dot(jnp.concatenate(outs, axis=1).astype(BF16), wo_ref[...])
        xo_ref[rows, :] = x
        t = _rms(x, gf_ref[...])
        t_ref[rows, :] = t
        t_hi = t.astype(BF16)
        t_lo = (t - t_hi.astype(F32)).astype(BF16)
        logits.append(_dot(t_hi, wrh_ref[...]) + (_dot(t_lo, wrh_ref[...]) + _dot(t_hi, wrl_ref[...])) + br_ref[...])
    running = carry[0:1, :]
    for sub, lg in enumerate(logits):
        rec, running = _route(lg, running)
        rec_ref[sub * POST_SUB:(sub + 1) * POST_SUB, :] = rec
    carry[0:1, :] = running
    cnt_ref[...] = jnp.broadcast_to(running, cnt_ref.shape)


def _post(x, gm, km, vm, wq, wo, gf, wr, br, tm=512):
    n = x.shape[0]
    tm = min(tm, n)
    assert tm % POST_SUB == 0
    w = MX_HEADS * MX_HD
    m = km.shape[0]
    wr_hi = wr.astype(BF16)
    wr_lo = (wr - wr_hi.astype(F32)).astype(BF16)
    row = lambda wd: pl.BlockSpec((tm, wd), lambda i: (i, 0))
    full = lambda shape: pl.BlockSpec(shape, lambda i: (0,) * len(shape))
    return pl.pallas_call(
        _post_kernel,
        grid=(n // tm,),
        in_specs=[row(D_MODEL), full((1, D_MODEL)), full((m, w)), full((m, w)), full((D_MODEL, w)), full((w, D_MODEL)),
                  full((1, D_MODEL)), full((D_MODEL, ROUTE_LANES)), full((D_MODEL, ROUTE_LANES)),
                  full((1, ROUTE_LANES))],
        out_specs=[row(D_MODEL), row(D_MODEL), row(ROUTE_LANES), full((8, ROUTE_LANES))],
        out_shape=[jax.ShapeDtypeStruct((n, D_MODEL), F32), jax.ShapeDtypeStruct((n, D_MODEL), F32),
                   jax.ShapeDtypeStruct((n, ROUTE_LANES), F32), jax.ShapeDtypeStruct((8, ROUTE_LANES), F32)],
        scratch_shapes=[pltpu.VMEM((8, ROUTE_LANES), F32)],
        compiler_params=_cparams(1),
        name="post",
    )(x, gm, km, vm, wq, wo, gf, wr_hi, wr_lo, br)


MOE_TB = 256


def _row_copy(src, src_row, dst, dst_row, sem):
    return pltpu.make_async_copy(src.at[pl.ds(src_row, 1), :], dst.at[pl.ds(dst_row, 1), :], sem)


def _dispatch_kernel(dest_ref, t_ref, xs_in_ref, xs_ref, sem):
    del xs_in_ref
    base = pl.program_id(0) * MOE_TB

    def issue(r, c):
        for k in range(2):
            _row_copy(t_ref, r, xs_ref, dest_ref[2 * (base + r) + k], sem).start()
        return c

    lax.fori_loop(0, MOE_TB, issue, 0, unroll=8)
    for k in range(2):
        pltpu.make_async_copy(t_ref, xs_ref.at[pl.ds(0, MOE_TB), :], sem).wait()


def _dispatch(dest, t, n_slots):
    n = t.shape[0]
    xs0 = jnp.zeros((n_slots, D_MODEL), F32)
    return pl.pallas_call(
        _dispatch_kernel,
        grid_spec=pltpu.PrefetchScalarGridSpec(
            num_scalar_prefetch=1,
            grid=(n // MOE_TB,),
            in_specs=[pl.BlockSpec((MOE_TB, D_MODEL), lambda i, d: (i, 0)), pl.BlockSpec(memory_space=pl.ANY)],
            out_specs=pl.BlockSpec(memory_space=pl.ANY),
            scratch_shapes=[pltpu.SemaphoreType.DMA(())],
        ),
        out_shape=jax.ShapeDtypeStruct((n_slots, D_MODEL), F32),
        input_output_aliases={2: 0},
        compiler_params=_cparams(1),
        name="dispatch",
    )(dest, t, xs0)


def _expert_kernel(be_ref, na_ref, xs_ref, w1_ref, w3_ref, w2_ref, ys_ref, w1b, w3b, w2b):
    i = pl.program_id(0)
    active = i < na_ref[0]
    new_expert = (i == 0) | (be_ref[i] != be_ref[jnp.maximum(i - 1, 0)])

    @pl.when(active & new_expert)
    def _():
        w1b[...] = w1_ref[...].astype(BF16)
        w3b[...] = w3_ref[...].astype(BF16)
        w2b[...] = w2_ref[...].astype(BF16)

    @pl.when(active)
    def _():
        x = xs_ref[...].astype(BF16)
        a = _dot(x, w1b[...])
        b = _dot(x, w3b[...])
        ys_ref[...] = _dot((jax.nn.silu(a) * b).astype(BF16), w2b[...])

    @pl.when(jnp.logical_not(active))
    def _():
        ys_ref[...] = jnp.zeros_like(ys_ref)


def _experts(block_e, n_active, xs, w1, w3, w2, layer):
    n_blocks = xs.shape[0] // MOE_BLOCK
    blk = lambda i, na: jnp.minimum(i, jnp.maximum(na[0] - 1, 0))
    wspec = lambda rows, cols: pl.BlockSpec((None, None, rows, cols), lambda i, be, na: (layer, be[blk(i, na)], 0, 0))
    return pl.pallas_call(
        _expert_kernel,
        grid_spec=pltpu.PrefetchScalarGridSpec(
            num_scalar_prefetch=2,
            grid=(n_blocks,),
            in_specs=[
                pl.BlockSpec((MOE_BLOCK, D_MODEL), lambda i, be, na: (blk(i, na), 0)),
                wspec(D_MODEL, D_EXPERT), wspec(D_MODEL, D_EXPERT), wspec(D_EXPERT, D_MODEL),
            ],
            out_specs=pl.BlockSpec((MOE_BLOCK, D_MODEL), lambda i, be, na: (i, 0)),
            scratch_shapes=[pltpu.VMEM((D_MODEL, D_EXPERT), BF16), pltpu.VMEM((D_MODEL, D_EXPERT), BF16),
                            pltpu.VMEM((D_EXPERT, D_MODEL), BF16)],
        ),
        out_shape=jax.ShapeDtypeStruct(xs.shape, F32),
        compiler_params=_cparams(1),
        name="experts",
    )(block_e, n_active, xs, w1, w3, w2)


def _combine_kernel(dest_ref, x_ref, rec_ref, g_ref, ys_ref, o_ref, rows, sem, *, final):
    base = pl.program_id(0) * MOE_TB

    def issue(r, c):
        for k in range(2):
            _row_copy(ys_ref, dest_ref[2 * (base + r) + k], rows.at[k], r, sem).start()
        return c

    lax.fori_loop(0, MOE_TB, issue, 0, unroll=8)
    for k in range(2):
        pltpu.make_async_copy(ys_ref.at[pl.ds(0, MOE_TB), :], rows.at[k], sem).wait()
    rec = rec_ref[...]
    x = x_ref[...] + rec[:, 2:3] * rows[0] + rec[:, 3:4] * rows[1]
    o_ref[...] = _rms(x, g_ref[...]) if final else x


def _combine(dest, x, rec, ys, g, final):
    n = x.shape[0]
    row = lambda wd: pl.BlockSpec((MOE_TB, wd), lambda i, d: (i, 0))
    return pl.pallas_call(
        functools.partial(_combine_kernel, final=final),
        grid_spec=pltpu.PrefetchScalarGridSpec(
            num_scalar_prefetch=1,
            grid=(n // MOE_TB,),
            in_specs=[row(D_MODEL), row(ROUTE_LANES), pl.BlockSpec((1, D_MODEL), lambda i, d: (0, 0)),
                      pl.BlockSpec(memory_space=pl.ANY)],
            out_specs=row(D_MODEL),
            scratch_shapes=[pltpu.VMEM((2, MOE_TB, D_MODEL), F32), pltpu.SemaphoreType.DMA(())],
        ),
        out_shape=jax.ShapeDtypeStruct((n, D_MODEL), F32),
        compiler_params=_cparams(1),
        name="combine",
    )(dest, x, rec, g, ys)


def _moe(x, t, rec, cnt, w1, w3, w2, layer, g_final, final):
    n = x.shape[0]
    n_asg = 2 * n
    n_blocks = -(-(n_asg + N_EXPERTS * (MOE_BLOCK - 1)) // MOE_BLOCK)
    counts = cnt[0, N_GROUPS:N_GROUPS + N_EXPERTS].astype(jnp.int32)
    padded = (counts + MOE_BLOCK - 1) // MOE_BLOCK * MOE_BLOCK
    pad_end = jnp.cumsum(padded)
    pad_start = pad_end - padded
    expert = rec[:, 0:2].astype(jnp.int32)
    dest = (pad_start[expert] + rec[:, 4:6].astype(jnp.int32)).reshape(-1)
    starts = jnp.arange(n_blocks, dtype=jnp.int32) * MOE_BLOCK
    block_e = jnp.minimum(jnp.sum(pad_end[None, :] <= starts[:, None], axis=1), N_EXPERTS - 1).astype(jnp.int32)
    n_active = (pad_end[-1:] // MOE_BLOCK).astype(jnp.int32)
    xs = _dispatch(dest, t, n_blocks * MOE_BLOCK)
    ys = _experts(block_e, n_active, xs, w1, w3, w2, layer)
    return _combine(dest, x, rec, ys, g_final, final)


def _block_diag(w):
    eye = jnp.eye(LRU_BLOCKS, dtype=w.dtype)
    return jnp.einsum("kcd,kj->kcjd", w, eye).reshape(LRU_W, LRU_W)


def _router_weights(w_group, b_group, w_router, b_router):
    pad = ROUTE_LANES - N_GROUPS - N_EXPERTS
    wr = jnp.concatenate([w_group, w_router, jnp.zeros((D_MODEL, pad), F32)], axis=1)
    br = jnp.concatenate([b_group, b_router, jnp.zeros((pad,), F32)])[None, :]
    return wr, br


def kernel(x, mem, norm_mix, w_in, conv_w, conv_b, lru_wa, lru_ba, lru_wx, lru_bx, lru_lambda, ca_rel_bias, diff_lambda, diff_subln, w_branch, w_out, norm_mem, mem_norm, w_mq, w_mk, w_mv, w_mo, norm_ffn, w_group, b_group, w_router, b_router, w1, w3, w2, final_norm):
    b, n, d = x.shape
    assert b == 1 and d == D_MODEL
    xs = x[0]
    for l in range(DEPTH):
        lam_init = 0.8 - 0.6 * math.exp(-0.3 * l)
        z = _inproj(xs, norm_mix[l][None, :], w_in[l].astype(BF16))
        wcat = jnp.concatenate([_block_diag(lru_wa[l]), _block_diag(lru_wx[l])], axis=1).astype(BF16)
        bcat = jnp.concatenate([lru_ba[l], lru_bx[l]])[None, :]
        ya = _lru(z, conv_w[l], conv_b[l][None, :], wcat, bcat, lru_lambda[l][None, :])
        yb = _chunkattn(z, _chunk_bias(ca_rel_bias[l]))
        yc = _retention(z)
        yd = _diffattn(z, diff_lambda[l], diff_subln[l][None, :], lam_init)
        x1 = _merge(xs, z, ya, yb, yc, yd, w_branch[l].astype(BF16), w_out[l].astype(BF16))
        km, vm = _memkv(mem[0], mem_norm[None, :], w_mk[l].astype(BF16), w_mv[l].astype(BF16))
        wr, br = _router_weights(w_group[l], b_group[l], w_router[l], b_router[l])
        x2, t, rec, cnt = _post(x1, norm_mem[l][None, :], km, vm, w_mq[l].astype(BF16), w_mo[l].astype(BF16),
                                norm_ffn[l][None, :], wr, br)
        xs = _moe(x2, t, rec, cnt, w1, w3, w2, l, final_norm[None, :], final=(l == DEPTH - 1))
    return xs[None]
```

```python
import functools
import math

import numpy as np
import jax
import jax.numpy as jnp
from jax import lax
from jax.experimental import pallas as pl
from jax.experimental.pallas import tpu as pltpu

F32 = jnp.float32
BF16 = jnp.bfloat16

D_MODEL = 1024
DEPTH = 2
CHUNK = 64
NORM_EPS = 1e-6
GN_EPS = 1e-5
LRU_W = 512
LRU_BLOCKS = 8
LRU_BW = LRU_W // LRU_BLOCKS
CONV_W = 4
LRU_C = 8.0
CA_HEADS = 8
CA_HD = 64
CA_PREV = 8
REL_CLIP = 128
RET_HEADS = 8
RET_HD = 64
DIFF_HEADS = 4
DIFF_HD = 64
DIFF_VD = 2 * DIFF_HD
MX_HEADS = 4
MX_HD = 128
N_GROUPS = 4
EXP_PER_GROUP = 8
N_EXPERTS = N_GROUPS * EXP_PER_GROUP
D_EXPERT = 512
MOE_BLOCK = 256
N_BRANCH = 4
BRANCH_W = 512

COL_AX, COL_AG = 0, 512
COL_BQ, COL_BK, COL_BV = 1024, 1536, 2048
COL_CQ, COL_CK, COL_CV, COL_CG = 2560, 3072, 3584, 4096
COL_DQ, COL_DK, COL_DV = 4608, 5120, 5632
COL_GATE = 6144
IN_COLS = 10240

LANES = 128
NEG_BIG = -1e30
LOG2E = 1.0 / math.log(2.0)
VMEM_LIMIT = 56 * 1024 * 1024


def _cparams(n_axes):
    return pltpu.CompilerParams(dimension_semantics=("arbitrary",) * n_axes, vmem_limit_bytes=VMEM_LIMIT)


def _rms(x, g):
    return x * lax.rsqrt(jnp.mean(x * x, axis=-1, keepdims=True) + NORM_EPS) * g


def _dot(a, b):
    return jnp.dot(a, b, preferred_element_type=F32)


def _dot_nt(a, b):
    return lax.dot_general(a, b, (((1,), (1,)), ((), ())), preferred_element_type=F32)


def _dot_tn(a, b):
    return lax.dot_general(a, b, (((0,), (0,)), ((), ())), preferred_element_type=F32)


def _inproj_kernel(x_ref, g_ref, w_ref, o_ref, h_scr):
    @pl.when(pl.program_id(1) == 0)
    def _():
        h_scr[...] = _rms(x_ref[...], g_ref[...]).astype(BF16)

    tn = o_ref.shape[1]
    w = w_ref[:, pl.ds(pl.multiple_of(pl.program_id(1) * tn, tn), tn)]
    o_ref[...] = _dot(h_scr[...], w).astype(o_ref.dtype)


def _inproj(x, g, w, tm=512, tn=2560):
    n = x.shape[0]
    tm = min(tm, n)
    return pl.pallas_call(
        _inproj_kernel,
        grid=(n // tm, IN_COLS // tn),
        in_specs=[
            pl.BlockSpec((tm, D_MODEL), lambda i, j: (i, 0)),
            pl.BlockSpec((1, D_MODEL), lambda i, j: (0, 0)),
            pl.BlockSpec((D_MODEL, IN_COLS), lambda i, j: (0, 0), pipeline_mode=pl.Buffered(1)),
        ],
        out_specs=pl.BlockSpec((tm, tn), lambda i, j: (i, j)),
        out_shape=jax.ShapeDtypeStruct((n, IN_COLS), BF16),
        scratch_shapes=[pltpu.VMEM((tm, D_MODEL), BF16)],
        compiler_params=_cparams(2),
        name="inproj",
    )(x, g, w)


def _gelu_tanh(x):
    return 0.5 * x * (1.0 + jnp.tanh(math.sqrt(2.0 / math.pi) * (x + 0.044715 * x * x * x)))


def _lru_kernel(ax_ref, ag_ref, cw_ref, cb_ref, w_ref, b_ref, lam_ref, o_ref, xbuf, a_scr, u_scr, h_scr, hc):
    t = ax_ref.shape[0]

    @pl.when(pl.program_id(0) == 0)
    def _():
        xbuf[0:8, :] = jnp.zeros((8, LRU_W), F32)
        hc[...] = jnp.zeros_like(hc)

    x = ax_ref[...].astype(F32)
    xbuf[8:8 + t, :] = x
    xc = (cb_ref[...] + cw_ref[3:4, :] * x + cw_ref[2:3, :] * xbuf[7:7 + t, :]
          + cw_ref[1:2, :] * xbuf[6:6 + t, :] + cw_ref[0:1, :] * xbuf[5:5 + t, :])
    xbuf[0:8, :] = xbuf[t:t + 8, :]
    ri = _dot(xc.astype(BF16), w_ref[...]) + b_ref[...]
    r = jax.nn.sigmoid(ri[:, :LRU_W])
    ig = jax.nn.sigmoid(ri[:, LRU_W:])
    nl = -lam_ref[...]
    softplus = jnp.maximum(nl, 0.0) + jnp.log(1.0 + jnp.exp(-jnp.abs(nl)))
    log_a = -LRU_C * r * softplus
    a = jnp.exp(log_a)
    a_scr[...] = a
    u_scr[...] = jnp.sqrt(-jnp.tanh(log_a) * (a * a + 1.0)) * (ig * xc)

    def body(s, h):
        h = a_scr[pl.ds(s, 1), :] * h + u_scr[pl.ds(s, 1), :]
        h_scr[pl.ds(s, 1), :] = h
        return h

    hc[0:1, :] = lax.fori_loop(0, t, body, hc[0:1, :], unroll=8)
    o_ref[...] = (h_scr[...] * _gelu_tanh(ag_ref[...].astype(F32))).astype(o_ref.dtype)


def _lru(z, cw, cb, wcat, bcat, lam, t=256):
    n = z.shape[0]
    t = min(t, n)
    full = lambda shape: pl.BlockSpec(shape, lambda i: (0,) * len(shape))
    return pl.pallas_call(
        _lru_kernel,
        grid=(n // t,),
        in_specs=[
            pl.BlockSpec((t, LRU_W), lambda i: (i, COL_AX // LRU_W)),
            pl.BlockSpec((t, LRU_W), lambda i: (i, COL_AG // LRU_W)),
            full((CONV_W, LRU_W)), full((1, LRU_W)), full((LRU_W, 2 * LRU_W)), full((1, 2 * LRU_W)), full((1, LRU_W)),
        ],
        out_specs=pl.BlockSpec((t, LRU_W), lambda i: (i, 0)),
        out_shape=jax.ShapeDtypeStruct((n, LRU_W), BF16),
        scratch_shapes=[pltpu.VMEM((t + 8, LRU_W), F32), pltpu.VMEM((t, LRU_W), F32), pltpu.VMEM((t, LRU_W), F32),
                        pltpu.VMEM((t, LRU_W), F32), pltpu.VMEM((8, LRU_W), F32)],
        compiler_params=_cparams(1),
        name="lru",
    )(z, z, cw, cb, wcat, bcat, lam)


CA_QB = 256
CA_KW = CA_PREV * CHUNK + CA_QB


CA_ONES = 16
CA_PAIR = 2 * CA_HD


def _chunkattn_kernel(q_ref, k0_ref, k1_ref, k2_ref, vt0_ref, vt1_ref, vt2_ref, bias_ref, o_ref):
    qt = (q_ref[...].astype(F32) * (LOG2E * CA_HD ** -0.5)).T
    k = jnp.concatenate([k0_ref[...], k1_ref[...], k2_ref[...]], axis=0)
    feat = lax.broadcasted_iota(jnp.int32, (CA_PAIR, CA_QB), 0)
    for hp in range(CA_HEADS // 2):
        lanes = slice(hp * CA_PAIR, (hp + 1) * CA_PAIR)
        qtp = qt[lanes, :]
        qq = jnp.concatenate([jnp.where(feat < CA_HD, qtp, 0.0), jnp.where(feat >= CA_HD, qtp, 0.0)], axis=1)
        t = _dot(k[:, lanes], qq.astype(BF16))
        t = t + jnp.concatenate([bias_ref[0, 2 * hp], bias_ref[0, 2 * hp + 1]], axis=1)
        p = jnp.exp2(t - jnp.max(t, axis=0, keepdims=True)).astype(BF16)
        vt = jnp.concatenate([vt0_ref[hp], vt1_ref[hp], vt2_ref[hp]], axis=1)
        acc = _dot(vt, p)
        o1 = acc[0:CA_HD, 0:CA_QB] / acc[CA_PAIR:CA_PAIR + 1, 0:CA_QB]
        o2 = acc[CA_HD:CA_PAIR, CA_QB:] / acc[CA_PAIR:CA_PAIR + 1, CA_QB:]
        o_ref[:, lanes] = jnp.concatenate([o1, o2], axis=0).T.astype(o_ref.dtype)


def _chunk_bias(rel_table):
    span = CA_QB + CA_KW - 1
    n_hi = CA_KW - 1 - REL_CLIP
    n_lo = CA_QB - 1 - REL_CLIP
    e = jnp.concatenate([jnp.broadcast_to(rel_table[:, -1:], (CA_HEADS, n_hi)), rel_table[:, ::-1],
                         jnp.broadcast_to(rel_table[:, :1], (CA_HEADS, n_lo + 1))], axis=1)
    w = jnp.tile(e, (1, CA_QB))[:, :CA_QB * span].reshape(CA_HEADS, CA_QB, span)
    toeplitz = w[:, :, CA_QB - 1:CA_QB - 1 + CA_KW]
    cq = np.arange(CA_QB)[:, None] // CHUNK
    ck = np.arange(CA_KW)[None, :] // CHUNK
    band = (ck >= cq) & (ck <= cq + CA_PREV)
    first_valid = np.maximum((CA_KW - CA_QB) - CA_QB * np.arange(3), 0)
    ok = band[None] & (np.arange(CA_KW)[None, None, :] >= first_valid[:, None, None])
    bias = jnp.where(ok[:, None], (LOG2E * toeplitz.astype(F32))[None], NEG_BIG)
    return bias.transpose(0, 1, 3, 2)


def _chunkattn(z, bias):
    n = z.shape[0]
    w = CA_HEADS * CA_HD
    nb = n // CA_QB
    n_pair = CA_HEADS // 2
    vt = z[:, COL_BV:COL_BV + w].reshape(n, n_pair, CA_PAIR).transpose(1, 2, 0)
    vt = jnp.concatenate([vt, jnp.ones((n_pair, CA_ONES, n), BF16)], axis=1)
    kspec = lambda back: pl.BlockSpec((CA_QB, w), lambda i: (jnp.maximum(i - back, 0), COL_BK // w))
    vspec = lambda back: pl.BlockSpec((n_pair, CA_PAIR + CA_ONES, CA_QB), lambda i: (0, 0, jnp.maximum(i - back, 0)))
    return pl.pallas_call(
        _chunkattn_kernel,
        grid=(nb,),
        in_specs=[
            pl.BlockSpec((CA_QB, w), lambda i: (i, COL_BQ // w)),
            kspec(2), kspec(1), kspec(0), vspec(2), vspec(1), vspec(0),
            pl.BlockSpec((1, CA_HEADS, CA_KW, CA_QB), lambda i: (jnp.minimum(i, 2), 0, 0, 0),
                         pipeline_mode=pl.Buffered(1)),
        ],
        out_specs=pl.BlockSpec((CA_QB, w), lambda i: (i, 0)),
        out_shape=jax.ShapeDtypeStruct((n, w), BF16),
        compiler_params=_cparams(1),
        name="chunkattn",
    )(z, z, z, z, vt, vt, vt, bias)


RET_T = 256
_RET_LOG_G = np.log(1.0 - 2.0 ** (-5.0 - np.arange(RET_HEADS)))


def _retention_consts(t):
    pos = np.arange(t)
    diff = pos[:, None] - pos[None, :]
    dmat = np.where(diff[None] >= 0, np.exp(_RET_LOG_G[:, None, None] * np.maximum(diff, 0)[None]), 0.0)
    dmat = dmat * (RET_HD ** -0.5)
    zeta = np.exp(_RET_LOG_G[None, :] * (t - 1 - pos)[:, None]) * (RET_HD ** -0.5)
    xi = np.exp(_RET_LOG_G[None, :] * (pos + 1)[:, None])
    rep = lambda m: np.repeat(m, RET_HD, axis=1)
    return (jnp.asarray(dmat, F32), jnp.asarray(rep(zeta), F32), jnp.asarray(rep(xi), F32))


def _retention_kernel(q_ref, k_ref, v_ref, g_ref, dmat_ref, zeta_ref, xi_ref, o_ref, s_scr):
    t = q_ref.shape[0]

    @pl.when(pl.program_id(0) == 0)
    def _():
        s_scr[...] = jnp.zeros_like(s_scr)

    q = q_ref[...]
    k = k_ref[...]
    v = v_ref[...]
    kz = (k.astype(F32) * zeta_ref[...]).astype(BF16)
    xi = xi_ref[...]
    g = g_ref[...].astype(F32)
    outs = []
    for h in range(RET_HEADS):
        sl = slice(h * RET_HD, (h + 1) * RET_HD)
        qh, vh = q[:, sl], v[:, sl]
        inner = _dot_nt(qh, k[:, sl]) * dmat_ref[h]
        state = s_scr[h]
        o = _dot(inner.astype(BF16), vh) + _dot(qh, state.astype(BF16)) * xi[:, sl]
        s_scr[h] = float(np.exp(_RET_LOG_G[h] * t)) * state + _dot_tn(kz[:, sl], vh)
        mu = jnp.mean(o, axis=-1, keepdims=True)
        d = o - mu
        var = jnp.mean(d * d, axis=-1, keepdims=True)
        outs.append(d * lax.rsqrt(var + GN_EPS))
    o_ref[...] = (jax.nn.silu(g) * jnp.concatenate(outs, axis=1)).astype(o_ref.dtype)


def _retention(z):
    n = z.shape[0]
    t = min(RET_T, n)
    w = RET_HEADS * RET_HD
    dmat, zeta, xi = _retention_consts(t)
    col = lambda c: pl.BlockSpec((t, w), lambda i: (i, c // w))
    return pl.pallas_call(
        _retention_kernel,
        grid=(n // t,),
        in_specs=[col(COL_CQ), col(COL_CK), col(COL_CV), col(COL_CG),
                  pl.BlockSpec((RET_HEADS, t, t), lambda i: (0, 0, 0)),
                  pl.BlockSpec((t, w), lambda i: (0, 0)), pl.BlockSpec((t, w), lambda i: (0, 0))],
        out_specs=pl.BlockSpec((t, w), lambda i: (i, 0)),
        out_shape=jax.ShapeDtypeStruct((n, w), BF16),
        scratch_shapes=[pltpu.VMEM((RET_HEADS, RET_HD, RET_HD), F32)],
        compiler_params=_cparams(1),
        name="retention",
    )(z, z, z, z, dmat, zeta, xi)


DA_B = 256
DA_UNROLLS = (8,)
DA_TAIL = 2
DA_HG = 2
DA_ONES = 16
_DA_SLOPES = (2.0 ** (-8.0 * np.arange(1, DIFF_HEADS + 1) / DIFF_HEADS)).astype(np.float32)


def _bf16_parts(x, n):
    parts = []
    for _ in range(n):
        p = float(np.asarray(x, np.float32).astype(jnp.bfloat16).astype(np.float32))
        parts.append(p)
        x = x - p
    return parts


_LOG2E_PARTS = _bf16_parts(LOG2E, 3)
_DA_DIGITS = -(-(DA_B - 1).bit_length() // 8)


def _diffattn_consts():
    kloc = np.arange(DA_B)[:, None]
    qloc = (np.arange(2 * DA_B) % DA_B)[None, :]
    slopes = _DA_SLOPES[:, None, None].astype(np.float64)
    kfeat = np.zeros((DIFF_HEADS, DA_B, DIFF_VD))
    for d in range(_DA_DIGITS):
        cols = slice(d * len(_LOG2E_PARTS), (d + 1) * len(_LOG2E_PARTS))
        kfeat[:, :, cols] = slopes * (((kloc >> (8 * d)) & 255) << (8 * d))
    diag = np.where((kloc // CHUNK) <= (qloc // CHUNK), LOG2E * slopes * (qloc - np.abs(qloc - kloc) - kloc), NEG_BIG)
    bias = np.stack([np.zeros_like(diag), diag, np.full_like(diag, NEG_BIG)], axis=1)
    return jnp.asarray(kfeat, BF16), jnp.asarray(bias, F32)


def _diffattn_kernel(slope_ref, q_ref, k_ref, vt_ref, kfeat_ref, bias_ref, lam_ref, g_ref, o_ref,
                     qq_scr, m_scr, acc_scr, s_scr, p_scr, a_scr, *, lam_init):
    i = pl.program_id(1)
    blk = DA_B
    heads = range(DA_HG)
    feat = lax.broadcasted_iota(jnp.int32, (DIFF_VD, blk), 0)
    for hh in heads:
        qt = (q_ref[:, hh * DIFF_VD:(hh + 1) * DIFF_VD].astype(F32) * (LOG2E * DIFF_HD ** -0.5)).T
        qq_scr[hh, 0:DIFF_VD, 0:blk] = jnp.where(feat < DIFF_HD, qt, 0.0).astype(BF16)
        qq_scr[hh, 0:DIFF_VD, blk:] = jnp.where(feat >= DIFF_HD, qt, 0.0).astype(BF16)

    @pl.when(i == 0)
    def _():
        frow = lax.broadcasted_iota(jnp.int32, (DIFF_VD, 2 * blk), 0)
        qfeat = jnp.zeros((DIFF_VD, 2 * blk), F32)
        for idx, part in enumerate(_LOG2E_PARTS * _DA_DIGITS):
            qfeat = jnp.where(frow == idx, part, qfeat)
        for hh in heads:
            qq_scr[hh, DIFF_VD:, :] = qfeat.astype(BF16)

    m_scr[...] = jnp.full_like(m_scr, NEG_BIG)
    acc_scr[...] = jnp.zeros_like(acc_scr)

    def scores(hh, j):
        k = k_ref[pl.ds(pl.multiple_of(j * blk, blk), blk), hh * DIFF_VD:(hh + 1) * DIFF_VD]
        return _dot(jnp.concatenate([k, kfeat_ref[hh]], axis=1), qq_scr[hh])

    def softmax_step(hh, j, slot, tail):
        slope = slope_ref[pl.program_id(0) * DA_HG + hh]
        offset = slope * LOG2E * (j * blk).astype(F32)
        t = s_scr[hh, slot]
        if tail:
            t = t + bias_ref[hh, jnp.where(j < i, 0, jnp.where(j == i, 1, 2))]
        m_old = m_scr[hh]
        m_new = jnp.maximum(m_old, jnp.max(t, axis=0, keepdims=True) + offset)
        m_scr[hh] = m_new
        p_scr[hh, slot] = jnp.exp2(t - (m_new - offset)).astype(BF16)
        a_scr[hh, slot] = jnp.exp2(m_old - m_new)

    def accumulate(hh, j, slot):
        vt = vt_ref[hh, :, pl.ds(pl.multiple_of(j * blk, blk), blk)]
        acc_scr[hh] = a_scr[hh, slot] * acc_scr[hh] + _dot(vt, p_scr[hh, slot])

    def stage(j, slot, tail):
        for hh in heads:
            s_scr[hh, 1 - slot] = scores(hh, jnp.minimum(j + 1, i))
            accumulate(hh, jnp.clip(j - 1, 0, i), 1 - slot)
            softmax_step(hh, j, slot, tail)

    for hh in heads:
        s_scr[hh, 0] = scores(hh, 0)
        p_scr[hh, 1] = jnp.zeros((blk, 2 * blk), BF16)
        a_scr[hh, 1] = jnp.ones((1, 2 * blk), F32)

    def run(first, trips, unroll, tail):
        def body(jj, c):
            for u in range(unroll):
                stage(first + unroll * jj + u, u % 2, tail)
            return c

        lax.fori_loop(0, trips, body, 0)
        return first + unroll * trips

    done = 0
    for unroll in DA_UNROLLS:
        done = run(done, (i - done) // unroll, unroll, False)
    run(done, (i - done + DA_TAIL) // DA_TAIL, DA_TAIL, True)

    lv = lam_ref[...]
    lam = (jnp.exp(jnp.sum(lv[0:1, :] * lv[1:2, :], axis=-1, keepdims=True))
           - jnp.exp(jnp.sum(lv[2:3, :] * lv[3:4, :], axis=-1, keepdims=True)) + lam_init)
    for hh in heads:
        accumulate(hh, i, 1)
        o = acc_scr[hh, 0:DIFF_VD, :] / acc_scr[hh, DIFF_VD:DIFF_VD + 1, :]
        od = o[:, 0:blk] - lam * o[:, blk:]
        od = od * lax.rsqrt(jnp.mean(od * od, axis=0, keepdims=True) + GN_EPS)
        o_ref[:, hh * DIFF_VD:(hh + 1) * DIFF_VD] = (od.T * g_ref[...] * (1.0 - lam_init)).astype(o_ref.dtype)


def _diffattn(z, lam_vecs, subln_g, lam_init):
    n = z.shape[0]
    assert n % DA_B == 0 and DIFF_HEADS % DA_HG == 0
    w = DIFF_VD
    gw = DA_HG * w
    kfeat, bias = _diffattn_consts()
    vt = z[:, COL_DV:COL_DV + DIFF_HEADS * w].reshape(n, DIFF_HEADS, w).transpose(1, 2, 0)
    vt = jnp.concatenate([vt, jnp.ones((DIFF_HEADS, DA_ONES, n), BF16)], axis=1)
    return pl.pallas_call(
        functools.partial(_diffattn_kernel, lam_init=lam_init),
        grid_spec=pltpu.PrefetchScalarGridSpec(
            num_scalar_prefetch=1,
            grid=(DIFF_HEADS // DA_HG, n // DA_B),
            in_specs=[
                pl.BlockSpec((DA_B, gw), lambda h, i, s: (i, COL_DQ // gw + h)),
                pl.BlockSpec((n, gw), lambda h, i, s: (0, COL_DK // gw + h)),
                pl.BlockSpec((DA_HG, w + DA_ONES, n), lambda h, i, s: (h, 0, 0)),
                pl.BlockSpec((DA_HG, DA_B, w), lambda h, i, s: (h, 0, 0)),
                pl.BlockSpec((DA_HG, 3, DA_B, 2 * DA_B), lambda h, i, s: (h, 0, 0, 0), pipeline_mode=pl.Buffered(1)),
                pl.BlockSpec((4, DIFF_HD), lambda h, i, s: (0, 0)),
                pl.BlockSpec((1, w), lambda h, i, s: (0, 0)),
            ],
            out_specs=pl.BlockSpec((DA_B, gw), lambda h, i, s: (i, h)),
            scratch_shapes=[pltpu.VMEM((DA_HG, 2 * w, 2 * DA_B), BF16), pltpu.VMEM((DA_HG, 1, 2 * DA_B), F32),
                            pltpu.VMEM((DA_HG, w + DA_ONES, 2 * DA_B), F32),
                            pltpu.VMEM((DA_HG, 2, DA_B, 2 * DA_B), F32), pltpu.VMEM((DA_HG, 2, DA_B, 2 * DA_B), BF16),
                            pltpu.VMEM((DA_HG, 2, 1, 2 * DA_B), F32)],
        ),
        out_shape=jax.ShapeDtypeStruct((n, DIFF_HEADS * w), BF16),
        compiler_params=_cparams(2),
        name="diffattn",
    )(jnp.asarray(_DA_SLOPES), z, z, vt, kfeat, bias, lam_vecs, subln_g)


def _merge_kernel(x_ref, ya_ref, yb_ref, yc_ref, yd_ref, g0_ref, g1_ref, g2_ref, g3_ref, wb_ref, wo_ref, o_ref):
    mixed = None
    for y_ref, g_ref, b in ((ya_ref, g0_ref, 0), (yb_ref, g1_ref, 1), (yc_ref, g2_ref, 2), (yd_ref, g3_ref, 3)):
        term = jax.nn.sigmoid(g_ref[...].astype(F32)) * _dot(y_ref[...], wb_ref[b])
        mixed = term if mixed is None else mixed + term
    o_ref[...] = x_ref[...] + _dot(mixed.astype(BF16), wo_ref[...])


def _merge(x, z, ya, yb, yc, yd, wb, wo, tm=512):
    n = x.shape[0]
    tm = min(tm, n)
    row = lambda w: pl.BlockSpec((tm, w), lambda i: (i, 0))
    gate = lambda b: pl.BlockSpec((tm, D_MODEL), lambda i: (i, COL_GATE // D_MODEL + b))
    return pl.pallas_call(
        _merge_kernel,
        grid=(n // tm,),
        in_specs=[row(D_MODEL), row(BRANCH_W), row(BRANCH_W), row(BRANCH_W), row(BRANCH_W),
                  gate(0), gate(1), gate(2), gate(3),
                  pl.BlockSpec((N_BRANCH, BRANCH_W, D_MODEL), lambda i: (0, 0, 0)),
                  pl.BlockSpec((D_MODEL, D_MODEL), lambda i: (0, 0))],
        out_specs=row(D_MODEL),
        out_shape=jax.ShapeDtypeStruct((n, D_MODEL), F32),
        compiler_params=_cparams(1),
        name="merge",
    )(x, ya, yb, yc, yd, z, z, z, z, wb, wo)


def _memkv_kernel(mem_ref, g_ref, wk_ref, wv_ref, k_ref, v_ref):
    mn = _rms(mem_ref[...], g_ref[...]).astype(BF16)
    k_ref[...] = _dot(mn, wk_ref[...]).astype(BF16)
    v_ref[...] = _dot(mn, wv_ref[...]).astype(BF16)


def _memkv(mem, g, wk, wv):
    m = mem.shape[0]
    w = MX_HEADS * MX_HD
    out = jax.ShapeDtypeStruct((m, w), BF16)
    return pl.pallas_call(_memkv_kernel, out_shape=(out, out), name="memkv",
                          compiler_params=pltpu.CompilerParams(vmem_limit_bytes=VMEM_LIMIT))(mem, g, wk, wv)


ROUTE_LANES = LANES


def _route(logits, carry):
    t = logits.shape[0]
    lane_i = lax.broadcasted_iota(jnp.int32, (t, ROUTE_LANES), 1)
    lane = lane_i.astype(F32)
    big = float(ROUTE_LANES)
    gl = jnp.where(lane_i < N_GROUPS, logits, NEG_BIG)
    gmax = jnp.max(gl, axis=-1, keepdims=True)
    gsum = jnp.sum(jnp.exp(gl - gmax), axis=-1, keepdims=True)
    g_sel = jnp.min(jnp.where(gl == gmax, lane, big), axis=-1, keepdims=True)
    g_prob = 1.0 / gsum
    lo = N_GROUPS + EXP_PER_GROUP * g_sel
    el = jnp.where((lane >= lo) & (lane < lo + EXP_PER_GROUP), logits, NEG_BIG)
    e1 = jnp.max(el, axis=-1, keepdims=True)
    i1 = jnp.min(jnp.where(el == e1, lane, big), axis=-1, keepdims=True)
    el2 = jnp.where(lane == i1, NEG_BIG, el)
    e2 = jnp.max(el2, axis=-1, keepdims=True)
    i2 = jnp.min(jnp.where(el2 == e2, lane, big), axis=-1, keepdims=True)
    esum = jnp.sum(jnp.exp(el - e1), axis=-1, keepdims=True)
    p1 = 1.0 / esum
    p2 = jnp.exp(e2 - e1) / esum
    w1 = p1 / (p1 + p2) * g_prob
    w2 = p2 / (p1 + p2) * g_prob
    hot1 = lane == i1
    hot2 = lane == i2
    cnt = jnp.where(hot1 | hot2, 1.0, 0.0)
    r = lax.broadcasted_iota(jnp.int32, (t, t), 0)
    c = lax.broadcasted_iota(jnp.int32, (t, t), 1)
    before = jnp.where(c < r, 1.0, 0.0).astype(BF16)
    prefix = _dot(before, cnt.astype(BF16)) + carry
    rank1 = jnp.sum(jnp.where(hot1, prefix, 0.0), axis=-1, keepdims=True)
    rank2 = jnp.sum(jnp.where(hot2, prefix, 0.0), axis=-1, keepdims=True)
    rec = jnp.zeros((t, ROUTE_LANES), F32)
    for idx, val in enumerate((i1 - N_GROUPS, i2 - N_GROUPS, w1, w2, rank1, rank2)):
        rec = jnp.where(lane_i == idx, val, rec)
    return rec, carry + jnp.sum(cnt, axis=0, keepdims=True)


POST_SUB = 256


def _post_kernel(x_ref, gm_ref, km_ref, vm_ref, wq_ref, wo_ref, gf_ref, wrh_ref, wrl_ref, br_ref,
                 xo_ref, t_ref, rec_ref, cnt_ref, carry):
    @pl.when(pl.program_id(0) == 0)
    def _():
        carry[...] = jnp.zeros_like(carry)

    km = km_ref[...]
    vm = vm_ref[...]
    logits = []
    for sub in range(x_ref.shape[0] // POST_SUB):
        rows = slice(sub * POST_SUB, (sub + 1) * POST_SUB)
        x = x_ref[rows, :]
        q = _dot(_rms(x, gm_ref[...]).astype(BF16), wq_ref[...]).astype(BF16)
        outs = []
        for h in range(MX_HEADS):
            sl = slice(h * MX_HD, (h + 1) * MX_HD)
            s = _dot_nt(q[:, sl], km[:, sl]) * (MX_HD ** -0.5)
            p = jnp.exp(s - jnp.max(s, axis=-1, keepdims=True))
            p = p / jnp.sum(p, axis=-1, keepdims=True)
            outs.append(_dot(p.astype(BF16), vm[:, sl]))
        x = x + _dot(jnp.concatenate(outs, axis=1).astype(BF16), wo_ref[...])
        xo_ref[rows, :] = x
        t = _rms(x, gf_ref[...])
        t_ref[rows, :] = t
        t_hi = t.astype(BF16)
        t_lo = (t - t_hi.astype(F32)).astype(BF16)
        logits.append(_dot(t_hi, wrh_ref[...]) + (_dot(t_lo, wrh_ref[...]) + _dot(t_hi, wrl_ref[...])) + br_ref[...])
    running = carry[0:1, :]
    for sub, lg in enumerate(logits):
        rec, running = _route(lg, running)
        rec_ref[sub * POST_SUB:(sub + 1) * POST_SUB, :] = rec
    carry[0:1, :] = running
    cnt_ref[...] = jnp.broadcast_to(running, cnt_ref.shape)


def _post(x, gm, km, vm, wq, wo, gf, wr, br, tm=512):
    n = x.shape[0]
    tm = min(tm, n)
    assert tm % POST_SUB == 0
    w = MX_HEADS * MX_HD
    m = km.shape[0]
    wr_hi = wr.astype(BF16)
    wr_lo = (wr - wr_hi.astype(F32)).astype(BF16)
    row = lambda wd: pl.BlockSpec((tm, wd), lambda i: (i, 0))
    full = lambda shape: pl.BlockSpec(shape, lambda i: (0,) * len(shape))
    return pl.pallas_call(
        _post_kernel,
        grid=(n // tm,),
        in_specs=[row(D_MODEL), full((1, D_MODEL)), full((m, w)), full((m, w)), full((D_MODEL, w)), full((w, D_MODEL)),
                  full((1, D_MODEL)), full((D_MODEL, ROUTE_LANES)), full((D_MODEL, ROUTE_LANES)),
                  full((1, ROUTE_LANES))],
        out_specs=[row(D_MODEL), row(D_MODEL), row(ROUTE_LANES), full((8, ROUTE_LANES))],
        out_shape=[jax.ShapeDtypeStruct((n, D_MODEL), F32), jax.ShapeDtypeStruct((n, D_MODEL), F32),
                   jax.ShapeDtypeStruct((n, ROUTE_LANES), F32), jax.ShapeDtypeStruct((8, ROUTE_LANES), F32)],
        scratch_shapes=[pltpu.VMEM((8, ROUTE_LANES), F32)],
        compiler_params=_cparams(1),
        name="post",
    )(x, gm, km, vm, wq, wo, gf, wr_hi, wr_lo, br)


MOE_TB = 256


def _row_copy(src, src_row, dst, dst_row, sem):
    return pltpu.make_async_copy(src.at[pl.ds(src_row, 1), :], dst.at[pl.ds(dst_row, 1), :], sem)


def _dispatch_kernel(dest_ref, t_ref, xs_in_ref, xs_ref, sem):
    del xs_in_ref
    base = pl.program_id(0) * MOE_TB

    def issue(r, c):
        for k in range(2):
            _row_copy(t_ref, r, xs_ref, dest_ref[2 * (base + r) + k], sem).start(priority=k)
        return c

    lax.fori_loop(0, MOE_TB, issue, 0, unroll=8)
    for k in range(2):
        pltpu.make_async_copy(t_ref, xs_ref.at[pl.ds(0, MOE_TB), :], sem).wait()


def _dispatch(dest, t, n_slots):
    n = t.shape[0]
    xs0 = jnp.zeros((n_slots, D_MODEL), F32)
    return pl.pallas_call(
        _dispatch_kernel,
        grid_spec=pltpu.PrefetchScalarGridSpec(
            num_scalar_prefetch=1,
            grid=(n // MOE_TB,),
            in_specs=[pl.BlockSpec((MOE_TB, D_MODEL), lambda i, d: (i, 0)), pl.BlockSpec(memory_space=pl.ANY)],
            out_specs=pl.BlockSpec(memory_space=pl.ANY),
            scratch_shapes=[pltpu.SemaphoreType.DMA(())],
        ),
        out_shape=jax.ShapeDtypeStruct((n_slots, D_MODEL), F32),
        input_output_aliases={2: 0},
        compiler_params=_cparams(1),
        name="dispatch",
    )(dest, t, xs0)


def _expert_kernel(be_ref, na_ref, xs_ref, w1_ref, w3_ref, w2_ref, ys_ref, w1b, w3b, w2b):
    i = pl.program_id(0)
    active = i < na_ref[0]
    new_expert = (i == 0) | (be_ref[i] != be_ref[jnp.maximum(i - 1, 0)])

    @pl.when(active & new_expert)
    def _():
        w1b[...] = w1_ref[...].astype(BF16)
        w3b[...] = w3_ref[...].astype(BF16)
        w2b[...] = w2_ref[...].astype(BF16)

    @pl.when(active)
    def _():
        x = xs_ref[...].astype(BF16)
        a = _dot(x, w1b[...])
        b = _dot(x, w3b[...])
        ys_ref[...] = _dot((jax.nn.silu(a) * b).astype(BF16), w2b[...])

    @pl.when(jnp.logical_not(active))
    def _():
        ys_ref[...] = jnp.zeros_like(ys_ref)


def _experts(block_e, n_active, xs, w1, w3, w2, layer):
    n_blocks = xs.shape[0] // MOE_BLOCK
    blk = lambda i, na: jnp.minimum(i, jnp.maximum(na[0] - 1, 0))
    wspec = lambda rows, cols: pl.BlockSpec((None, None, rows, cols), lambda i, be, na: (layer, be[blk(i, na)], 0, 0))
    return pl.pallas_call(
        _expert_kernel,
        grid_spec=pltpu.PrefetchScalarGridSpec(
            num_scalar_prefetch=2,
            grid=(n_blocks,),
            in_specs=[
                pl.BlockSpec((MOE_BLOCK, D_MODEL), lambda i, be, na: (blk(i, na), 0)),
                wspec(D_MODEL, D_EXPERT), wspec(D_MODEL, D_EXPERT), wspec(D_EXPERT, D_MODEL),
            ],
            out_specs=pl.BlockSpec((MOE_BLOCK, D_MODEL), lambda i, be, na: (i, 0)),
            scratch_shapes=[pltpu.VMEM((D_MODEL, D_EXPERT), BF16), pltpu.VMEM((D_MODEL, D_EXPERT), BF16),
                            pltpu.VMEM((D_EXPERT, D_MODEL), BF16)],
        ),
        out_shape=jax.ShapeDtypeStruct(xs.shape, F32),
        compiler_params=_cparams(1),
        name="experts",
    )(block_e, n_active, xs, w1, w3, w2)


def _combine_kernel(dest_ref, x_ref, rec_ref, g_ref, ys_ref, o_ref, rows, sem, *, final):
    base = pl.program_id(0) * MOE_TB

    def issue(r, c):
        for k in range(2):
            _row_copy(ys_ref, dest_ref[2 * (base + r) + k], rows.at[k], r, sem).start(priority=k)
        return c

    lax.fori_loop(0, MOE_TB, issue, 0, unroll=8)
    for k in range(2):
        pltpu.make_async_copy(ys_ref.at[pl.ds(0, MOE_TB), :], rows.at[k], sem).wait()
    rec = rec_ref[...]
    x = x_ref[...] + rec[:, 2:3] * rows[0] + rec[:, 3:4] * rows[1]
    o_ref[...] = _rms(x, g_ref[...]) if final else x


def _combine(dest, x, rec, ys, g, final):
    n = x.shape[0]
    row = lambda wd: pl.BlockSpec((MOE_TB, wd), lambda i, d: (i, 0))
    return pl.pallas_call(
        functools.partial(_combine_kernel, final=final),
        grid_spec=pltpu.PrefetchScalarGridSpec(
            num_scalar_prefetch=1,
            grid=(n // MOE_TB,),
            in_specs=[row(D_MODEL), row(ROUTE_LANES), pl.BlockSpec((1, D_MODEL), lambda i, d: (0, 0)),
                      pl.BlockSpec(memory_space=pl.ANY)],
            out_specs=row(D_MODEL),
            scratch_shapes=[pltpu.VMEM((2, MOE_TB, D_MODEL), F32), pltpu.SemaphoreType.DMA(())],
        ),
        out_shape=jax.ShapeDtypeStruct((n, D_MODEL), F32),
        compiler_params=_cparams(1),
        name="combine",
    )(dest, x, rec, g, ys)


def _moe(x, t, rec, cnt, w1, w3, w2, layer, g_final, final):
    n = x.shape[0]
    n_asg = 2 * n
    n_blocks = -(-(n_asg + N_EXPERTS * (MOE_BLOCK - 1)) // MOE_BLOCK)
    counts = cnt[0, N_GROUPS:N_GROUPS + N_EXPERTS].astype(jnp.int32)
    padded = (counts + MOE_BLOCK - 1) // MOE_BLOCK * MOE_BLOCK
    pad_end = jnp.cumsum(padded)
    pad_start = pad_end - padded
    expert = rec[:, 0:2].astype(jnp.int32)
    dest = (pad_start[expert] + rec[:, 4:6].astype(jnp.int32)).reshape(-1)
    starts = jnp.arange(n_blocks, dtype=jnp.int32) * MOE_BLOCK
    block_e = jnp.minimum(jnp.sum(pad_end[None, :] <= starts[:, None], axis=1), N_EXPERTS - 1).astype(jnp.int32)
    n_active = (pad_end[-1:] // MOE_BLOCK).astype(jnp.int32)
    xs = _dispatch(dest, t, n_blocks * MOE_BLOCK)
    ys = _experts(block_e, n_active, xs, w1, w3, w2, layer)
    return _combine(dest, x, rec, ys, g_final, final)


def _block_diag(w):
    eye = jnp.eye(LRU_BLOCKS, dtype=w.dtype)
    return jnp.einsum("kcd,kj->kcjd", w, eye).reshape(LRU_W, LRU_W)


def _router_weights(w_group, b_group, w_router, b_router):
    pad = ROUTE_LANES - N_GROUPS - N_EXPERTS
    wr = jnp.concatenate([w_group, w_router, jnp.zeros((D_MODEL, pad), F32)], axis=1)
    br = jnp.concatenate([b_group, b_router, jnp.zeros((pad,), F32)])[None, :]
    return wr, br


def kernel(x, mem, norm_mix, w_in, conv_w, conv_b, lru_wa, lru_ba, lru_wx, lru_bx, lru_lambda, ca_rel_bias, diff_lambda, diff_subln, w_branch, w_out, norm_mem, mem_norm, w_mq, w_mk, w_mv, w_mo, norm_ffn, w_group, b_group, w_router, b_router, w1, w3, w2, final_norm):
    b, n, d = x.shape
    assert b == 1 and d == D_MODEL
    xs = x[0]
    for l in range(DEPTH):
        lam_init = 0.8 - 0.6 * math.exp(-0.3 * l)
        z = _inproj(xs, norm_mix[l][None, :], w_in[l].astype(BF16))
        wcat = jnp.concatenate([_block_diag(lru_wa[l]), _block_diag(lru_wx[l])], axis=1).astype(BF16)
        bcat = jnp.concatenate([lru_ba[l], lru_bx[l]])[None, :]
        ya = _lru(z, conv_w[l], conv_b[l][None, :], wcat, bcat, lru_lambda[l][None, :])
        yb = _chunkattn(z, _chunk_bias(ca_rel_bias[l]))
        yc = _retention(z)
        yd = _diffattn(z, diff_lambda[l], diff_subln[l][None, :], lam_init)
        x1 = _merge(xs, z, ya, yb, yc, yd, w_branch[l].astype(BF16), w_out[l].astype(BF16))
        km, vm = _memkv(mem[0], mem_norm[None, :], w_mk[l].astype(BF16), w_mv[l].astype(BF16))
        wr, br = _router_weights(w_group[l], b_group[l], w_router[l], b_router[l])
        x2, t, rec, cnt = _post(x1, norm_mem[l][None, :], km, vm, w_mq[l].astype(BF16), w_mo[l].astype(BF16),
                                norm_ffn[l][None, :], wr, br)
        xs = _moe(x2, t, rec, cnt, w1, w3, w2, l, final_norm[None, :], final=(l == DEPTH - 1))
    return xs[None]
```

```python
import functools
import math

import numpy as np
import jax
import jax.numpy as jnp
from jax import lax
from jax.experimental import pallas as pl
from jax.experimental.pallas import tpu as pltpu

F32 = jnp.float32
BF16 = jnp.bfloat16

D_MODEL = 1024
DEPTH = 2
CHUNK = 64
NORM_EPS = 1e-6
GN_EPS = 1e-5
LRU_W = 512
LRU_BLOCKS = 8
LRU_BW = LRU_W // LRU_BLOCKS
CONV_W = 4
LRU_C = 8.0
CA_HEADS = 8
CA_HD = 64
CA_PREV = 8
REL_CLIP = 128
RET_HEADS = 8
RET_HD = 64
DIFF_HEADS = 4
DIFF_HD = 64
DIFF_VD = 2 * DIFF_HD
MX_HEADS = 4
MX_HD = 128
N_GROUPS = 4
EXP_PER_GROUP = 8
N_EXPERTS = N_GROUPS * EXP_PER_GROUP
D_EXPERT = 512
MOE_BLOCK = 256
N_BRANCH = 4
BRANCH_W = 512

COL_AX, COL_AG = 0, 512
COL_BQ, COL_BK, COL_BV = 1024, 1536, 2048
COL_CQ, COL_CK, COL_CV, COL_CG = 2560, 3072, 3584, 4096
COL_DQ, COL_DK, COL_DV = 4608, 5120, 5632
COL_GATE = 6144
IN_COLS = 10240

LANES = 128
NEG_BIG = -1e30
LOG2E = 1.0 / math.log(2.0)
VMEM_LIMIT = 56 * 1024 * 1024


def _cparams(n_axes):
    return pltpu.CompilerParams(dimension_semantics=("arbitrary",) * n_axes, vmem_limit_bytes=VMEM_LIMIT)


def _rms(x, g):
    return x * lax.rsqrt(jnp.mean(x * x, axis=-1, keepdims=True) + NORM_EPS) * g


def _dot(a, b):
    return jnp.dot(a, b, preferred_element_type=F32)


def _dot_nt(a, b):
    return lax.dot_general(a, b, (((1,), (1,)), ((), ())), preferred_element_type=F32)


def _dot_tn(a, b):
    return lax.dot_general(a, b, (((0,), (0,)), ((), ())), preferred_element_type=F32)


def _inproj_kernel(x_ref, g_ref, w_ref, o_ref, vtb_ref, vtd_ref, h_scr):
    j = pl.program_id(1)

    @pl.when(j == 0)
    def _():
        h_scr[...] = _rms(x_ref[...], g_ref[...]).astype(BF16)

    tn = o_ref.shape[1]
    w = w_ref[:, pl.ds(pl.multiple_of(j * tn, tn), tn)]
    acc = _dot(h_scr[...], w)
    o_ref[...] = acc.astype(o_ref.dtype)
    for col, vt_ref in ((COL_BV, vtb_ref), (COL_DV, vtd_ref)):
        off = col % tn

        @pl.when(j == col // tn)
        def _():
            vt_ref[...] = acc[:, off:off + vt_ref.shape[0]].T.astype(vt_ref.dtype)


def _inproj(x, g, w, tm=512, tn=2560):
    n = x.shape[0]
    tm = min(tm, n)
    vw = CA_HEADS * CA_HD
    assert vw == DIFF_HEADS * DIFF_VD and all(c % tn + vw <= tn for c in (COL_BV, COL_DV))
    vt_spec = pl.BlockSpec((vw, tm), lambda i, j: (0, i))
    vt_shape = jax.ShapeDtypeStruct((vw, n), BF16)
    return pl.pallas_call(
        _inproj_kernel,
        grid=(n // tm, IN_COLS // tn),
        in_specs=[
            pl.BlockSpec((tm, D_MODEL), lambda i, j: (i, 0)),
            pl.BlockSpec((1, D_MODEL), lambda i, j: (0, 0)),
            pl.BlockSpec((D_MODEL, IN_COLS), lambda i, j: (0, 0), pipeline_mode=pl.Buffered(1)),
        ],
        out_specs=[pl.BlockSpec((tm, tn), lambda i, j: (i, j)), vt_spec, vt_spec],
        out_shape=[jax.ShapeDtypeStruct((n, IN_COLS), BF16), vt_shape, vt_shape],
        scratch_shapes=[pltpu.VMEM((tm, D_MODEL), BF16)],
        compiler_params=_cparams(2),
        name="inproj",
    )(x, g, w)


def _gelu_tanh(x):
    return 0.5 * x * (1.0 + jnp.tanh(math.sqrt(2.0 / math.pi) * (x + 0.044715 * x * x * x)))


def _lru_kernel(ax_ref, ag_ref, cw_ref, cb_ref, w_ref, b_ref, lam_ref, o_ref, xbuf, a_scr, u_scr, h_scr, hc):
    t = ax_ref.shape[0]

    @pl.when(pl.program_id(0) == 0)
    def _():
        xbuf[0:8, :] = jnp.zeros((8, LRU_W), F32)
        hc[...] = jnp.zeros_like(hc)

    x = ax_ref[...].astype(F32)
    xbuf[8:8 + t, :] = x
    xc = (cb_ref[...] + cw_ref[3:4, :] * x + cw_ref[2:3, :] * xbuf[7:7 + t, :]
          + cw_ref[1:2, :] * xbuf[6:6 + t, :] + cw_ref[0:1, :] * xbuf[5:5 + t, :])
    xbuf[0:8, :] = xbuf[t:t + 8, :]
    ri = _dot(xc.astype(BF16), w_ref[...]) + b_ref[...]
    r = jax.nn.sigmoid(ri[:, :LRU_W])
    ig = jax.nn.sigmoid(ri[:, LRU_W:])
    nl = -lam_ref[...]
    softplus = jnp.maximum(nl, 0.0) + jnp.log(1.0 + jnp.exp(-jnp.abs(nl)))
    log_a = -LRU_C * r * softplus
    a = jnp.exp(log_a)
    a_scr[...] = a
    u_scr[...] = jnp.sqrt(-jnp.tanh(log_a) * (a * a + 1.0)) * (ig * xc)

    def body(s, h):
        h = a_scr[pl.ds(s, 1), :] * h + u_scr[pl.ds(s, 1), :]
        h_scr[pl.ds(s, 1), :] = h
        return h

    hc[0:1, :] = lax.fori_loop(0, t, body, hc[0:1, :], unroll=8)
    o_ref[...] = (h_scr[...] * _gelu_tanh(ag_ref[...].astype(F32))).astype(o_ref.dtype)


def _lru(z, cw, cb, wcat, bcat, lam, t=256):
    n = z.shape[0]
    t = min(t, n)
    full = lambda shape: pl.BlockSpec(shape, lambda i: (0,) * len(shape))
    return pl.pallas_call(
        _lru_kernel,
        grid=(n // t,),
        in_specs=[
            pl.BlockSpec((t, LRU_W), lambda i: (i, COL_AX // LRU_W)),
            pl.BlockSpec((t, LRU_W), lambda i: (i, COL_AG // LRU_W)),
            full((CONV_W, LRU_W)), full((1, LRU_W)), full((LRU_W, 2 * LRU_W)), full((1, 2 * LRU_W)), full((1, LRU_W)),
        ],
        out_specs=pl.BlockSpec((t, LRU_W), lambda i: (i, 0)),
        out_shape=jax.ShapeDtypeStruct((n, LRU_W), BF16),
        scratch_shapes=[pltpu.VMEM((t + 8, LRU_W), F32), pltpu.VMEM((t, LRU_W), F32), pltpu.VMEM((t, LRU_W), F32),
                        pltpu.VMEM((t, LRU_W), F32), pltpu.VMEM((8, LRU_W), F32)],
        compiler_params=_cparams(1),
        name="lru",
    )(z, z, cw, cb, wcat, bcat, lam)


CA_QB = 256
CA_KW = CA_PREV * CHUNK + CA_QB


CA_ONES = 16
CA_PAIR = 2 * CA_HD


def _chunkattn_kernel(q_ref, k0_ref, k1_ref, k2_ref, vt0_ref, vt1_ref, vt2_ref, bias_ref, o_ref):
    qt = (q_ref[...].astype(F32) * (LOG2E * CA_HD ** -0.5)).T
    k = jnp.concatenate([k0_ref[...], k1_ref[...], k2_ref[...]], axis=0)
    feat = lax.broadcasted_iota(jnp.int32, (CA_PAIR, CA_QB), 0)
    for hp in range(CA_HEADS // 2):
        lanes = slice(hp * CA_PAIR, (hp + 1) * CA_PAIR)
        qtp = qt[lanes, :]
        qq = jnp.concatenate([jnp.where(feat < CA_HD, qtp, 0.0), jnp.where(feat >= CA_HD, qtp, 0.0)], axis=1)
        t = _dot(k[:, lanes], qq.astype(BF16))
        t = t + jnp.concatenate([bias_ref[0, 2 * hp], bias_ref[0, 2 * hp + 1]], axis=1)
        p = jnp.exp2(t - jnp.max(t, axis=0, keepdims=True)).astype(BF16)
        vt = jnp.concatenate([vt0_ref[hp], vt1_ref[hp], vt2_ref[hp]], axis=1)
        vt = jnp.concatenate([vt, jnp.ones((CA_ONES, CA_KW), BF16)], axis=0)
        acc = _dot(vt, p)
        o1 = acc[0:CA_HD, 0:CA_QB] / acc[CA_PAIR:CA_PAIR + 1, 0:CA_QB]
        o2 = acc[CA_HD:CA_PAIR, CA_QB:] / acc[CA_PAIR:CA_PAIR + 1, CA_QB:]
        o_ref[:, lanes] = jnp.concatenate([o1, o2], axis=0).T.astype(o_ref.dtype)


def _chunk_bias(rel_table):
    span = CA_QB + CA_KW - 1
    n_hi = CA_KW - 1 - REL_CLIP
    n_lo = CA_QB - 1 - REL_CLIP
    e = jnp.concatenate([jnp.broadcast_to(rel_table[:, -1:], (CA_HEADS, n_hi)), rel_table[:, ::-1],
                         jnp.broadcast_to(rel_table[:, :1], (CA_HEADS, n_lo + 1))], axis=1)
    w = jnp.tile(e, (1, CA_QB))[:, :CA_QB * span].reshape(CA_HEADS, CA_QB, span)
    toeplitz = w[:, :, CA_QB - 1:CA_QB - 1 + CA_KW]
    cq = np.arange(CA_QB)[:, None] // CHUNK
    ck = np.arange(CA_KW)[None, :] // CHUNK
    band = (ck >= cq) & (ck <= cq + CA_PREV)
    first_valid = np.maximum((CA_KW - CA_QB) - CA_QB * np.arange(3), 0)
    ok = band[None] & (np.arange(CA_KW)[None, None, :] >= first_valid[:, None, None])
    bias = jnp.where(ok[:, None], (LOG2E * toeplitz.astype(F32))[None], NEG_BIG)
    return bias.transpose(0, 1, 3, 2)


def _chunkattn(z, vt, bias):
    n = z.shape[0]
    w = CA_HEADS * CA_HD
    nb = n // CA_QB
    n_pair = CA_HEADS // 2
    vt = vt.reshape(n_pair, CA_PAIR, n)
    kspec = lambda back: pl.BlockSpec((CA_QB, w), lambda i: (jnp.maximum(i - back, 0), COL_BK // w))
    vspec = lambda back: pl.BlockSpec((n_pair, CA_PAIR, CA_QB), lambda i: (0, 0, jnp.maximum(i - back, 0)))
    return pl.pallas_call(
        _chunkattn_kernel,
        grid=(nb,),
        in_specs=[
            pl.BlockSpec((CA_QB, w), lambda i: (i, COL_BQ // w)),
            kspec(2), kspec(1), kspec(0), vspec(2), vspec(1), vspec(0),
            pl.BlockSpec((1, CA_HEADS, CA_KW, CA_QB), lambda i: (jnp.minimum(i, 2), 0, 0, 0),
                         pipeline_mode=pl.Buffered(1)),
        ],
        out_specs=pl.BlockSpec((CA_QB, w), lambda i: (i, 0)),
        out_shape=jax.ShapeDtypeStruct((n, w), BF16),
        compiler_params=_cparams(1),
        name="chunkattn",
    )(z, z, z, z, vt, vt, vt, bias)


RET_T = 256
_RET_LOG_G = np.log(1.0 - 2.0 ** (-5.0 - np.arange(RET_HEADS)))


def _retention_consts(t):
    pos = np.arange(t)
    diff = pos[:, None] - pos[None, :]
    dmat = np.where(diff[None] >= 0, np.exp(_RET_LOG_G[:, None, None] * np.maximum(diff, 0)[None]), 0.0)
    dmat = dmat * (RET_HD ** -0.5)
    zeta = np.exp(_RET_LOG_G[None, :] * (t - 1 - pos)[:, None]) * (RET_HD ** -0.5)
    xi = np.exp(_RET_LOG_G[None, :] * (pos + 1)[:, None])
    rep = lambda m: np.repeat(m, RET_HD, axis=1)
    return (jnp.asarray(dmat, F32), jnp.asarray(rep(zeta), F32), jnp.asarray(rep(xi), F32))


def _retention_kernel(q_ref, k_ref, v_ref, g_ref, dmat_ref, zeta_ref, xi_ref, o_ref, s_scr):
    t = q_ref.shape[0]

    @pl.when(pl.program_id(0) == 0)
    def _():
        s_scr[...] = jnp.zeros_like(s_scr)

    q = q_ref[...]
    k = k_ref[...]
    v = v_ref[...]
    kz = (k.astype(F32) * zeta_ref[...]).astype(BF16)
    xi = xi_ref[...]
    g = g_ref[...].astype(F32)
    outs = []
    for h in range(RET_HEADS):
        sl = slice(h * RET_HD, (h + 1) * RET_HD)
        qh, vh = q[:, sl], v[:, sl]
        inner = _dot_nt(qh, k[:, sl]) * dmat_ref[h]
        state = s_scr[h]
        o = _dot(inner.astype(BF16), vh) + _dot(qh, state.astype(BF16)) * xi[:, sl]
        s_scr[h] = float(np.exp(_RET_LOG_G[h] * t)) * state + _dot_tn(kz[:, sl], vh)
        mu = jnp.mean(o, axis=-1, keepdims=True)
        d = o - mu
        var = jnp.mean(d * d, axis=-1, keepdims=True)
        outs.append(d * lax.rsqrt(var + GN_EPS))
    o_ref[...] = (jax.nn.silu(g) * jnp.concatenate(outs, axis=1)).astype(o_ref.dtype)


def _retention(z):
    n = z.shape[0]
    t = min(RET_T, n)
    w = RET_HEADS * RET_HD
    dmat, zeta, xi = _retention_consts(t)
    col = lambda c: pl.BlockSpec((t, w), lambda i: (i, c // w))
    return pl.pallas_call(
        _retention_kernel,
        grid=(n // t,),
        in_specs=[col(COL_CQ), col(COL_CK), col(COL_CV), col(COL_CG),
                  pl.BlockSpec((RET_HEADS, t, t), lambda i: (0, 0, 0)),
                  pl.BlockSpec((t, w), lambda i: (0, 0)), pl.BlockSpec((t, w), lambda i: (0, 0))],
        out_specs=pl.BlockSpec((t, w), lambda i: (i, 0)),
        out_shape=jax.ShapeDtypeStruct((n, w), BF16),
        scratch_shapes=[pltpu.VMEM((RET_HEADS, RET_HD, RET_HD), F32)],
        compiler_params=_cparams(1),
        name="retention",
    )(z, z, z, z, dmat, zeta, xi)


DA_B = 256
DA_UNROLLS = (8,)
DA_TAIL = 2
DA_HG = 2
DA_ONES = 16
_DA_SLOPES = (2.0 ** (-8.0 * np.arange(1, DIFF_HEADS + 1) / DIFF_HEADS)).astype(np.float32)


def _bf16_parts(x, n):
    parts = []
    for _ in range(n):
        p = float(np.asarray(x, np.float32).astype(jnp.bfloat16).astype(np.float32))
        parts.append(p)
        x = x - p
    return parts


_LOG2E_PARTS = _bf16_parts(LOG2E, 3)
_DA_DIGITS = -(-(DA_B - 1).bit_length() // 8)


def _diffattn_consts():
    kloc = np.arange(DA_B)[:, None]
    qloc = (np.arange(2 * DA_B) % DA_B)[None, :]
    slopes = _DA_SLOPES[:, None, None].astype(np.float64)
    kfeat = np.zeros((DIFF_HEADS, DA_B, DIFF_VD))
    for d in range(_DA_DIGITS):
        cols = slice(d * len(_LOG2E_PARTS), (d + 1) * len(_LOG2E_PARTS))
        kfeat[:, :, cols] = slopes * (((kloc >> (8 * d)) & 255) << (8 * d))
    diag = np.where((kloc // CHUNK) <= (qloc // CHUNK), LOG2E * slopes * (qloc - np.abs(qloc - kloc) - kloc), NEG_BIG)
    bias = np.stack([np.zeros_like(diag), diag, np.full_like(diag, NEG_BIG)], axis=1)
    return jnp.asarray(kfeat, BF16), jnp.asarray(bias, F32)


def _diffattn_kernel(slope_ref, q_ref, k_ref, vt_ref, kfeat_ref, bias_ref, lam_ref, g_ref, o_ref,
                     qq_scr, m_scr, acc_scr, s_scr, p_scr, a_scr, *, lam_init):
    i = pl.program_id(1)
    blk = DA_B
    heads = range(DA_HG)
    feat = lax.broadcasted_iota(jnp.int32, (DIFF_VD, blk), 0)
    for hh in heads:
        qt = (q_ref[:, hh * DIFF_VD:(hh + 1) * DIFF_VD].astype(F32) * (LOG2E * DIFF_HD ** -0.5)).T
        qq_scr[hh, 0:DIFF_VD, 0:blk] = jnp.where(feat < DIFF_HD, qt, 0.0).astype(BF16)
        qq_scr[hh, 0:DIFF_VD, blk:] = jnp.where(feat >= DIFF_HD, qt, 0.0).astype(BF16)

    @pl.when(i == 0)
    def _():
        frow = lax.broadcasted_iota(jnp.int32, (DIFF_VD, 2 * blk), 0)
        qfeat = jnp.zeros((DIFF_VD, 2 * blk), F32)
        for idx, part in enumerate(_LOG2E_PARTS * _DA_DIGITS):
            qfeat = jnp.where(frow == idx, part, qfeat)
        for hh in heads:
            qq_scr[hh, DIFF_VD:, :] = qfeat.astype(BF16)

    m_scr[...] = jnp.full_like(m_scr, NEG_BIG)
    acc_scr[...] = jnp.zeros_like(acc_scr)

    def scores(hh, j):
        k = k_ref[pl.ds(pl.multiple_of(j * blk, blk), blk), hh * DIFF_VD:(hh + 1) * DIFF_VD]
        return _dot(jnp.concatenate([k, kfeat_ref[hh]], axis=1), qq_scr[hh])

    def softmax_step(hh, j, slot, tail):
        slope = slope_ref[pl.program_id(0) * DA_HG + hh]
        offset = slope * LOG2E * (j * blk).astype(F32)
        t = s_scr[hh, slot]
        if tail:
            t = t + bias_ref[hh, jnp.where(j < i, 0, jnp.where(j == i, 1, 2))]
        m_old = m_scr[hh]
        m_new = jnp.maximum(m_old, jnp.max(t, axis=0, keepdims=True) + offset)
        m_scr[hh] = m_new
        p_scr[hh, slot] = jnp.exp2(t - (m_new - offset)).astype(BF16)
        a_scr[hh, slot] = jnp.exp2(m_old - m_new)

    def accumulate(hh, j, slot):
        vt = vt_ref[hh, :, pl.ds(pl.multiple_of(j * blk, blk), blk)]
        vt = jnp.concatenate([vt, jnp.ones((DA_ONES, blk), BF16)], axis=0)
        acc_scr[hh] = a_scr[hh, slot] * acc_scr[hh] + _dot(vt, p_scr[hh, slot])

    def stage(j, slot, tail):
        for hh in heads:
            s_scr[hh, 1 - slot] = scores(hh, jnp.minimum(j + 1, i))
            accumulate(hh, jnp.clip(j - 1, 0, i), 1 - slot)
            softmax_step(hh, j, slot, tail)

    for hh in heads:
        s_scr[hh, 0] = scores(hh, 0)
        p_scr[hh, 1] = jnp.zeros((blk, 2 * blk), BF16)
        a_scr[hh, 1] = jnp.ones((1, 2 * blk), F32)

    def run(first, trips, unroll, tail):
        def body(jj, c):
            for u in range(unroll):
                stage(first + unroll * jj + u, u % 2, tail)
            return c

        lax.fori_loop(0, trips, body, 0)
        return first + unroll * trips

    done = 0
    for unroll in DA_UNROLLS:
        done = run(done, (i - done) // unroll, unroll, False)
    run(done, (i - done + DA_TAIL) // DA_TAIL, DA_TAIL, True)

    lv = lam_ref[...]
    lam = (jnp.exp(jnp.sum(lv[0:1, :] * lv[1:2, :], axis=-1, keepdims=True))
           - jnp.exp(jnp.sum(lv[2:3, :] * lv[3:4, :], axis=-1, keepdims=True)) + lam_init)
    for hh in heads:
        accumulate(hh, i, 1)
        o = acc_scr[hh, 0:DIFF_VD, :] / acc_scr[hh, DIFF_VD:DIFF_VD + 1, :]
        od = o[:, 0:blk] - lam * o[:, blk:]
        od = od * lax.rsqrt(jnp.mean(od * od, axis=0, keepdims=True) + GN_EPS)
        o_ref[:, hh * DIFF_VD:(hh + 1) * DIFF_VD] = (od.T * g_ref[...] * (1.0 - lam_init)).astype(o_ref.dtype)


def _diffattn(z, vt, lam_vecs, subln_g, lam_init):
    n = z.shape[0]
    assert n % DA_B == 0 and DIFF_HEADS % DA_HG == 0
    w = DIFF_VD
    gw = DA_HG * w
    kfeat, bias = _diffattn_consts()
    vt = vt.reshape(DIFF_HEADS, w, n)
    return pl.pallas_call(
        functools.partial(_diffattn_kernel, lam_init=lam_init),
        grid_spec=pltpu.PrefetchScalarGridSpec(
            num_scalar_prefetch=1,
            grid=(DIFF_HEADS // DA_HG, n // DA_B),
            in_specs=[
                pl.BlockSpec((DA_B, gw), lambda h, i, s: (i, COL_DQ // gw + h)),
                pl.BlockSpec((n, gw), lambda h, i, s: (0, COL_DK // gw + h)),
                pl.BlockSpec((DA_HG, w, n), lambda h, i, s: (h, 0, 0)),
                pl.BlockSpec((DA_HG, DA_B, w), lambda h, i, s: (h, 0, 0)),
                pl.BlockSpec((DA_HG, 3, DA_B, 2 * DA_B), lambda h, i, s: (h, 0, 0, 0), pipeline_mode=pl.Buffered(1)),
                pl.BlockSpec((4, DIFF_HD), lambda h, i, s: (0, 0)),
                pl.BlockSpec((1, w), lambda h, i, s: (0, 0)),
            ],
            out_specs=pl.BlockSpec((DA_B, gw), lambda h, i, s: (i, h)),
            scratch_shapes=[pltpu.VMEM((DA_HG, 2 * w, 2 * DA_B), BF16), pltpu.VMEM((DA_HG, 1, 2 * DA_B), F32),
                            pltpu.VMEM((DA_HG, w + DA_ONES, 2 * DA_B), F32),
                            pltpu.VMEM((DA_HG, 2, DA_B, 2 * DA_B), F32), pltpu.VMEM((DA_HG, 2, DA_B, 2 * DA_B), BF16),
                            pltpu.VMEM((DA_HG, 2, 1, 2 * DA_B), F32)],
        ),
        out_shape=jax.ShapeDtypeStruct((n, DIFF_HEADS * w), BF16),
        compiler_params=_cparams(2),
        name="diffattn",
    )(jnp.asarray(_DA_SLOPES), z, z, vt, kfeat, bias, lam_vecs, subln_g)


def _merge_kernel(x_ref, ya_ref, yb_ref, yc_ref, yd_ref, g0_ref, g1_ref, g2_ref, g3_ref, wb_ref, wo_ref, o_ref):
    mixed = None
    for y_ref, g_ref, b in ((ya_ref, g0_ref, 0), (yb_ref, g1_ref, 1), (yc_ref, g2_ref, 2), (yd_ref, g3_ref, 3)):
        term = jax.nn.sigmoid(g_ref[...].astype(F32)) * _dot(y_ref[...], wb_ref[b])
        mixed = term if mixed is None else mixed + term
    o_ref[...] = x_ref[...] + _dot(mixed.astype(BF16), wo_ref[...])


def _merge(x, z, ya, yb, yc, yd, wb, wo, tm=512):
    n = x.shape[0]
    tm = min(tm, n)
    row = lambda w: pl.BlockSpec((tm, w), lambda i: (i, 0))
    gate = lambda b: pl.BlockSpec((tm, D_MODEL), lambda i: (i, COL_GATE // D_MODEL + b))
    return pl.pallas_call(
        _merge_kernel,
        grid=(n // tm,),
        in_specs=[row(D_MODEL), row(BRANCH_W), row(BRANCH_W), row(BRANCH_W), row(BRANCH_W),
                  gate(0), gate(1), gate(2), gate(3),
                  pl.BlockSpec((N_BRANCH, BRANCH_W, D_MODEL), lambda i: (0, 0, 0)),
                  pl.BlockSpec((D_MODEL, D_MODEL), lambda i: (0, 0))],
        out_specs=row(D_MODEL),
        out_shape=jax.ShapeDtypeStruct((n, D_MODEL), F32),
        compiler_params=_cparams(1),
        name="merge",
    )(x, ya, yb, yc, yd, z, z, z, z, wb, wo)


def _memkv_kernel(mem_ref, g_ref, wk_ref, wv_ref, k_ref, v_ref):
    mn = _rms(mem_ref[...], g_ref[...]).astype(BF16)
    k_ref[...] = _dot(mn, wk_ref[...]).astype(BF16)
    v_ref[...] = _dot(mn, wv_ref[...]).astype(BF16)


def _memkv(mem, g, wk, wv):
    m = mem.shape[0]
    w = MX_HEADS * MX_HD
    out = jax.ShapeDtypeStruct((m, w), BF16)
    return pl.pallas_call(_memkv_kernel, out_shape=(out, out), name="memkv",
                          compiler_params=pltpu.CompilerParams(vmem_limit_bytes=VMEM_LIMIT))(mem, g, wk, wv)


ROUTE_LANES = LANES


def _route(logits, carry):
    t = logits.shape[0]
    lane_i = lax.broadcasted_iota(jnp.int32, (t, ROUTE_LANES), 1)
    lane = lane_i.astype(F32)
    big = float(ROUTE_LANES)
    gl = jnp.where(lane_i < N_GROUPS, logits, NEG_BIG)
    gmax = jnp.max(gl, axis=-1, keepdims=True)
    gsum = jnp.sum(jnp.exp(gl - gmax), axis=-1, keepdims=True)
    g_sel = jnp.min(jnp.where(gl == gmax, lane, big), axis=-1, keepdims=True)
    g_prob = 1.0 / gsum
    lo = N_GROUPS + EXP_PER_GROUP * g_sel
    el = jnp.where((lane >= lo) & (lane < lo + EXP_PER_GROUP), logits, NEG_BIG)
    e1 = jnp.max(el, axis=-1, keepdims=True)
    i1 = jnp.min(jnp.where(el == e1, lane, big), axis=-1, keepdims=True)
    el2 = jnp.where(lane == i1, NEG_BIG, el)
    e2 = jnp.max(el2, axis=-1, keepdims=True)
    i2 = jnp.min(jnp.where(el2 == e2, lane, big), axis=-1, keepdims=True)
    esum = jnp.sum(jnp.exp(el - e1), axis=-1, keepdims=True)
    p1 = 1.0 / esum
    p2 = jnp.exp(e2 - e1) / esum
    w1 = p1 / (p1 + p2) * g_prob
    w2 = p2 / (p1 + p2) * g_prob
    hot1 = lane == i1
    hot2 = lane == i2
    cnt = jnp.where(hot1 | hot2, 1.0, 0.0)
    r = lax.broadcasted_iota(jnp.int32, (t, t), 0)
    c = lax.broadcasted_iota(jnp.int32, (t, t), 1)
    before = jnp.where(c < r, 1.0, 0.0).astype(BF16)
    prefix = _dot(before, cnt.astype(BF16)) + carry
    rank1 = jnp.sum(jnp.where(hot1, prefix, 0.0), axis=-1, keepdims=True)
    rank2 = jnp.sum(jnp.where(hot2, prefix, 0.0), axis=-1, keepdims=True)
    rec = jnp.zeros((t, ROUTE_LANES), F32)
    for idx, val in enumerate((i1 - N_GROUPS, i2 - N_GROUPS, w1, w2, rank1, rank2)):
        rec = jnp.where(lane_i == idx, val, rec)
    return rec, carry + jnp.sum(cnt, axis=0, keepdims=True)


POST_SUB = 256


def _post_kernel(x_ref, gm_ref, km_ref, vm_ref, wq_ref, wo_ref, gf_ref, wrh_ref, wrl_ref, br_ref,
                 xo_ref, t_ref, rec_ref, cnt_ref, carry):
    @pl.when(pl.program_id(0) == 0)
    def _():
        carry[...] = jnp.zeros_like(carry)

    km = km_ref[...]
    vm = vm_ref[...]
    logits = []
    for sub in range(x_ref.shape[0] // POST_SUB):
        rows = slice(sub * POST_SUB, (sub + 1) * POST_SUB)
        x = x_ref[rows, :]
        q = _dot(_rms(x, gm_ref[...]).astype(BF16), wq_ref[...]).astype(BF16)
        outs = []
        for h in range(MX_HEADS):
            sl = slice(h * MX_HD, (h + 1) * MX_HD)
            s = _dot_nt(q[:, sl], km[:, sl]) * (MX_HD ** -0.5)
            p = jnp.exp(s - jnp.max(s, axis=-1, keepdims=True))
            p = p / jnp.sum(p, axis=-1, keepdims=True)
            outs.append(_dot(p.astype(BF16), vm[:, sl]))
        x = x + _dot(jnp.concatenate(outs, axis=1).astype(BF16), wo_ref[...])
        xo_ref[rows, :] = x
        t = _rms(x, gf_ref[...])
        t_ref[rows, :] = t
        t_hi = t.astype(BF16)
        t_lo = (t - t_hi.astype(F32)).astype(BF16)
        logits.append(_dot(t_hi, wrh_ref[...]) + (_dot(t_lo, wrh_ref[...]) + _dot(t_hi, wrl_ref[...])) + br_ref[...])
    running = carry[0:1, :]
    for sub, lg in enumerate(logits):
        rec, running = _route(lg, running)
        rec_ref[sub * POST_SUB:(sub + 1) * POST_SUB, :] = rec
    carry[0:1, :] = running
    cnt_ref[...] = jnp.broadcast_to(running, cnt_ref.shape)


def _post(x, gm, km, vm, wq, wo, gf, wr, br, tm=512):
    n = x.shape[0]
    tm = min(tm, n)
    assert tm % POST_SUB == 0
    w = MX_HEADS * MX_HD
    m = km.shape[0]
    wr_hi = wr.astype(BF16)
    wr_lo = (wr - wr_hi.astype(F32)).astype(BF16)
    row = lambda wd: pl.BlockSpec((tm, wd), lambda i: (i, 0))
    full = lambda shape: pl.BlockSpec(shape, lambda i: (0,) * len(shape))
    return pl.pallas_call(
        _post_kernel,
        grid=(n // tm,),
        in_specs=[row(D_MODEL), full((1, D_MODEL)), full((m, w)), full((m, w)), full((D_MODEL, w)), full((w, D_MODEL)),
                  full((1, D_MODEL)), full((D_MODEL, ROUTE_LANES)), full((D_MODEL, ROUTE_LANES)),
                  full((1, ROUTE_LANES))],
        out_specs=[row(D_MODEL), row(D_MODEL), row(ROUTE_LANES), full((8, ROUTE_LANES))],
        out_shape=[jax.ShapeDtypeStruct((n, D_MODEL), F32), jax.ShapeDtypeStruct((n, D_MODEL), F32),
                   jax.ShapeDtypeStruct((n, ROUTE_LANES), F32), jax.ShapeDtypeStruct((8, ROUTE_LANES), F32)],
        scratch_shapes=[pltpu.VMEM((8, ROUTE_LANES), F32)],
        compiler_params=_cparams(1),
        name="post",
    )(x, gm, km, vm, wq, wo, gf, wr_hi, wr_lo, br)


MOE_TB = 256


def _row_copy(src, src_row, dst, dst_row, sem):
    return pltpu.make_async_copy(src.at[pl.ds(src_row, 1), :], dst.at[pl.ds(dst_row, 1), :], sem)


def _dispatch_kernel(dest_ref, t_ref, xs_in_ref, xs_ref, sem):
    del xs_in_ref
    base = pl.program_id(0) * MOE_TB

    def issue(r, c):
        for k in range(2):
            _row_copy(t_ref, r, xs_ref, dest_ref[2 * (base + r) + k], sem).start(priority=k)
        return c

    lax.fori_loop(0, MOE_TB, issue, 0, unroll=8)
    for k in range(2):
        pltpu.make_async_copy(t_ref, xs_ref.at[pl.ds(0, MOE_TB), :], sem).wait()


def _dispatch(dest, t, n_slots):
    n = t.shape[0]
    xs0 = jnp.zeros((n_slots, D_MODEL), F32)
    return pl.pallas_call(
        _dispatch_kernel,
        grid_spec=pltpu.PrefetchScalarGridSpec(
            num_scalar_prefetch=1,
            grid=(n // MOE_TB,),
            in_specs=[pl.BlockSpec((MOE_TB, D_MODEL), lambda i, d: (i, 0)), pl.BlockSpec(memory_space=pl.ANY)],
            out_specs=pl.BlockSpec(memory_space=pl.ANY),
            scratch_shapes=[pltpu.SemaphoreType.DMA(())],
        ),
        out_shape=jax.ShapeDtypeStruct((n_slots, D_MODEL), F32),
        input_output_aliases={2: 0},
        compiler_params=_cparams(1),
        name="dispatch",
    )(dest, t, xs0)


def _expert_kernel(be_ref, na_ref, xs_ref, w1_ref, w3_ref, w2_ref, ys_ref, w1b, w3b, w2b):
    i = pl.program_id(0)
    active = i < na_ref[0]
    new_expert = (i == 0) | (be_ref[i] != be_ref[jnp.maximum(i - 1, 0)])

    @pl.when(active & new_expert)
    def _():
        w1b[...] = w1_ref[...].astype(BF16)
        w3b[...] = w3_ref[...].astype(BF16)
        w2b[...] = w2_ref[...].astype(BF16)

    @pl.when(active)
    def _():
        x = xs_ref[...].astype(BF16)
        a = _dot(x, w1b[...])
        b = _dot(x, w3b[...])
        ys_ref[...] = _dot((jax.nn.silu(a) * b).astype(BF16), w2b[...])

    @pl.when(jnp.logical_not(active))
    def _():
        ys_ref[...] = jnp.zeros_like(ys_ref)


def _experts(block_e, n_active, xs, w1, w3, w2, layer):
    n_blocks = xs.shape[0] // MOE_BLOCK
    blk = lambda i, na: jnp.minimum(i, jnp.maximum(na[0] - 1, 0))
    wspec = lambda rows, cols: pl.BlockSpec((None, None, rows, cols), lambda i, be, na: (layer, be[blk(i, na)], 0, 0))
    return pl.pallas_call(
        _expert_kernel,
        grid_spec=pltpu.PrefetchScalarGridSpec(
            num_scalar_prefetch=2,
            grid=(n_blocks,),
            in_specs=[
                pl.BlockSpec((MOE_BLOCK, D_MODEL), lambda i, be, na: (blk(i, na), 0)),
                wspec(D_MODEL, D_EXPERT), wspec(D_MODEL, D_EXPERT), wspec(D_EXPERT, D_MODEL),
            ],
            out_specs=pl.BlockSpec((MOE_BLOCK, D_MODEL), lambda i, be, na: (i, 0)),
            scratch_shapes=[pltpu.VMEM((D_MODEL, D_EXPERT), BF16), pltpu.VMEM((D_MODEL, D_EXPERT), BF16),
                            pltpu.VMEM((D_EXPERT, D_MODEL), BF16)],
        ),
        out_shape=jax.ShapeDtypeStruct(xs.shape, F32),
        compiler_params=_cparams(1),
        name="experts",
    )(block_e, n_active, xs, w1, w3, w2)


def _combine_kernel(dest_ref, x_ref, rec_ref, g_ref, ys_ref, o_ref, rows, sem, *, final):
    base = pl.program_id(0) * MOE_TB

    def issue(r, c):
        for k in range(2):
            _row_copy(ys_ref, dest_ref[2 * (base + r) + k], rows.at[k], r, sem).start(priority=k)
        return c

    lax.fori_loop(0, MOE_TB, issue, 0, unroll=8)
    for k in range(2):
        pltpu.make_async_copy(ys_ref.at[pl.ds(0, MOE_TB), :], rows.at[k], sem).wait()
    rec = rec_ref[...]
    x = x_ref[...] + rec[:, 2:3] * rows[0] + rec[:, 3:4] * rows[1]
    o_ref[...] = _rms(x, g_ref[...]) if final else x


def _combine(dest, x, rec, ys, g, final):
    n = x.shape[0]
    row = lambda wd: pl.BlockSpec((MOE_TB, wd), lambda i, d: (i, 0))
    return pl.pallas_call(
        functools.partial(_combine_kernel, final=final),
        grid_spec=pltpu.PrefetchScalarGridSpec(
            num_scalar_prefetch=1,
            grid=(n // MOE_TB,),
            in_specs=[row(D_MODEL), row(ROUTE_LANES), pl.BlockSpec((1, D_MODEL), lambda i, d: (0, 0)),
                      pl.BlockSpec(memory_space=pl.ANY)],
            out_specs=row(D_MODEL),
            scratch_shapes=[pltpu.VMEM((2, MOE_TB, D_MODEL), F32), pltpu.SemaphoreType.DMA(())],
        ),
        out_shape=jax.ShapeDtypeStruct((n, D_MODEL), F32),
        compiler_params=_cparams(1),
        name="combine",
    )(dest, x, rec, g, ys)


def _moe(x, t, rec, cnt, w1, w3, w2, layer, g_final, final):
    n = x.shape[0]
    n_asg = 2 * n
    n_blocks = -(-(n_asg + N_EXPERTS * (MOE_BLOCK - 1)) // MOE_BLOCK)
    counts = cnt[0, N_GROUPS:N_GROUPS + N_EXPERTS].astype(jnp.int32)
    padded = (counts + MOE_BLOCK - 1) // MOE_BLOCK * MOE_BLOCK
    pad_end = jnp.cumsum(padded)
    pad_start = pad_end - padded
    expert = rec[:, 0:2].astype(jnp.int32)
    first = jnp.sum(jnp.where(expert[..., None] == jnp.arange(N_EXPERTS), pad_start, 0), axis=-1)
    dest = (first + rec[:, 4:6].astype(jnp.int32)).reshape(-1)
    starts = jnp.arange(n_blocks, dtype=jnp.int32) * MOE_BLOCK
    block_e = jnp.minimum(jnp.sum(pad_end[None, :] <= starts[:, None], axis=1), N_EXPERTS - 1).astype(jnp.int32)
    n_active = (pad_end[-1:] // MOE_BLOCK).astype(jnp.int32)
    xs = _dispatch(dest, t, n_blocks * MOE_BLOCK)
    ys = _experts(block_e, n_active, xs, w1, w3, w2, layer)
    return _combine(dest, x, rec, ys, g_final, final)


def _block_diag(w):
    eye = jnp.eye(LRU_BLOCKS, dtype=w.dtype)
    return jnp.einsum("kcd,kj->kcjd", w, eye).reshape(LRU_W, LRU_W)


def _router_weights(w_group, b_group, w_router, b_router):
    pad = ROUTE_LANES - N_GROUPS - N_EXPERTS
    wr = jnp.concatenate([w_group, w_router, jnp.zeros((D_MODEL, pad), F32)], axis=1)
    br = jnp.concatenate([b_group, b_router, jnp.zeros((pad,), F32)])[None, :]
    return wr, br


def kernel(x, mem, norm_mix, w_in, conv_w, conv_b, lru_wa, lru_ba, lru_wx, lru_bx, lru_lambda, ca_rel_bias, diff_lambda, diff_subln, w_branch, w_out, norm_mem, mem_norm, w_mq, w_mk, w_mv, w_mo, norm_ffn, w_group, b_group, w_router, b_router, w1, w3, w2, final_norm):
    b, n, d = x.shape
    assert b == 1 and d == D_MODEL
    xs = x[0]
    for l in range(DEPTH):
        lam_init = 0.8 - 0.6 * math.exp(-0.3 * l)
        z, vt_b, vt_d = _inproj(xs, norm_mix[l][None, :], w_in[l].astype(BF16))
        wcat = jnp.concatenate([_block_diag(lru_wa[l]), _block_diag(lru_wx[l])], axis=1).astype(BF16)
        bcat = jnp.concatenate([lru_ba[l], lru_bx[l]])[None, :]
        ya = _lru(z, conv_w[l], conv_b[l][None, :], wcat, bcat, lru_lambda[l][None, :])
        yb = _chunkattn(z, vt_b, _chunk_bias(ca_rel_bias[l]))
        yc = _retention(z)
        yd = _diffattn(z, vt_d, diff_lambda[l], diff_subln[l][None, :], lam_init)
        x1 = _merge(xs, z, ya, yb, yc, yd, w_branch[l].astype(BF16), w_out[l].astype(BF16))
        km, vm = _memkv(mem[0], mem_norm[None, :], w_mk[l].astype(BF16), w_mv[l].astype(BF16))
        wr, br = _router_weights(w_group[l], b_group[l], w_router[l], b_router[l])
        x2, t, rec, cnt = _post(x1, norm_mem[l][None, :], km, vm, w_mq[l].astype(BF16), w_mo[l].astype(BF16),
                                norm_ffn[l][None, :], wr, br)
        xs = _moe(x2, t, rec, cnt, w1, w3, w2, l, final_norm[None, :], final=(l == DEPTH - 1))
    return xs[None]
```

```python
import functools
import math

import numpy as np
import jax
import jax.numpy as jnp
from jax import lax
from jax.experimental import pallas as pl
from jax.experimental.pallas import tpu as pltpu

F32 = jnp.float32
BF16 = jnp.bfloat16

D_MODEL = 1024
DEPTH = 2
CHUNK = 64
NORM_EPS = 1e-6
GN_EPS = 1e-5
LRU_W = 512
LRU_BLOCKS = 8
LRU_BW = LRU_W // LRU_BLOCKS
CONV_W = 4
LRU_C = 8.0
CA_HEADS = 8
CA_HD = 64
CA_PREV = 8
REL_CLIP = 128
RET_HEADS = 8
RET_HD = 64
DIFF_HEADS = 4
DIFF_HD = 64
DIFF_VD = 2 * DIFF_HD
MX_HEADS = 4
MX_HD = 128
N_GROUPS = 4
EXP_PER_GROUP = 8
N_EXPERTS = N_GROUPS * EXP_PER_GROUP
D_EXPERT = 512
MOE_BLOCK = 256
N_BRANCH = 4
BRANCH_W = 512

COL_AX, COL_AG = 0, 512
COL_BQ, COL_BK, COL_BV = 1024, 1536, 2048
COL_CQ, COL_CK, COL_CV, COL_CG = 2560, 3072, 3584, 4096
COL_DQ, COL_DK, COL_DV = 4608, 5120, 5632
COL_GATE = 6144
IN_COLS = 10240

LANES = 128
NEG_BIG = -1e30
LOG2E = 1.0 / math.log(2.0)
VMEM_LIMIT = 56 * 1024 * 1024


def _cparams(n_axes):
    return pltpu.CompilerParams(dimension_semantics=("arbitrary",) * n_axes, vmem_limit_bytes=VMEM_LIMIT)


def _rms(x, g):
    return x * lax.rsqrt(jnp.mean(x * x, axis=-1, keepdims=True) + NORM_EPS) * g


def _dot(a, b):
    return jnp.dot(a, b, preferred_element_type=F32)


def _dot_nt(a, b):
    return lax.dot_general(a, b, (((1,), (1,)), ((), ())), preferred_element_type=F32)


def _dot_tn(a, b):
    return lax.dot_general(a, b, (((0,), (0,)), ((), ())), preferred_element_type=F32)


PACK_W = D_MODEL // 2


def _pack_rows(x):
    as_bits = lambda v: lax.bitcast_convert_type(v.astype(BF16).astype(F32), jnp.uint32)
    return (as_bits(x[:, :PACK_W]) >> 16) | (as_bits(x[:, PACK_W:]) & jnp.uint32(0xFFFF0000))


def _unpack_rows(w):
    lo = lax.bitcast_convert_type(w << 16, F32)
    hi = lax.bitcast_convert_type(w & jnp.uint32(0xFFFF0000), F32)
    return jnp.concatenate([lo, hi], axis=1)


def _inproj_kernel(x_ref, g_ref, w_ref, o_ref, vtb_ref, vtd_ref, h_scr):
    j = pl.program_id(1)

    @pl.when(j == 0)
    def _():
        h_scr[...] = _rms(x_ref[...], g_ref[...]).astype(BF16)

    tn = o_ref.shape[1]
    w = w_ref[:, pl.ds(pl.multiple_of(j * tn, tn), tn)]
    acc = _dot(h_scr[...], w)
    o_ref[...] = acc.astype(o_ref.dtype)
    for col, vt_ref in ((COL_BV, vtb_ref), (COL_DV, vtd_ref)):
        off = col % tn

        @pl.when(j == col // tn)
        def _():
            vt_ref[...] = acc[:, off:off + vt_ref.shape[0]].T.astype(vt_ref.dtype)


def _inproj(x, g, w, tm=512, tn=2560):
    n = x.shape[0]
    tm = min(tm, n)
    vw = CA_HEADS * CA_HD
    assert vw == DIFF_HEADS * DIFF_VD and all(c % tn + vw <= tn for c in (COL_BV, COL_DV))
    vt_spec = pl.BlockSpec((vw, tm), lambda i, j: (0, i))
    vt_shape = jax.ShapeDtypeStruct((vw, n), BF16)
    return pl.pallas_call(
        _inproj_kernel,
        grid=(n // tm, IN_COLS // tn),
        in_specs=[
            pl.BlockSpec((tm, D_MODEL), lambda i, j: (i, 0)),
            pl.BlockSpec((1, D_MODEL), lambda i, j: (0, 0)),
            pl.BlockSpec((D_MODEL, IN_COLS), lambda i, j: (0, 0), pipeline_mode=pl.Buffered(1)),
        ],
        out_specs=[pl.BlockSpec((tm, tn), lambda i, j: (i, j)), vt_spec, vt_spec],
        out_shape=[jax.ShapeDtypeStruct((n, IN_COLS), BF16), vt_shape, vt_shape],
        scratch_shapes=[pltpu.VMEM((tm, D_MODEL), BF16)],
        compiler_params=_cparams(2),
        name="inproj",
    )(x, g, w)


def _gelu_tanh(x):
    return 0.5 * x * (1.0 + jnp.tanh(math.sqrt(2.0 / math.pi) * (x + 0.044715 * x * x * x)))


def _lru_kernel(ax_ref, ag_ref, cw_ref, cb_ref, w_ref, b_ref, lam_ref, o_ref, xbuf, a_scr, u_scr, h_scr, hc):
    t = ax_ref.shape[0]

    @pl.when(pl.program_id(0) == 0)
    def _():
        xbuf[0:8, :] = jnp.zeros((8, LRU_W), F32)
        hc[...] = jnp.zeros_like(hc)

    x = ax_ref[...].astype(F32)
    xbuf[8:8 + t, :] = x
    xc = (cb_ref[...] + cw_ref[3:4, :] * x + cw_ref[2:3, :] * xbuf[7:7 + t, :]
          + cw_ref[1:2, :] * xbuf[6:6 + t, :] + cw_ref[0:1, :] * xbuf[5:5 + t, :])
    xbuf[0:8, :] = xbuf[t:t + 8, :]
    ri = _dot(xc.astype(BF16), w_ref[...]) + b_ref[...]
    r = jax.nn.sigmoid(ri[:, :LRU_W])
    ig = jax.nn.sigmoid(ri[:, LRU_W:])
    nl = -lam_ref[...]
    softplus = jnp.maximum(nl, 0.0) + jnp.log(1.0 + jnp.exp(-jnp.abs(nl)))
    log_a = -LRU_C * r * softplus
    a = jnp.exp(log_a)
    a_scr[...] = a
    u_scr[...] = jnp.sqrt(-jnp.tanh(log_a) * (a * a + 1.0)) * (ig * xc)

    def body(s, h):
        h = a_scr[pl.ds(s, 1), :] * h + u_scr[pl.ds(s, 1), :]
        h_scr[pl.ds(s, 1), :] = h
        return h

    hc[0:1, :] = lax.fori_loop(0, t, body, hc[0:1, :], unroll=8)
    o_ref[...] = (h_scr[...] * _gelu_tanh(ag_ref[...].astype(F32))).astype(o_ref.dtype)


def _lru(z, cw, cb, wcat, bcat, lam, t=256):
    n = z.shape[0]
    t = min(t, n)
    full = lambda shape: pl.BlockSpec(shape, lambda i: (0,) * len(shape))
    return pl.pallas_call(
        _lru_kernel,
        grid=(n // t,),
        in_specs=[
            pl.BlockSpec((t, LRU_W), lambda i: (i, COL_AX // LRU_W)),
            pl.BlockSpec((t, LRU_W), lambda i: (i, COL_AG // LRU_W)),
            full((CONV_W, LRU_W)), full((1, LRU_W)), full((LRU_W, 2 * LRU_W)), full((1, 2 * LRU_W)), full((1, LRU_W)),
        ],
        out_specs=pl.BlockSpec((t, LRU_W), lambda i: (i, 0)),
        out_shape=jax.ShapeDtypeStruct((n, LRU_W), BF16),
        scratch_shapes=[pltpu.VMEM((t + 8, LRU_W), F32), pltpu.VMEM((t, LRU_W), F32), pltpu.VMEM((t, LRU_W), F32),
                        pltpu.VMEM((t, LRU_W), F32), pltpu.VMEM((8, LRU_W), F32)],
        compiler_params=_cparams(1),
        name="lru",
    )(z, z, cw, cb, wcat, bcat, lam)


CA_QB = 256
CA_KW = CA_PREV * CHUNK + CA_QB


CA_ONES = 16
CA_PAIR = 2 * CA_HD


def _chunkattn_kernel(q_ref, k0_ref, k1_ref, k2_ref, vt0_ref, vt1_ref, vt2_ref, bias_ref, o_ref):
    qt = (q_ref[...].astype(F32) * (LOG2E * CA_HD ** -0.5)).T
    k = jnp.concatenate([k0_ref[...], k1_ref[...], k2_ref[...]], axis=0)
    feat = lax.broadcasted_iota(jnp.int32, (CA_PAIR, CA_QB), 0)
    for hp in range(CA_HEADS // 2):
        lanes = slice(hp * CA_PAIR, (hp + 1) * CA_PAIR)
        qtp = qt[lanes, :]
        qq = jnp.concatenate([jnp.where(feat < CA_HD, qtp, 0.0), jnp.where(feat >= CA_HD, qtp, 0.0)], axis=1)
        t = _dot(k[:, lanes], qq.astype(BF16))
        t = t + jnp.concatenate([bias_ref[0, 2 * hp], bias_ref[0, 2 * hp + 1]], axis=1)
        p = jnp.exp2(t - jnp.max(t, axis=0, keepdims=True)).astype(BF16)
        vt = jnp.concatenate([vt0_ref[hp], vt1_ref[hp], vt2_ref[hp]], axis=1)
        vt = jnp.concatenate([vt, jnp.ones((CA_ONES, CA_KW), BF16)], axis=0)
        acc = _dot(vt, p)
        o1 = acc[0:CA_HD, 0:CA_QB] / acc[CA_PAIR:CA_PAIR + 1, 0:CA_QB]
        o2 = acc[CA_HD:CA_PAIR, CA_QB:] / acc[CA_PAIR:CA_PAIR + 1, CA_QB:]
        o_ref[:, lanes] = jnp.concatenate([o1, o2], axis=0).T.astype(o_ref.dtype)


def _chunk_bias(rel_table):
    span = CA_QB + CA_KW - 1
    n_hi = CA_KW - 1 - REL_CLIP
    n_lo = CA_QB - 1 - REL_CLIP
    e = jnp.concatenate([jnp.broadcast_to(rel_table[:, -1:], (CA_HEADS, n_hi)), rel_table[:, ::-1],
                         jnp.broadcast_to(rel_table[:, :1], (CA_HEADS, n_lo + 1))], axis=1)
    w = jnp.tile(e, (1, CA_QB))[:, :CA_QB * span].reshape(CA_HEADS, CA_QB, span)
    toeplitz = w[:, :, CA_QB - 1:CA_QB - 1 + CA_KW]
    cq = np.arange(CA_QB)[:, None] // CHUNK
    ck = np.arange(CA_KW)[None, :] // CHUNK
    band = (ck >= cq) & (ck <= cq + CA_PREV)
    first_valid = np.maximum((CA_KW - CA_QB) - CA_QB * np.arange(3), 0)
    ok = band[None] & (np.arange(CA_KW)[None, None, :] >= first_valid[:, None, None])
    bias = jnp.where(ok[:, None], (LOG2E * toeplitz.astype(F32))[None], NEG_BIG)
    return bias.transpose(0, 1, 3, 2)


def _chunkattn(z, vt, bias):
    n = z.shape[0]
    w = CA_HEADS * CA_HD
    nb = n // CA_QB
    n_pair = CA_HEADS // 2
    vt = vt.reshape(n_pair, CA_PAIR, n)
    kspec = lambda back: pl.BlockSpec((CA_QB, w), lambda i: (jnp.maximum(i - back, 0), COL_BK // w))
    vspec = lambda back: pl.BlockSpec((n_pair, CA_PAIR, CA_QB), lambda i: (0, 0, jnp.maximum(i - back, 0)))
    return pl.pallas_call(
        _chunkattn_kernel,
        grid=(nb,),
        in_specs=[
            pl.BlockSpec((CA_QB, w), lambda i: (i, COL_BQ // w)),
            kspec(2), kspec(1), kspec(0), vspec(2), vspec(1), vspec(0),
            pl.BlockSpec((1, CA_HEADS, CA_KW, CA_QB), lambda i: (jnp.minimum(i, 2), 0, 0, 0),
                         pipeline_mode=pl.Buffered(1)),
        ],
        out_specs=pl.BlockSpec((CA_QB, w), lambda i: (i, 0)),
        out_shape=jax.ShapeDtypeStruct((n, w), BF16),
        compiler_params=_cparams(1),
        name="chunkattn",
    )(z, z, z, z, vt, vt, vt, bias)


RET_T = 256
_RET_LOG_G = np.log(1.0 - 2.0 ** (-5.0 - np.arange(RET_HEADS)))


def _retention_consts(t):
    pos = np.arange(t)
    diff = pos[:, None] - pos[None, :]
    dmat = np.where(diff[None] >= 0, np.exp(_RET_LOG_G[:, None, None] * np.maximum(diff, 0)[None]), 0.0)
    dmat = dmat * (RET_HD ** -0.5)
    zeta = np.exp(_RET_LOG_G[None, :] * (t - 1 - pos)[:, None]) * (RET_HD ** -0.5)
    xi = np.exp(_RET_LOG_G[None, :] * (pos + 1)[:, None])
    rep = lambda m: np.repeat(m, RET_HD, axis=1)
    return (jnp.asarray(dmat, F32), jnp.asarray(rep(zeta), F32), jnp.asarray(rep(xi), F32))


def _retention_kernel(q_ref, k_ref, v_ref, g_ref, dmat_ref, zeta_ref, xi_ref, o_ref, s_scr):
    t = q_ref.shape[0]

    @pl.when(pl.program_id(0) == 0)
    def _():
        s_scr[...] = jnp.zeros_like(s_scr)

    q = q_ref[...]
    k = k_ref[...]
    v = v_ref[...]
    kz = (k.astype(F32) * zeta_ref[...]).astype(BF16)
    xi = xi_ref[...]
    g = g_ref[...].astype(F32)
    outs = []
    for h in range(RET_HEADS):
        sl = slice(h * RET_HD, (h + 1) * RET_HD)
        qh, vh = q[:, sl], v[:, sl]
        inner = _dot_nt(qh, k[:, sl]) * dmat_ref[h]
        state = s_scr[h]
        o = _dot(inner.astype(BF16), vh) + _dot(qh, state.astype(BF16)) * xi[:, sl]
        s_scr[h] = float(np.exp(_RET_LOG_G[h] * t)) * state + _dot_tn(kz[:, sl], vh)
        mu = jnp.mean(o, axis=-1, keepdims=True)
        d = o - mu
        var = jnp.mean(d * d, axis=-1, keepdims=True)
        outs.append(d * lax.rsqrt(var + GN_EPS))
    o_ref[...] = (jax.nn.silu(g) * jnp.concatenate(outs, axis=1)).astype(o_ref.dtype)


def _retention(z):
    n = z.shape[0]
    t = min(RET_T, n)
    w = RET_HEADS * RET_HD
    dmat, zeta, xi = _retention_consts(t)
    col = lambda c: pl.BlockSpec((t, w), lambda i: (i, c // w))
    return pl.pallas_call(
        _retention_kernel,
        grid=(n // t,),
        in_specs=[col(COL_CQ), col(COL_CK), col(COL_CV), col(COL_CG),
                  pl.BlockSpec((RET_HEADS, t, t), lambda i: (0, 0, 0)),
                  pl.BlockSpec((t, w), lambda i: (0, 0)), pl.BlockSpec((t, w), lambda i: (0, 0))],
        out_specs=pl.BlockSpec((t, w), lambda i: (i, 0)),
        out_shape=jax.ShapeDtypeStruct((n, w), BF16),
        scratch_shapes=[pltpu.VMEM((RET_HEADS, RET_HD, RET_HD), F32)],
        compiler_params=_cparams(1),
        name="retention",
    )(z, z, z, z, dmat, zeta, xi)


DA_B = 256
DA_UNROLLS = (8,)
DA_TAIL = 2
DA_HG = 2
DA_ONES = 16
_DA_SLOPES = (2.0 ** (-8.0 * np.arange(1, DIFF_HEADS + 1) / DIFF_HEADS)).astype(np.float32)


def _bf16_parts(x, n):
    parts = []
    for _ in range(n):
        p = float(np.asarray(x, np.float32).astype(jnp.bfloat16).astype(np.float32))
        parts.append(p)
        x = x - p
    return parts


_LOG2E_PARTS = _bf16_parts(LOG2E, 3)
_DA_DIGITS = -(-(DA_B - 1).bit_length() // 8)


def _diffattn_consts():
    kloc = np.arange(DA_B)[:, None]
    qloc = (np.arange(2 * DA_B) % DA_B)[None, :]
    slopes = _DA_SLOPES[:, None, None].astype(np.float64)
    kfeat = np.zeros((DIFF_HEADS, DA_B, DIFF_VD))
    for d in range(_DA_DIGITS):
        cols = slice(d * len(_LOG2E_PARTS), (d + 1) * len(_LOG2E_PARTS))
        kfeat[:, :, cols] = slopes * (((kloc >> (8 * d)) & 255) << (8 * d))
    diag = np.where((kloc // CHUNK) <= (qloc // CHUNK), LOG2E * slopes * (qloc - np.abs(qloc - kloc) - kloc), NEG_BIG)
    bias = np.stack([np.zeros_like(diag), diag, np.full_like(diag, NEG_BIG)], axis=1)
    return jnp.asarray(kfeat, BF16), jnp.asarray(bias, F32)


def _diffattn_kernel(slope_ref, q_ref, k_ref, vt_ref, kfeat_ref, bias_ref, lam_ref, g_ref, o_ref,
                     qq_scr, m_scr, acc_scr, s_scr, p_scr, a_scr, *, lam_init):
    i = pl.program_id(1)
    blk = DA_B
    heads = range(DA_HG)
    feat = lax.broadcasted_iota(jnp.int32, (DIFF_VD, blk), 0)
    for hh in heads:
        qt = (q_ref[:, hh * DIFF_VD:(hh + 1) * DIFF_VD].astype(F32) * (LOG2E * DIFF_HD ** -0.5)).T
        qq_scr[hh, 0:DIFF_VD, 0:blk] = jnp.where(feat < DIFF_HD, qt, 0.0).astype(BF16)
        qq_scr[hh, 0:DIFF_VD, blk:] = jnp.where(feat >= DIFF_HD, qt, 0.0).astype(BF16)

    @pl.when(i == 0)
    def _():
        frow = lax.broadcasted_iota(jnp.int32, (DIFF_VD, 2 * blk), 0)
        qfeat = jnp.zeros((DIFF_VD, 2 * blk), F32)
        for idx, part in enumerate(_LOG2E_PARTS * _DA_DIGITS):
            qfeat = jnp.where(frow == idx, part, qfeat)
        for hh in heads:
            qq_scr[hh, DIFF_VD:, :] = qfeat.astype(BF16)

    m_scr[...] = jnp.full_like(m_scr, NEG_BIG)
    acc_scr[...] = jnp.zeros_like(acc_scr)

    def scores(hh, j):
        k = k_ref[pl.ds(pl.multiple_of(j * blk, blk), blk), hh * DIFF_VD:(hh + 1) * DIFF_VD]
        return _dot(jnp.concatenate([k, kfeat_ref[hh]], axis=1), qq_scr[hh])

    def softmax_step(hh, j, slot, tail):
        slope = slope_ref[pl.program_id(0) * DA_HG + hh]
        offset = slope * LOG2E * (j * blk).astype(F32)
        t = s_scr[hh, slot]
        if tail:
            t = t + bias_ref[hh, jnp.where(j < i, 0, jnp.where(j == i, 1, 2))]
        m_old = m_scr[hh]
        m_new = jnp.maximum(m_old, jnp.max(t, axis=0, keepdims=True) + offset)
        m_scr[hh] = m_new
        p_scr[hh, slot] = jnp.exp2(t - (m_new - offset)).astype(BF16)
        a_scr[hh, slot] = jnp.exp2(m_old - m_new)

    def accumulate(hh, j, slot):
        vt = vt_ref[hh, :, pl.ds(pl.multiple_of(j * blk, blk), blk)]
        vt = jnp.concatenate([vt, jnp.ones((DA_ONES, blk), BF16)], axis=0)
        acc_scr[hh] = a_scr[hh, slot] * acc_scr[hh] + _dot(vt, p_scr[hh, slot])

    def stage(j, slot, tail):
        for hh in heads:
            s_scr[hh, 1 - slot] = scores(hh, jnp.minimum(j + 1, i))
            accumulate(hh, jnp.clip(j - 1, 0, i), 1 - slot)
            softmax_step(hh, j, slot, tail)

    for hh in heads:
        s_scr[hh, 0] = scores(hh, 0)
        p_scr[hh, 1] = jnp.zeros((blk, 2 * blk), BF16)
        a_scr[hh, 1] = jnp.ones((1, 2 * blk), F32)

    def run(first, trips, unroll, tail):
        def body(jj, c):
            for u in range(unroll):
                stage(first + unroll * jj + u, u % 2, tail)
            return c

        lax.fori_loop(0, trips, body, 0)
        return first + unroll * trips

    done = 0
    for unroll in DA_UNROLLS:
        done = run(done, (i - done) // unroll, unroll, False)
    run(done, (i - done + DA_TAIL) // DA_TAIL, DA_TAIL, True)

    lv = lam_ref[...]
    lam = (jnp.exp(jnp.sum(lv[0:1, :] * lv[1:2, :], axis=-1, keepdims=True))
           - jnp.exp(jnp.sum(lv[2:3, :] * lv[3:4, :], axis=-1, keepdims=True)) + lam_init)
    for hh in heads:
        accumulate(hh, i, 1)
        o = acc_scr[hh, 0:DIFF_VD, :] / acc_scr[hh, DIFF_VD:DIFF_VD + 1, :]
        od = o[:, 0:blk] - lam * o[:, blk:]
        od = od * lax.rsqrt(jnp.mean(od * od, axis=0, keepdims=True) + GN_EPS)
        o_ref[:, hh * DIFF_VD:(hh + 1) * DIFF_VD] = (od.T * g_ref[...] * (1.0 - lam_init)).astype(o_ref.dtype)


def _diffattn(z, vt, lam_vecs, subln_g, lam_init):
    n = z.shape[0]
    assert n % DA_B == 0 and DIFF_HEADS % DA_HG == 0
    w = DIFF_VD
    gw = DA_HG * w
    kfeat, bias = _diffattn_consts()
    vt = vt.reshape(DIFF_HEADS, w, n)
    return pl.pallas_call(
        functools.partial(_diffattn_kernel, lam_init=lam_init),
        grid_spec=pltpu.PrefetchScalarGridSpec(
            num_scalar_prefetch=1,
            grid=(DIFF_HEADS // DA_HG, n // DA_B),
            in_specs=[
                pl.BlockSpec((DA_B, gw), lambda h, i, s: (i, COL_DQ // gw + h)),
                pl.BlockSpec((n, gw), lambda h, i, s: (0, COL_DK // gw + h)),
                pl.BlockSpec((DA_HG, w, n), lambda h, i, s: (h, 0, 0)),
                pl.BlockSpec((DA_HG, DA_B, w), lambda h, i, s: (h, 0, 0)),
                pl.BlockSpec((DA_HG, 3, DA_B, 2 * DA_B), lambda h, i, s: (h, 0, 0, 0), pipeline_mode=pl.Buffered(1)),
                pl.BlockSpec((4, DIFF_HD), lambda h, i, s: (0, 0)),
                pl.BlockSpec((1, w), lambda h, i, s: (0, 0)),
            ],
            out_specs=pl.BlockSpec((DA_B, gw), lambda h, i, s: (i, h)),
            scratch_shapes=[pltpu.VMEM((DA_HG, 2 * w, 2 * DA_B), BF16), pltpu.VMEM((DA_HG, 1, 2 * DA_B), F32),
                            pltpu.VMEM((DA_HG, w + DA_ONES, 2 * DA_B), F32),
                            pltpu.VMEM((DA_HG, 2, DA_B, 2 * DA_B), F32), pltpu.VMEM((DA_HG, 2, DA_B, 2 * DA_B), BF16),
                            pltpu.VMEM((DA_HG, 2, 1, 2 * DA_B), F32)],
        ),
        out_shape=jax.ShapeDtypeStruct((n, DIFF_HEADS * w), BF16),
        compiler_params=_cparams(2),
        name="diffattn",
    )(jnp.asarray(_DA_SLOPES), z, z, vt, kfeat, bias, lam_vecs, subln_g)


def _merge_kernel(x_ref, ya_ref, yb_ref, yc_ref, yd_ref, g0_ref, g1_ref, g2_ref, g3_ref, wb_ref, wo_ref, o_ref):
    mixed = None
    for y_ref, g_ref, b in ((ya_ref, g0_ref, 0), (yb_ref, g1_ref, 1), (yc_ref, g2_ref, 2), (yd_ref, g3_ref, 3)):
        term = jax.nn.sigmoid(g_ref[...].astype(F32)) * _dot(y_ref[...], wb_ref[b])
        mixed = term if mixed is None else mixed + term
    o_ref[...] = x_ref[...] + _dot(mixed.astype(BF16), wo_ref[...])


def _merge(x, z, ya, yb, yc, yd, wb, wo, tm=512):
    n = x.shape[0]
    tm = min(tm, n)
    row = lambda w: pl.BlockSpec((tm, w), lambda i: (i, 0))
    gate = lambda b: pl.BlockSpec((tm, D_MODEL), lambda i: (i, COL_GATE // D_MODEL + b))
    return pl.pallas_call(
        _merge_kernel,
        grid=(n // tm,),
        in_specs=[row(D_MODEL), row(BRANCH_W), row(BRANCH_W), row(BRANCH_W), row(BRANCH_W),
                  gate(0), gate(1), gate(2), gate(3),
                  pl.BlockSpec((N_BRANCH, BRANCH_W, D_MODEL), lambda i: (0, 0, 0)),
                  pl.BlockSpec((D_MODEL, D_MODEL), lambda i: (0, 0))],
        out_specs=row(D_MODEL),
        out_shape=jax.ShapeDtypeStruct((n, D_MODEL), F32),
        compiler_params=_cparams(1),
        name="merge",
    )(x, ya, yb, yc, yd, z, z, z, z, wb, wo)


def _memkv_kernel(mem_ref, g_ref, wk_ref, wv_ref, k_ref, v_ref):
    mn = _rms(mem_ref[...], g_ref[...]).astype(BF16)
    k_ref[...] = _dot(mn, wk_ref[...]).astype(BF16)
    v_ref[...] = _dot(mn, wv_ref[...]).astype(BF16)


def _memkv(mem, g, wk, wv):
    m = mem.shape[0]
    w = MX_HEADS * MX_HD
    out = jax.ShapeDtypeStruct((m, w), BF16)
    return pl.pallas_call(_memkv_kernel, out_shape=(out, out), name="memkv",
                          compiler_params=pltpu.CompilerParams(vmem_limit_bytes=VMEM_LIMIT))(mem, g, wk, wv)


ROUTE_LANES = LANES


def _route(logits, carry):
    t = logits.shape[0]
    lane_i = lax.broadcasted_iota(jnp.int32, (t, ROUTE_LANES), 1)
    lane = lane_i.astype(F32)
    big = float(ROUTE_LANES)
    gl = jnp.where(lane_i < N_GROUPS, logits, NEG_BIG)
    gmax = jnp.max(gl, axis=-1, keepdims=True)
    gsum = jnp.sum(jnp.exp(gl - gmax), axis=-1, keepdims=True)
    g_sel = jnp.min(jnp.where(gl == gmax, lane, big), axis=-1, keepdims=True)
    g_prob = 1.0 / gsum
    lo = N_GROUPS + EXP_PER_GROUP * g_sel
    el = jnp.where((lane >= lo) & (lane < lo + EXP_PER_GROUP), logits, NEG_BIG)
    e1 = jnp.max(el, axis=-1, keepdims=True)
    i1 = jnp.min(jnp.where(el == e1, lane, big), axis=-1, keepdims=True)
    el2 = jnp.where(lane == i1, NEG_BIG, el)
    e2 = jnp.max(el2, axis=-1, keepdims=True)
    i2 = jnp.min(jnp.where(el2 == e2, lane, big), axis=-1, keepdims=True)
    esum = jnp.sum(jnp.exp(el - e1), axis=-1, keepdims=True)
    p1 = 1.0 / esum
    p2 = jnp.exp(e2 - e1) / esum
    w1 = p1 / (p1 + p2) * g_prob
    w2 = p2 / (p1 + p2) * g_prob
    hot1 = lane == i1
    hot2 = lane == i2
    cnt = jnp.where(hot1 | hot2, 1.0, 0.0)
    r = lax.broadcasted_iota(jnp.int32, (t, t), 0)
    c = lax.broadcasted_iota(jnp.int32, (t, t), 1)
    before = jnp.where(c < r, 1.0, 0.0).astype(BF16)
    prefix = _dot(before, cnt.astype(BF16)) + carry
    rank1 = jnp.sum(jnp.where(hot1, prefix, 0.0), axis=-1, keepdims=True)
    rank2 = jnp.sum(jnp.where(hot2, prefix, 0.0), axis=-1, keepdims=True)
    rec = jnp.zeros((t, ROUTE_LANES), F32)
    for idx, val in enumerate((i1 - N_GROUPS, i2 - N_GROUPS, w1, w2, rank1, rank2)):
        rec = jnp.where(lane_i == idx, val, rec)
    return rec, carry + jnp.sum(cnt, axis=0, keepdims=True)


POST_SUB = 256


def _post_kernel(x_ref, gm_ref, km_ref, vm_ref, wq_ref, wo_ref, gf_ref, wrh_ref, wrl_ref, br_ref,
                 xo_ref, t_ref, rec_ref, cnt_ref, carry):
    @pl.when(pl.program_id(0) == 0)
    def _():
        carry[...] = jnp.zeros_like(carry)

    km = km_ref[...]
    vm = vm_ref[...]
    logits = []
    for sub in range(x_ref.shape[0] // POST_SUB):
        rows = slice(sub * POST_SUB, (sub + 1) * POST_SUB)
        x = x_ref[rows, :]
        q = _dot(_rms(x, gm_ref[...]).astype(BF16), wq_ref[...]).astype(BF16)
        outs = []
        for h in range(MX_HEADS):
            sl = slice(h * MX_HD, (h + 1) * MX_HD)
            s = _dot_nt(q[:, sl], km[:, sl]) * (MX_HD ** -0.5)
            p = jnp.exp(s - jnp.max(s, axis=-1, keepdims=True))
            p = p / jnp.sum(p, axis=-1, keepdims=True)
            outs.append(_dot(p.astype(BF16), vm[:, sl]))
        x = x + _dot(jnp.concatenate(outs, axis=1).astype(BF16), wo_ref[...])
        xo_ref[rows, :] = x
        t = _rms(x, gf_ref[...])
        t_ref[rows, :] = _pack_rows(t)
        t_hi = t.astype(BF16)
        t_lo = (t - t_hi.astype(F32)).astype(BF16)
        logits.append(_dot(t_hi, wrh_ref[...]) + (_dot(t_lo, wrh_ref[...]) + _dot(t_hi, wrl_ref[...])) + br_ref[...])
    running = carry[0:1, :]
    for sub, lg in enumerate(logits):
        rec, running = _route(lg, running)
        rec_ref[sub * POST_SUB:(sub + 1) * POST_SUB, :] = rec
    carry[0:1, :] = running
    cnt_ref[...] = jnp.broadcast_to(running, cnt_ref.shape)


def _post(x, gm, km, vm, wq, wo, gf, wr, br, tm=512):
    n = x.shape[0]
    tm = min(tm, n)
    assert tm % POST_SUB == 0
    w = MX_HEADS * MX_HD
    m = km.shape[0]
    wr_hi = wr.astype(BF16)
    wr_lo = (wr - wr_hi.astype(F32)).astype(BF16)
    row = lambda wd: pl.BlockSpec((tm, wd), lambda i: (i, 0))
    full = lambda shape: pl.BlockSpec(shape, lambda i: (0,) * len(shape))
    return pl.pallas_call(
        _post_kernel,
        grid=(n // tm,),
        in_specs=[row(D_MODEL), full((1, D_MODEL)), full((m, w)), full((m, w)), full((D_MODEL, w)), full((w, D_MODEL)),
                  full((1, D_MODEL)), full((D_MODEL, ROUTE_LANES)), full((D_MODEL, ROUTE_LANES)),
                  full((1, ROUTE_LANES))],
        out_specs=[row(D_MODEL), row(PACK_W), row(ROUTE_LANES), full((8, ROUTE_LANES))],
        out_shape=[jax.ShapeDtypeStruct((n, D_MODEL), F32), jax.ShapeDtypeStruct((n, PACK_W), jnp.uint32),
                   jax.ShapeDtypeStruct((n, ROUTE_LANES), F32), jax.ShapeDtypeStruct((8, ROUTE_LANES), F32)],
        scratch_shapes=[pltpu.VMEM((8, ROUTE_LANES), F32)],
        compiler_params=_cparams(1),
        name="post",
    )(x, gm, km, vm, wq, wo, gf, wr_hi, wr_lo, br)


MOE_TB = 256


def _row_copy(src, src_row, dst, dst_row, sem):
    return pltpu.make_async_copy(src.at[pl.ds(src_row, 1), :], dst.at[pl.ds(dst_row, 1), :], sem)


def _dispatch_kernel(dest_ref, t_ref, xs_in_ref, xs_ref, sem):
    del xs_in_ref
    base = pl.program_id(0) * MOE_TB

    def issue(r, c):
        for k in range(2):
            _row_copy(t_ref, r, xs_ref, dest_ref[2 * (base + r) + k], sem).start(priority=k)
        return c

    lax.fori_loop(0, MOE_TB, issue, 0, unroll=8)
    for k in range(2):
        pltpu.make_async_copy(t_ref, xs_ref.at[pl.ds(0, MOE_TB), :], sem).wait()


def _dispatch(dest, t, n_slots):
    n = t.shape[0]
    xs0 = jnp.zeros((n_slots, PACK_W), jnp.uint32)
    return pl.pallas_call(
        _dispatch_kernel,
        grid_spec=pltpu.PrefetchScalarGridSpec(
            num_scalar_prefetch=1,
            grid=(n // MOE_TB,),
            in_specs=[pl.BlockSpec((MOE_TB, PACK_W), lambda i, d: (i, 0)), pl.BlockSpec(memory_space=pl.ANY)],
            out_specs=pl.BlockSpec(memory_space=pl.ANY),
            scratch_shapes=[pltpu.SemaphoreType.DMA(())],
        ),
        out_shape=jax.ShapeDtypeStruct((n_slots, PACK_W), jnp.uint32),
        input_output_aliases={2: 0},
        compiler_params=_cparams(1),
        name="dispatch",
    )(dest, t, xs0)


def _expert_kernel(be_ref, na_ref, xs_ref, w1_ref, w3_ref, w2_ref, ys_ref, w1b, w3b, w2b):
    i = pl.program_id(0)
    active = i < na_ref[0]
    new_expert = (i == 0) | (be_ref[i] != be_ref[jnp.maximum(i - 1, 0)])

    @pl.when(active & new_expert)
    def _():
        w1b[...] = w1_ref[...].astype(BF16)
        w3b[...] = w3_ref[...].astype(BF16)
        w2b[...] = w2_ref[...].astype(BF16)

    @pl.when(active)
    def _():
        x = _unpack_rows(xs_ref[...]).astype(BF16)
        a = _dot(x, w1b[...])
        b = _dot(x, w3b[...])
        ys_ref[...] = _pack_rows(_dot((jax.nn.silu(a) * b).astype(BF16), w2b[...]))

    @pl.when(jnp.logical_not(active))
    def _():
        ys_ref[...] = jnp.zeros_like(ys_ref)


def _experts(block_e, n_active, xs, w1, w3, w2, layer):
    n_blocks = xs.shape[0] // MOE_BLOCK
    blk = lambda i, na: jnp.minimum(i, jnp.maximum(na[0] - 1, 0))
    wspec = lambda rows, cols: pl.BlockSpec((None, None, rows, cols), lambda i, be, na: (layer, be[blk(i, na)], 0, 0))
    return pl.pallas_call(
        _expert_kernel,
        grid_spec=pltpu.PrefetchScalarGridSpec(
            num_scalar_prefetch=2,
            grid=(n_blocks,),
            in_specs=[
                pl.BlockSpec((MOE_BLOCK, PACK_W), lambda i, be, na: (blk(i, na), 0)),
                wspec(D_MODEL, D_EXPERT), wspec(D_MODEL, D_EXPERT), wspec(D_EXPERT, D_MODEL),
            ],
            out_specs=pl.BlockSpec((MOE_BLOCK, PACK_W), lambda i, be, na: (i, 0)),
            scratch_shapes=[pltpu.VMEM((D_MODEL, D_EXPERT), BF16), pltpu.VMEM((D_MODEL, D_EXPERT), BF16),
                            pltpu.VMEM((D_EXPERT, D_MODEL), BF16)],
        ),
        out_shape=jax.ShapeDtypeStruct(xs.shape, jnp.uint32),
        compiler_params=_cparams(1),
        name="experts",
    )(block_e, n_active, xs, w1, w3, w2)


def _combine_kernel(dest_ref, x_ref, rec_ref, g_ref, ys_ref, o_ref, rows, sem, *, final):
    base = pl.program_id(0) * MOE_TB

    def issue(r, c):
        for k in range(2):
            _row_copy(ys_ref, dest_ref[2 * (base + r) + k], rows.at[k], r, sem).start(priority=k)
        return c

    lax.fori_loop(0, MOE_TB, issue, 0, unroll=8)
    for k in range(2):
        pltpu.make_async_copy(ys_ref.at[pl.ds(0, MOE_TB), :], rows.at[k], sem).wait()
    rec = rec_ref[...]
    x = x_ref[...] + rec[:, 2:3] * _unpack_rows(rows[0]) + rec[:, 3:4] * _unpack_rows(rows[1])
    o_ref[...] = _rms(x, g_ref[...]) if final else x


def _combine(dest, x, rec, ys, g, final):
    n = x.shape[0]
    row = lambda wd: pl.BlockSpec((MOE_TB, wd), lambda i, d: (i, 0))
    return pl.pallas_call(
        functools.partial(_combine_kernel, final=final),
        grid_spec=pltpu.PrefetchScalarGridSpec(
            num_scalar_prefetch=1,
            grid=(n // MOE_TB,),
            in_specs=[row(D_MODEL), row(ROUTE_LANES), pl.BlockSpec((1, D_MODEL), lambda i, d: (0, 0)),
                      pl.BlockSpec(memory_space=pl.ANY)],
            out_specs=row(D_MODEL),
            scratch_shapes=[pltpu.VMEM((2, MOE_TB, PACK_W), jnp.uint32), pltpu.SemaphoreType.DMA(())],
        ),
        out_shape=jax.ShapeDtypeStruct((n, D_MODEL), F32),
        compiler_params=_cparams(1),
        name="combine",
    )(dest, x, rec, g, ys)


def _moe(x, t, rec, cnt, w1, w3, w2, layer, g_final, final):
    n = x.shape[0]
    n_asg = 2 * n
    n_blocks = -(-(n_asg + N_EXPERTS * (MOE_BLOCK - 1)) // MOE_BLOCK)
    counts = cnt[0, N_GROUPS:N_GROUPS + N_EXPERTS].astype(jnp.int32)
    padded = (counts + MOE_BLOCK - 1) // MOE_BLOCK * MOE_BLOCK
    pad_end = jnp.cumsum(padded)
    pad_start = pad_end - padded
    expert = rec[:, 0:2].astype(jnp.int32)
    first = jnp.sum(jnp.where(expert[..., None] == jnp.arange(N_EXPERTS), pad_start, 0), axis=-1)
    dest = (first + rec[:, 4:6].astype(jnp.int32)).reshape(-1)
    starts = jnp.arange(n_blocks, dtype=jnp.int32) * MOE_BLOCK
    block_e = jnp.minimum(jnp.sum(pad_end[None, :] <= starts[:, None], axis=1), N_EXPERTS - 1).astype(jnp.int32)
    n_active = (pad_end[-1:] // MOE_BLOCK).astype(jnp.int32)
    xs = _dispatch(dest, t, n_blocks * MOE_BLOCK)
    ys = _experts(block_e, n_active, xs, w1, w3, w2, layer)
    return _combine(dest, x, rec, ys, g_final, final)


def _block_diag(w):
    eye = jnp.eye(LRU_BLOCKS, dtype=w.dtype)
    return jnp.einsum("kcd,kj->kcjd", w, eye).reshape(LRU_W, LRU_W)


def _router_weights(w_group, b_group, w_router, b_router):
    pad = ROUTE_LANES - N_GROUPS - N_EXPERTS
    wr = jnp.concatenate([w_group, w_router, jnp.zeros((D_MODEL, pad), F32)], axis=1)
    br = jnp.concatenate([b_group, b_router, jnp.zeros((pad,), F32)])[None, :]
    return wr, br


def kernel(x, mem, norm_mix, w_in, conv_w, conv_b, lru_wa, lru_ba, lru_wx, lru_bx, lru_lambda, ca_rel_bias, diff_lambda, diff_subln, w_branch, w_out, norm_mem, mem_norm, w_mq, w_mk, w_mv, w_mo, norm_ffn, w_group, b_group, w_router, b_router, w1, w3, w2, final_norm):
    b, n, d = x.shape
    assert b == 1 and d == D_MODEL
    xs = x[0]
    for l in range(DEPTH):
        lam_init = 0.8 - 0.6 * math.exp(-0.3 * l)
        z, vt_b, vt_d = _inproj(xs, norm_mix[l][None, :], w_in[l].astype(BF16))
        wcat = jnp.concatenate([_block_diag(lru_wa[l]), _block_diag(lru_wx[l])], axis=1).astype(BF16)
        bcat = jnp.concatenate([lru_ba[l], lru_bx[l]])[None, :]
        ya = _lru(z, conv_w[l], conv_b[l][None, :], wcat, bcat, lru_lambda[l][None, :])
        yb = _chunkattn(z, vt_b, _chunk_bias(ca_rel_bias[l]))
        yc = _retention(z)
        yd = _diffattn(z, vt_d, diff_lambda[l], diff_subln[l][None, :], lam_init)
        x1 = _merge(xs, z, ya, yb, yc, yd, w_branch[l].astype(BF16), w_out[l].astype(BF16))
        km, vm = _memkv(mem[0], mem_norm[None, :], w_mk[l].astype(BF16), w_mv[l].astype(BF16))
        wr, br = _router_weights(w_group[l], b_group[l], w_router[l], b_router[l])
        x2, t, rec, cnt = _post(x1, norm_mem[l][None, :], km, vm, w_mq[l].astype(BF16), w_mo[l].astype(BF16),
                                norm_ffn[l][None, :], wr, br)
        xs = _moe(x2, t, rec, cnt, w1, w3, w2, l, final_norm[None, :], final=(l == DEPTH - 1))
    return xs[None]
```

```python
import functools
import math

import numpy as np
import jax
import jax.numpy as jnp
from jax import lax
from jax.experimental import pallas as pl
from jax.experimental.pallas import tpu as pltpu

F32 = jnp.float32
BF16 = jnp.bfloat16

D_MODEL = 1024
DEPTH = 2
CHUNK = 64
NORM_EPS = 1e-6
GN_EPS = 1e-5
LRU_W = 512
LRU_BLOCKS = 8
LRU_BW = LRU_W // LRU_BLOCKS
CONV_W = 4
LRU_C = 8.0
CA_HEADS = 8
CA_HD = 64
CA_PREV = 8
REL_CLIP = 128
RET_HEADS = 8
RET_HD = 64
DIFF_HEADS = 4
DIFF_HD = 64
DIFF_VD = 2 * DIFF_HD
MX_HEADS = 4
MX_HD = 128
N_GROUPS = 4
EXP_PER_GROUP = 8
N_EXPERTS = N_GROUPS * EXP_PER_GROUP
D_EXPERT = 512
MOE_BLOCK = 256
N_BRANCH = 4
BRANCH_W = 512

COL_AX, COL_AG = 0, 512
COL_BQ, COL_BK, COL_BV = 1024, 1536, 2048
COL_CQ, COL_CK, COL_CV, COL_CG = 2560, 3072, 3584, 4096
COL_DQ, COL_DK, COL_DV = 4608, 5120, 5632
COL_GATE = 6144
IN_COLS = 10240

LANES = 128
NEG_BIG = -1e30
LOG2E = 1.0 / math.log(2.0)
VMEM_LIMIT = 56 * 1024 * 1024


def _cparams(n_axes):
    return pltpu.CompilerParams(dimension_semantics=("arbitrary",) * n_axes, vmem_limit_bytes=VMEM_LIMIT)


def _rms(x, g):
    return x * lax.rsqrt(jnp.mean(x * x, axis=-1, keepdims=True) + NORM_EPS) * g


def _dot(a, b):
    return jnp.dot(a, b, preferred_element_type=F32)


def _dot_nt(a, b):
    return lax.dot_general(a, b, (((1,), (1,)), ((), ())), preferred_element_type=F32)


def _dot_tn(a, b):
    return lax.dot_general(a, b, (((0,), (0,)), ((), ())), preferred_element_type=F32)


PACK_W = D_MODEL // 2


def _pack_rows(x):
    as_bits = lambda v: lax.bitcast_convert_type(v.astype(BF16).astype(F32), jnp.uint32)
    return (as_bits(x[:, :PACK_W]) >> 16) | (as_bits(x[:, PACK_W:]) & jnp.uint32(0xFFFF0000))


def _unpack_rows(w):
    lo = lax.bitcast_convert_type(w << 16, F32)
    hi = lax.bitcast_convert_type(w & jnp.uint32(0xFFFF0000), F32)
    return jnp.concatenate([lo, hi], axis=1)


def _inproj_kernel(x_ref, g_ref, w_ref, o_ref, vtb_ref, vtd_ref, h_scr):
    j = pl.program_id(1)

    @pl.when(j == 0)
    def _():
        h_scr[...] = _rms(x_ref[...], g_ref[...]).astype(BF16)

    tn = o_ref.shape[1]
    w = w_ref[:, pl.ds(pl.multiple_of(j * tn, tn), tn)]
    acc = _dot(h_scr[...], w)
    o_ref[...] = acc.astype(o_ref.dtype)
    for col, vt_ref in ((COL_BV, vtb_ref), (COL_DV, vtd_ref)):
        off = col % tn

        @pl.when(j == col // tn)
        def _():
            vt_ref[...] = acc[:, off:off + vt_ref.shape[0]].T.astype(vt_ref.dtype)


def _inproj(x, g, w, tm=512, tn=2560):
    n = x.shape[0]
    tm = min(tm, n)
    vw = CA_HEADS * CA_HD
    assert vw == DIFF_HEADS * DIFF_VD and all(c % tn + vw <= tn for c in (COL_BV, COL_DV))
    vt_spec = pl.BlockSpec((vw, tm), lambda i, j: (0, i))
    vt_shape = jax.ShapeDtypeStruct((vw, n), BF16)
    return pl.pallas_call(
        _inproj_kernel,
        grid=(n // tm, IN_COLS // tn),
        in_specs=[
            pl.BlockSpec((tm, D_MODEL), lambda i, j: (i, 0)),
            pl.BlockSpec((1, D_MODEL), lambda i, j: (0, 0)),
            pl.BlockSpec((D_MODEL, IN_COLS), lambda i, j: (0, 0), pipeline_mode=pl.Buffered(1)),
        ],
        out_specs=[pl.BlockSpec((tm, tn), lambda i, j: (i, j)), vt_spec, vt_spec],
        out_shape=[jax.ShapeDtypeStruct((n, IN_COLS), BF16), vt_shape, vt_shape],
        scratch_shapes=[pltpu.VMEM((tm, D_MODEL), BF16)],
        compiler_params=_cparams(2),
        name="inproj",
    )(x, g, w)


def _gelu_tanh(x):
    return 0.5 * x * (1.0 + jnp.tanh(math.sqrt(2.0 / math.pi) * (x + 0.044715 * x * x * x)))


def _lru_kernel(ax_ref, ag_ref, cw_ref, cb_ref, w_ref, b_ref, lam_ref, o_ref, xbuf, a_scr, u_scr, h_scr, hc):
    t = ax_ref.shape[0]

    @pl.when(pl.program_id(0) == 0)
    def _():
        xbuf[0:8, :] = jnp.zeros((8, LRU_W), F32)
        hc[...] = jnp.zeros_like(hc)

    x = ax_ref[...].astype(F32)
    xbuf[8:8 + t, :] = x
    xc = (cb_ref[...] + cw_ref[3:4, :] * x + cw_ref[2:3, :] * xbuf[7:7 + t, :]
          + cw_ref[1:2, :] * xbuf[6:6 + t, :] + cw_ref[0:1, :] * xbuf[5:5 + t, :])
    xbuf[0:8, :] = xbuf[t:t + 8, :]
    ri = _dot(xc.astype(BF16), w_ref[...]) + b_ref[...]
    r = jax.nn.sigmoid(ri[:, :LRU_W])
    ig = jax.nn.sigmoid(ri[:, LRU_W:])
    nl = -lam_ref[...]
    softplus = jnp.maximum(nl, 0.0) + jnp.log(1.0 + jnp.exp(-jnp.abs(nl)))
    log_a = -LRU_C * r * softplus
    a = jnp.exp(log_a)
    a_scr[...] = a
    u_scr[...] = jnp.sqrt(-jnp.tanh(log_a) * (a * a + 1.0)) * (ig * xc)

    def body(s, h):
        h = a_scr[pl.ds(s, 1), :] * h + u_scr[pl.ds(s, 1), :]
        h_scr[pl.ds(s, 1), :] = h
        return h

    hc[0:1, :] = lax.fori_loop(0, t, body, hc[0:1, :], unroll=8)
    o_ref[...] = (h_scr[...] * _gelu_tanh(ag_ref[...].astype(F32))).astype(o_ref.dtype)


def _lru(z, cw, cb, wcat, bcat, lam, t=256):
    n = z.shape[0]
    t = min(t, n)
    full = lambda shape: pl.BlockSpec(shape, lambda i: (0,) * len(shape))
    return pl.pallas_call(
        _lru_kernel,
        grid=(n // t,),
        in_specs=[
            pl.BlockSpec((t, LRU_W), lambda i: (i, COL_AX // LRU_W)),
            pl.BlockSpec((t, LRU_W), lambda i: (i, COL_AG // LRU_W)),
            full((CONV_W, LRU_W)), full((1, LRU_W)), full((LRU_W, 2 * LRU_W)), full((1, 2 * LRU_W)), full((1, LRU_W)),
        ],
        out_specs=pl.BlockSpec((t, LRU_W), lambda i: (i, 0)),
        out_shape=jax.ShapeDtypeStruct((n, LRU_W), BF16),
        scratch_shapes=[pltpu.VMEM((t + 8, LRU_W), F32), pltpu.VMEM((t, LRU_W), F32), pltpu.VMEM((t, LRU_W), F32),
                        pltpu.VMEM((t, LRU_W), F32), pltpu.VMEM((8, LRU_W), F32)],
        compiler_params=_cparams(1),
        name="lru",
    )(z, z, cw, cb, wcat, bcat, lam)


CA_QB = 256
CA_KW = CA_PREV * CHUNK + CA_QB


CA_ONES = 16
CA_PAIR = 2 * CA_HD


def _chunkattn_kernel(q_ref, k0_ref, k1_ref, k2_ref, vt0_ref, vt1_ref, vt2_ref, bias_ref, o_ref):
    qt = (q_ref[...].astype(F32) * (LOG2E * CA_HD ** -0.5)).T
    k = jnp.concatenate([k0_ref[...], k1_ref[...], k2_ref[...]], axis=0)
    feat = lax.broadcasted_iota(jnp.int32, (CA_PAIR, CA_QB), 0)
    for hp in range(CA_HEADS // 2):
        lanes = slice(hp * CA_PAIR, (hp + 1) * CA_PAIR)
        qtp = qt[lanes, :]
        qq = jnp.concatenate([jnp.where(feat < CA_HD, qtp, 0.0), jnp.where(feat >= CA_HD, qtp, 0.0)], axis=1)
        t = _dot(k[:, lanes], qq.astype(BF16))
        t = t + jnp.concatenate([bias_ref[0, 2 * hp], bias_ref[0, 2 * hp + 1]], axis=1)
        p = jnp.exp2(t - jnp.max(t, axis=0, keepdims=True)).astype(BF16)
        vt = jnp.concatenate([vt0_ref[hp], vt1_ref[hp], vt2_ref[hp]], axis=1)
        vt = jnp.concatenate([vt, jnp.ones((CA_ONES, CA_KW), BF16)], axis=0)
        acc = _dot(vt, p)
        o1 = acc[0:CA_HD, 0:CA_QB] / acc[CA_PAIR:CA_PAIR + 1, 0:CA_QB]
        o2 = acc[CA_HD:CA_PAIR, CA_QB:] / acc[CA_PAIR:CA_PAIR + 1, CA_QB:]
        o_ref[:, lanes] = jnp.concatenate([o1, o2], axis=0).T.astype(o_ref.dtype)


def _chunk_bias(rel_table):
    span = CA_QB + CA_KW - 1
    n_hi = CA_KW - 1 - REL_CLIP
    n_lo = CA_QB - 1 - REL_CLIP
    e = jnp.concatenate([jnp.broadcast_to(rel_table[:, -1:], (CA_HEADS, n_hi)), rel_table[:, ::-1],
                         jnp.broadcast_to(rel_table[:, :1], (CA_HEADS, n_lo + 1))], axis=1)
    w = jnp.tile(e, (1, CA_QB))[:, :CA_QB * span].reshape(CA_HEADS, CA_QB, span)
    toeplitz = w[:, :, CA_QB - 1:CA_QB - 1 + CA_KW]
    cq = np.arange(CA_QB)[:, None] // CHUNK
    ck = np.arange(CA_KW)[None, :] // CHUNK
    band = (ck >= cq) & (ck <= cq + CA_PREV)
    first_valid = np.maximum((CA_KW - CA_QB) - CA_QB * np.arange(3), 0)
    ok = band[None] & (np.arange(CA_KW)[None, None, :] >= first_valid[:, None, None])
    bias = jnp.where(ok[:, None], (LOG2E * toeplitz.astype(F32))[None], NEG_BIG)
    return bias.transpose(0, 1, 3, 2)


def _chunkattn(z, vt, bias):
    n = z.shape[0]
    w = CA_HEADS * CA_HD
    nb = n // CA_QB
    n_pair = CA_HEADS // 2
    vt = vt.reshape(n_pair, CA_PAIR, n)
    kspec = lambda back: pl.BlockSpec((CA_QB, w), lambda i: (jnp.maximum(i - back, 0), COL_BK // w))
    vspec = lambda back: pl.BlockSpec((n_pair, CA_PAIR, CA_QB), lambda i: (0, 0, jnp.maximum(i - back, 0)))
    return pl.pallas_call(
        _chunkattn_kernel,
        grid=(nb,),
        in_specs=[
            pl.BlockSpec((CA_QB, w), lambda i: (i, COL_BQ // w)),
            kspec(2), kspec(1), kspec(0), vspec(2), vspec(1), vspec(0),
            pl.BlockSpec((1, CA_HEADS, CA_KW, CA_QB), lambda i: (jnp.minimum(i, 2), 0, 0, 0),
                         pipeline_mode=pl.Buffered(1)),
        ],
        out_specs=pl.BlockSpec((CA_QB, w), lambda i: (i, 0)),
        out_shape=jax.ShapeDtypeStruct((n, w), BF16),
        compiler_params=_cparams(1),
        name="chunkattn",
    )(z, z, z, z, vt, vt, vt, bias)


RET_T = 256
_RET_LOG_G = np.log(1.0 - 2.0 ** (-5.0 - np.arange(RET_HEADS)))


def _retention_consts(t):
    pos = np.arange(t)
    diff = pos[:, None] - pos[None, :]
    dmat = np.where(diff[None] >= 0, np.exp(_RET_LOG_G[:, None, None] * np.maximum(diff, 0)[None]), 0.0)
    dmat = dmat * (RET_HD ** -0.5)
    zeta = np.exp(_RET_LOG_G[None, :] * (t - 1 - pos)[:, None]) * (RET_HD ** -0.5)
    xi = np.exp(_RET_LOG_G[None, :] * (pos + 1)[:, None])
    rep = lambda m: np.repeat(m, RET_HD, axis=1)
    return (jnp.asarray(dmat, F32), jnp.asarray(rep(zeta), F32), jnp.asarray(rep(xi), F32))


def _retention_kernel(q_ref, k_ref, v_ref, g_ref, dmat_ref, zeta_ref, xi_ref, avg_ref, o_ref, s_scr):
    t = q_ref.shape[0]

    @pl.when(pl.program_id(0) == 0)
    def _():
        s_scr[...] = jnp.zeros_like(s_scr)

    q = q_ref[...]
    k = k_ref[...]
    v = v_ref[...]
    kz = (k.astype(F32) * zeta_ref[...]).astype(BF16)
    xi = xi_ref[...]
    g = g_ref[...].astype(F32)
    outs = []
    for h in range(RET_HEADS):
        sl = slice(h * RET_HD, (h + 1) * RET_HD)
        qh, vh = q[:, sl], v[:, sl]
        inner = _dot_nt(qh, k[:, sl]) * dmat_ref[h]
        state = s_scr[h]
        o = _dot(inner.astype(BF16), vh) + _dot(qh, state.astype(BF16)) * xi[:, sl]
        s_scr[h] = float(np.exp(_RET_LOG_G[h] * t)) * state + _dot_tn(kz[:, sl], vh)
        outs.append(o)
    o = jnp.concatenate(outs, axis=1)
    o_hi = o.astype(BF16)
    o_lo = (o - o_hi.astype(F32)).astype(BF16)
    d = o - (_dot(o_hi, avg_ref[...]) + _dot(o_lo, avg_ref[...]))
    var = _dot((d * d).astype(BF16), avg_ref[...])
    o_ref[...] = (jax.nn.silu(g) * d * lax.rsqrt(var + GN_EPS)).astype(o_ref.dtype)


def _retention(z):
    n = z.shape[0]
    t = min(RET_T, n)
    w = RET_HEADS * RET_HD
    dmat, zeta, xi = _retention_consts(t)
    avg = jnp.asarray(np.kron(np.eye(RET_HEADS), np.full((RET_HD, RET_HD), 1.0 / RET_HD)), BF16)
    col = lambda c: pl.BlockSpec((t, w), lambda i: (i, c // w))
    return pl.pallas_call(
        _retention_kernel,
        grid=(n // t,),
        in_specs=[col(COL_CQ), col(COL_CK), col(COL_CV), col(COL_CG),
                  pl.BlockSpec((RET_HEADS, t, t), lambda i: (0, 0, 0)),
                  pl.BlockSpec((t, w), lambda i: (0, 0)), pl.BlockSpec((t, w), lambda i: (0, 0)),
                  pl.BlockSpec((w, w), lambda i: (0, 0))],
        out_specs=pl.BlockSpec((t, w), lambda i: (i, 0)),
        out_shape=jax.ShapeDtypeStruct((n, w), BF16),
        scratch_shapes=[pltpu.VMEM((RET_HEADS, RET_HD, RET_HD), F32)],
        compiler_params=_cparams(1),
        name="retention",
    )(z, z, z, z, dmat, zeta, xi, avg)


DA_B = 256
DA_UNROLLS = (8,)
DA_TAIL = 2
DA_HG = 2
DA_ONES = 16
_DA_SLOPES = (2.0 ** (-8.0 * np.arange(1, DIFF_HEADS + 1) / DIFF_HEADS)).astype(np.float32)


def _bf16_parts(x, n):
    parts = []
    for _ in range(n):
        p = float(np.asarray(x, np.float32).astype(jnp.bfloat16).astype(np.float32))
        parts.append(p)
        x = x - p
    return parts


_LOG2E_PARTS = _bf16_parts(LOG2E, 3)
_DA_DIGITS = -(-(DA_B - 1).bit_length() // 8)


def _diffattn_consts():
    kloc = np.arange(DA_B)[:, None]
    qloc = (np.arange(2 * DA_B) % DA_B)[None, :]
    slopes = _DA_SLOPES[:, None, None].astype(np.float64)
    kfeat = np.zeros((DIFF_HEADS, DA_B, DIFF_VD))
    for d in range(_DA_DIGITS):
        cols = slice(d * len(_LOG2E_PARTS), (d + 1) * len(_LOG2E_PARTS))
        kfeat[:, :, cols] = slopes * (((kloc >> (8 * d)) & 255) << (8 * d))
    diag = np.where((kloc // CHUNK) <= (qloc // CHUNK), LOG2E * slopes * (qloc - np.abs(qloc - kloc) - kloc), NEG_BIG)
    bias = np.stack([np.zeros_like(diag), diag, np.full_like(diag, NEG_BIG)], axis=1)
    return jnp.asarray(kfeat, BF16), jnp.asarray(bias, F32)


def _diffattn_kernel(slope_ref, q_ref, k_ref, vt_ref, kfeat_ref, bias_ref, lam_ref, g_ref, o_ref,
                     qq_scr, m_scr, acc_scr, s_scr, p_scr, a_scr, *, lam_init):
    i = pl.program_id(1)
    blk = DA_B
    heads = range(DA_HG)
    feat = lax.broadcasted_iota(jnp.int32, (DIFF_VD, blk), 0)
    for hh in heads:
        qt = (q_ref[:, hh * DIFF_VD:(hh + 1) * DIFF_VD].astype(F32) * (LOG2E * DIFF_HD ** -0.5)).T
        qq_scr[hh, 0:DIFF_VD, 0:blk] = jnp.where(feat < DIFF_HD, qt, 0.0).astype(BF16)
        qq_scr[hh, 0:DIFF_VD, blk:] = jnp.where(feat >= DIFF_HD, qt, 0.0).astype(BF16)

    @pl.when(i == 0)
    def _():
        frow = lax.broadcasted_iota(jnp.int32, (DIFF_VD, 2 * blk), 0)
        qfeat = jnp.zeros((DIFF_VD, 2 * blk), F32)
        for idx, part in enumerate(_LOG2E_PARTS * _DA_DIGITS):
            qfeat = jnp.where(frow == idx, part, qfeat)
        for hh in heads:
            qq_scr[hh, DIFF_VD:, :] = qfeat.astype(BF16)

    m_scr[...] = jnp.full_like(m_scr, NEG_BIG)
    acc_scr[...] = jnp.zeros_like(acc_scr)

    def scores(hh, j):
        k = k_ref[pl.ds(pl.multiple_of(j * blk, blk), blk), hh * DIFF_VD:(hh + 1) * DIFF_VD]
        return _dot(jnp.concatenate([k, kfeat_ref[hh]], axis=1), qq_scr[hh])

    def softmax_step(hh, j, slot, tail):
        slope = slope_ref[pl.program_id(0) * DA_HG + hh]
        offset = slope * LOG2E * (j * blk).astype(F32)
        t = s_scr[hh, slot]
        if tail:
            t = t + bias_ref[hh, jnp.where(j < i, 0, jnp.where(j == i, 1, 2))]
        m_old = m_scr[hh]
        m_new = jnp.maximum(m_old, jnp.max(t, axis=0, keepdims=True) + offset)
        m_scr[hh] = m_new
        p_scr[hh, slot] = jnp.exp2(t - (m_new - offset)).astype(BF16)
        a_scr[hh, slot] = jnp.exp2(m_old - m_new)

    def accumulate(hh, j, slot):
        vt = vt_ref[hh, :, pl.ds(pl.multiple_of(j * blk, blk), blk)]
        vt = jnp.concatenate([vt, jnp.ones((DA_ONES, blk), BF16)], axis=0)
        acc_scr[hh] = a_scr[hh, slot] * acc_scr[hh] + _dot(vt, p_scr[hh, slot])

    def stage(j, slot, tail):
        for hh in heads:
            s_scr[hh, 1 - slot] = scores(hh, jnp.minimum(j + 1, i))
            accumulate(hh, jnp.clip(j - 1, 0, i), 1 - slot)
            softmax_step(hh, j, slot, tail)

    for hh in heads:
        s_scr[hh, 0] = scores(hh, 0)
        p_scr[hh, 1] = jnp.zeros((blk, 2 * blk), BF16)
        a_scr[hh, 1] = jnp.ones((1, 2 * blk), F32)

    def run(first, trips, unroll, tail):
        def body(jj, c):
            for u in range(unroll):
                stage(first + unroll * jj + u, u % 2, tail)
            return c

        lax.fori_loop(0, trips, body, 0)
        return first + unroll * trips

    done = 0
    for unroll in DA_UNROLLS:
        done = run(done, (i - done) // unroll, unroll, False)
    run(done, (i - done + DA_TAIL) // DA_TAIL, DA_TAIL, True)

    lv = lam_ref[...]
    lam = (jnp.exp(jnp.sum(lv[0:1, :] * lv[1:2, :], axis=-1, keepdims=True))
           - jnp.exp(jnp.sum(lv[2:3, :] * lv[3:4, :], axis=-1, keepdims=True)) + lam_init)
    for hh in heads:
        accumulate(hh, i, 1)
        o = acc_scr[hh, 0:DIFF_VD, :] / acc_scr[hh, DIFF_VD:DIFF_VD + 1, :]
        od = o[:, 0:blk] - lam * o[:, blk:]
        od = od * lax.rsqrt(jnp.mean(od * od, axis=0, keepdims=True) + GN_EPS)
        o_ref[:, hh * DIFF_VD:(hh + 1) * DIFF_VD] = (od.T * g_ref[...] * (1.0 - lam_init)).astype(o_ref.dtype)


def _diffattn(z, vt, lam_vecs, subln_g, lam_init):
    n = z.shape[0]
    assert n % DA_B == 0 and DIFF_HEADS % DA_HG == 0
    w = DIFF_VD
    gw = DA_HG * w
    kfeat, bias = _diffattn_consts()
    vt = vt.reshape(DIFF_HEADS, w, n)
    return pl.pallas_call(
        functools.partial(_diffattn_kernel, lam_init=lam_init),
        grid_spec=pltpu.PrefetchScalarGridSpec(
            num_scalar_prefetch=1,
            grid=(DIFF_HEADS // DA_HG, n // DA_B),
            in_specs=[
                pl.BlockSpec((DA_B, gw), lambda h, i, s: (i, COL_DQ // gw + h)),
                pl.BlockSpec((n, gw), lambda h, i, s: (0, COL_DK // gw + h)),
                pl.BlockSpec((DA_HG, w, n), lambda h, i, s: (h, 0, 0)),
                pl.BlockSpec((DA_HG, DA_B, w), lambda h, i, s: (h, 0, 0)),
                pl.BlockSpec((DA_HG, 3, DA_B, 2 * DA_B), lambda h, i, s: (h, 0, 0, 0), pipeline_mode=pl.Buffered(1)),
                pl.BlockSpec((4, DIFF_HD), lambda h, i, s: (0, 0)),
                pl.BlockSpec((1, w), lambda h, i, s: (0, 0)),
            ],
            out_specs=pl.BlockSpec((DA_B, gw), lambda h, i, s: (i, h)),
            scratch_shapes=[pltpu.VMEM((DA_HG, 2 * w, 2 * DA_B), BF16), pltpu.VMEM((DA_HG, 1, 2 * DA_B), F32),
                            pltpu.VMEM((DA_HG, w + DA_ONES, 2 * DA_B), F32),
                            pltpu.VMEM((DA_HG, 2, DA_B, 2 * DA_B), F32), pltpu.VMEM((DA_HG, 2, DA_B, 2 * DA_B), BF16),
                            pltpu.VMEM((DA_HG, 2, 1, 2 * DA_B), F32)],
        ),
        out_shape=jax.ShapeDtypeStruct((n, DIFF_HEADS * w), BF16),
        compiler_params=_cparams(2),
        name="diffattn",
    )(jnp.asarray(_DA_SLOPES), z, z, vt, kfeat, bias, lam_vecs, subln_g)


def _memkv_kernel(mem_ref, g_ref, wk_ref, wv_ref, k_ref, v_ref):
    mn = _rms(mem_ref[...], g_ref[...]).astype(BF16)
    k_ref[...] = _dot(mn, wk_ref[...]).astype(BF16)
    v_ref[...] = _dot(mn, wv_ref[...]).astype(BF16)


def _memkv(mem, g, wk, wv):
    m = mem.shape[0]
    w = MX_HEADS * MX_HD
    out = jax.ShapeDtypeStruct((m, w), BF16)
    return pl.pallas_call(_memkv_kernel, out_shape=(out, out), name="memkv",
                          compiler_params=pltpu.CompilerParams(vmem_limit_bytes=VMEM_LIMIT))(mem, g, wk, wv)


ROUTE_LANES = LANES


def _route(logits, carry):
    t = logits.shape[0]
    lane_i = lax.broadcasted_iota(jnp.int32, (t, ROUTE_LANES), 1)
    lane = lane_i.astype(F32)
    big = float(ROUTE_LANES)
    gl = jnp.where(lane_i < N_GROUPS, logits, NEG_BIG)
    gmax = jnp.max(gl, axis=-1, keepdims=True)
    gsum = jnp.sum(jnp.exp(gl - gmax), axis=-1, keepdims=True)
    g_sel = jnp.min(jnp.where(gl == gmax, lane, big), axis=-1, keepdims=True)
    g_prob = 1.0 / gsum
    lo = N_GROUPS + EXP_PER_GROUP * g_sel
    el = jnp.where((lane >= lo) & (lane < lo + EXP_PER_GROUP), logits, NEG_BIG)
    e1 = jnp.max(el, axis=-1, keepdims=True)
    i1 = jnp.min(jnp.where(el == e1, lane, big), axis=-1, keepdims=True)
    el2 = jnp.where(lane == i1, NEG_BIG, el)
    e2 = jnp.max(el2, axis=-1, keepdims=True)
    i2 = jnp.min(jnp.where(el2 == e2, lane, big), axis=-1, keepdims=True)
    esum = jnp.sum(jnp.exp(el - e1), axis=-1, keepdims=True)
    p1 = 1.0 / esum
    p2 = jnp.exp(e2 - e1) / esum
    w1 = p1 / (p1 + p2) * g_prob
    w2 = p2 / (p1 + p2) * g_prob
    hot1 = lane == i1
    hot2 = lane == i2
    cnt = jnp.where(hot1 | hot2, 1.0, 0.0)
    r = lax.broadcasted_iota(jnp.int32, (t, t), 0)
    c = lax.broadcasted_iota(jnp.int32, (t, t), 1)
    before = jnp.where(c < r, 1.0, 0.0).astype(BF16)
    prefix = _dot(before, cnt.astype(BF16)) + carry
    rank1 = jnp.sum(jnp.where(hot1, prefix, 0.0), axis=-1, keepdims=True)
    rank2 = jnp.sum(jnp.where(hot2, prefix, 0.0), axis=-1, keepdims=True)
    rec = jnp.zeros((t, ROUTE_LANES), F32)
    for idx, val in enumerate((i1 - N_GROUPS, i2 - N_GROUPS, w1, w2, rank1, rank2)):
        rec = jnp.where(lane_i == idx, val, rec)
    return rec, carry + jnp.sum(cnt, axis=0, keepdims=True)


POST_SUB = 256


def _mixpost_kernel(x_ref, ya_ref, yb_ref, yc_ref, yd_ref, g0_ref, g1_ref, g2_ref, g3_ref, wb_ref, wo_ref,
                    gm_ref, km_ref, vm_ref, wq_ref, wmo_ref, gf_ref, wrh_ref, wrl_ref, br_ref,
                    xo_ref, t_ref, rec_ref, cnt_ref, carry):
    @pl.when(pl.program_id(0) == 0)
    def _():
        carry[...] = jnp.zeros_like(carry)

    km = km_ref[...]
    vm = vm_ref[...]
    logits = []
    for sub in range(x_ref.shape[0] // POST_SUB):
        rows = slice(sub * POST_SUB, (sub + 1) * POST_SUB)
        mixed = None
        for y_ref, g_ref, b in ((ya_ref, g0_ref, 0), (yb_ref, g1_ref, 1), (yc_ref, g2_ref, 2), (yd_ref, g3_ref, 3)):
            term = jax.nn.sigmoid(g_ref[rows, :].astype(F32)) * _dot(y_ref[rows, :], wb_ref[b])
            mixed = term if mixed is None else mixed + term
        x = x_ref[rows, :] + _dot(mixed.astype(BF16), wo_ref[...])
        q = _dot(_rms(x, gm_ref[...]).astype(BF16), wq_ref[...]).astype(BF16)
        outs = []
        for h in range(MX_HEADS):
            sl = slice(h * MX_HD, (h + 1) * MX_HD)
            s = _dot_nt(q[:, sl], km[:, sl]) * (MX_HD ** -0.5)
            p = jnp.exp(s - jnp.max(s, axis=-1, keepdims=True))
            p = p / jnp.sum(p, axis=-1, keepdims=True)
            outs.append(_dot(p.astype(BF16), vm[:, sl]))
        x = x + _dot(jnp.concatenate(outs, axis=1).astype(BF16), wmo_ref[...])
        xo_ref[rows, :] = x
        t = _rms(x, gf_ref[...])
        t_ref[rows, :] = _pack_rows(t)
        t_hi = t.astype(BF16)
        t_lo = (t - t_hi.astype(F32)).astype(BF16)
        logits.append(_dot(t_hi, wrh_ref[...]) + (_dot(t_lo, wrh_ref[...]) + _dot(t_hi, wrl_ref[...])) + br_ref[...])
    running = carry[0:1, :]
    for sub, lg in enumerate(logits):
        rec, running = _route(lg, running)
        rec_ref[sub * POST_SUB:(sub + 1) * POST_SUB, :] = rec
    carry[0:1, :] = running
    cnt_ref[...] = jnp.broadcast_to(running, cnt_ref.shape)


def _mixpost(x, z, ya, yb, yc, yd, wb, wo, gm, km, vm, wq, wmo, gf, wr, br, tm=512):
    n = x.shape[0]
    tm = min(tm, n)
    assert tm % POST_SUB == 0
    w = MX_HEADS * MX_HD
    m = km.shape[0]
    wr_hi = wr.astype(BF16)
    wr_lo = (wr - wr_hi.astype(F32)).astype(BF16)
    row = lambda wd: pl.BlockSpec((tm, wd), lambda i: (i, 0))
    gate = lambda b: pl.BlockSpec((tm, D_MODEL), lambda i: (i, COL_GATE // D_MODEL + b))
    full = lambda shape: pl.BlockSpec(shape, lambda i: (0,) * len(shape))
    return pl.pallas_call(
        _mixpost_kernel,
        grid=(n // tm,),
        in_specs=[row(D_MODEL), row(BRANCH_W), row(BRANCH_W), row(BRANCH_W), row(BRANCH_W),
                  gate(0), gate(1), gate(2), gate(3),
                  full((N_BRANCH, BRANCH_W, D_MODEL)), full((D_MODEL, D_MODEL)),
                  full((1, D_MODEL)), full((m, w)), full((m, w)), full((D_MODEL, w)), full((w, D_MODEL)),
                  full((1, D_MODEL)), full((D_MODEL, ROUTE_LANES)), full((D_MODEL, ROUTE_LANES)),
                  full((1, ROUTE_LANES))],
        out_specs=[row(D_MODEL), row(PACK_W), row(ROUTE_LANES), full((8, ROUTE_LANES))],
        out_shape=[jax.ShapeDtypeStruct((n, D_MODEL), F32), jax.ShapeDtypeStruct((n, PACK_W), jnp.uint32),
                   jax.ShapeDtypeStruct((n, ROUTE_LANES), F32), jax.ShapeDtypeStruct((8, ROUTE_LANES), F32)],
        scratch_shapes=[pltpu.VMEM((8, ROUTE_LANES), F32)],
        compiler_params=_cparams(1),
        name="mixpost",
    )(x, ya, yb, yc, yd, z, z, z, z, wb, wo, gm, km, vm, wq, wmo, gf, wr_hi, wr_lo, br)


MOE_TB = 256


def _row_copy(src, src_row, dst, dst_row, sem):
    return pltpu.make_async_copy(src.at[pl.ds(src_row, 1), :], dst.at[pl.ds(dst_row, 1), :], sem)


def _dispatch_kernel(dest_ref, t_ref, xs_in_ref, xs_ref, sem):
    del xs_in_ref
    base = pl.program_id(0) * MOE_TB

    def issue(r, c):
        for k in range(2):
            _row_copy(t_ref, r, xs_ref, dest_ref[2 * (base + r) + k], sem).start(priority=k)
        return c

    lax.fori_loop(0, MOE_TB, issue, 0, unroll=8)
    for k in range(2):
        pltpu.make_async_copy(t_ref, xs_ref.at[pl.ds(0, MOE_TB), :], sem).wait()


def _dispatch(dest, t, n_slots):
    n = t.shape[0]
    xs0 = jnp.zeros((n_slots, PACK_W), jnp.uint32)
    return pl.pallas_call(
        _dispatch_kernel,
        grid_spec=pltpu.PrefetchScalarGridSpec(
            num_scalar_prefetch=1,
            grid=(n // MOE_TB,),
            in_specs=[pl.BlockSpec((MOE_TB, PACK_W), lambda i, d: (i, 0)), pl.BlockSpec(memory_space=pl.ANY)],
            out_specs=pl.BlockSpec(memory_space=pl.ANY),
            scratch_shapes=[pltpu.SemaphoreType.DMA(())],
        ),
        out_shape=jax.ShapeDtypeStruct((n_slots, PACK_W), jnp.uint32),
        input_output_aliases={2: 0},
        compiler_params=_cparams(1),
        name="dispatch",
    )(dest, t, xs0)


def _expert_kernel(be_ref, na_ref, xs_ref, w1_ref, w3_ref, w2_ref, ys_ref, w1b, w3b, w2b):
    i = pl.program_id(0)
    active = i < na_ref[0]
    new_expert = (i == 0) | (be_ref[i] != be_ref[jnp.maximum(i - 1, 0)])

    @pl.when(active & new_expert)
    def _():
        w1b[...] = w1_ref[...].astype(BF16)
        w3b[...] = w3_ref[...].astype(BF16)
        w2b[...] = w2_ref[...].astype(BF16)

    @pl.when(active)
    def _():
        x = _unpack_rows(xs_ref[...]).astype(BF16)
        a = _dot(x, w1b[...])
        b = _dot(x, w3b[...])
        ys_ref[...] = _pack_rows(_dot((jax.nn.silu(a) * b).astype(BF16), w2b[...]))

    @pl.when(jnp.logical_not(active))
    def _():
        ys_ref[...] = jnp.zeros_like(ys_ref)


def _experts(block_e, n_active, xs, w1, w3, w2, layer):
    n_blocks = xs.shape[0] // MOE_BLOCK
    blk = lambda i, na: jnp.minimum(i, jnp.maximum(na[0] - 1, 0))
    wspec = lambda rows, cols: pl.BlockSpec((None, None, rows, cols), lambda i, be, na: (layer, be[blk(i, na)], 0, 0))
    return pl.pallas_call(
        _expert_kernel,
        grid_spec=pltpu.PrefetchScalarGridSpec(
            num_scalar_prefetch=2,
            grid=(n_blocks,),
            in_specs=[
                pl.BlockSpec((MOE_BLOCK, PACK_W), lambda i, be, na: (blk(i, na), 0)),
                wspec(D_MODEL, D_EXPERT), wspec(D_MODEL, D_EXPERT), wspec(D_EXPERT, D_MODEL),
            ],
            out_specs=pl.BlockSpec((MOE_BLOCK, PACK_W), lambda i, be, na: (i, 0)),
            scratch_shapes=[pltpu.VMEM((D_MODEL, D_EXPERT), BF16), pltpu.VMEM((D_MODEL, D_EXPERT), BF16),
                            pltpu.VMEM((D_EXPERT, D_MODEL), BF16)],
        ),
        out_shape=jax.ShapeDtypeStruct(xs.shape, jnp.uint32),
        compiler_params=_cparams(1),
        name="experts",
    )(block_e, n_active, xs, w1, w3, w2)


def _combine_kernel(dest_ref, x_ref, rec_ref, g_ref, ys_ref, o_ref, rows, sem, *, final):
    base = pl.program_id(0) * MOE_TB

    def issue(r, c):
        for k in range(2):
            _row_copy(ys_ref, dest_ref[2 * (base + r) + k], rows.at[k], r, sem).start(priority=k)
        return c

    lax.fori_loop(0, MOE_TB, issue, 0, unroll=8)
    for k in range(2):
        pltpu.make_async_copy(ys_ref.at[pl.ds(0, MOE_TB), :], rows.at[k], sem).wait()
    rec = rec_ref[...]
    x = x_ref[...] + rec[:, 2:3] * _unpack_rows(rows[0]) + rec[:, 3:4] * _unpack_rows(rows[1])
    o_ref[...] = _rms(x, g_ref[...]) if final else x


def _combine(dest, x, rec, ys, g, final):
    n = x.shape[0]
    row = lambda wd: pl.BlockSpec((MOE_TB, wd), lambda i, d: (i, 0))
    return pl.pallas_call(
        functools.partial(_combine_kernel, final=final),
        grid_spec=pltpu.PrefetchScalarGridSpec(
            num_scalar_prefetch=1,
            grid=(n // MOE_TB,),
            in_specs=[row(D_MODEL), row(ROUTE_LANES), pl.BlockSpec((1, D_MODEL), lambda i, d: (0, 0)),
                      pl.BlockSpec(memory_space=pl.ANY)],
            out_specs=row(D_MODEL),
            scratch_shapes=[pltpu.VMEM((2, MOE_TB, PACK_W), jnp.uint32), pltpu.SemaphoreType.DMA(())],
        ),
        out_shape=jax.ShapeDtypeStruct((n, D_MODEL), F32),
        compiler_params=_cparams(1),
        name="combine",
    )(dest, x, rec, g, ys)


def _moe(x, t, rec, cnt, w1, w3, w2, layer, g_final, final):
    n = x.shape[0]
    n_asg = 2 * n
    n_blocks = -(-(n_asg + N_EXPERTS * (MOE_BLOCK - 1)) // MOE_BLOCK)
    counts = cnt[0, N_GROUPS:N_GROUPS + N_EXPERTS].astype(jnp.int32)
    padded = (counts + MOE_BLOCK - 1) // MOE_BLOCK * MOE_BLOCK
    pad_end = jnp.cumsum(padded)
    pad_start = pad_end - padded
    expert = rec[:, 0:2].astype(jnp.int32)
    first = jnp.sum(jnp.where(expert[..., None] == jnp.arange(N_EXPERTS), pad_start, 0), axis=-1)
    dest = (first + rec[:, 4:6].astype(jnp.int32)).reshape(-1)
    starts = jnp.arange(n_blocks, dtype=jnp.int32) * MOE_BLOCK
    block_e = jnp.minimum(jnp.sum(pad_end[None, :] <= starts[:, None], axis=1), N_EXPERTS - 1).astype(jnp.int32)
    n_active = (pad_end[-1:] // MOE_BLOCK).astype(jnp.int32)
    xs = _dispatch(dest, t, n_blocks * MOE_BLOCK)
    ys = _experts(block_e, n_active, xs, w1, w3, w2, layer)
    return _combine(dest, x, rec, ys, g_final, final)


def _block_diag(w):
    eye = jnp.eye(LRU_BLOCKS, dtype=w.dtype)
    return jnp.einsum("kcd,kj->kcjd", w, eye).reshape(LRU_W, LRU_W)


def _router_weights(w_group, b_group, w_router, b_router):
    pad = ROUTE_LANES - N_GROUPS - N_EXPERTS
    wr = jnp.concatenate([w_group, w_router, jnp.zeros((D_MODEL, pad), F32)], axis=1)
    br = jnp.concatenate([b_group, b_router, jnp.zeros((pad,), F32)])[None, :]
    return wr, br


def kernel(x, mem, norm_mix, w_in, conv_w, conv_b, lru_wa, lru_ba, lru_wx, lru_bx, lru_lambda, ca_rel_bias, diff_lambda, diff_subln, w_branch, w_out, norm_mem, mem_norm, w_mq, w_mk, w_mv, w_mo, norm_ffn, w_group, b_group, w_router, b_router, w1, w3, w2, final_norm):
    b, n, d = x.shape
    assert b == 1 and d == D_MODEL
    xs = x[0]
    for l in range(DEPTH):
        lam_init = 0.8 - 0.6 * math.exp(-0.3 * l)
        z, vt_b, vt_d = _inproj(xs, norm_mix[l][None, :], w_in[l].astype(BF16))
        wcat = jnp.concatenate([_block_diag(lru_wa[l]), _block_diag(lru_wx[l])], axis=1).astype(BF16)
        bcat = jnp.concatenate([lru_ba[l], lru_bx[l]])[None, :]
        ya = _lru(z, conv_w[l], conv_b[l][None, :], wcat, bcat, lru_lambda[l][None, :])
        yb = _chunkattn(z, vt_b, _chunk_bias(ca_rel_bias[l]))
        yc = _retention(z)
        yd = _diffattn(z, vt_d, diff_lambda[l], diff_subln[l][None, :], lam_init)
        km, vm = _memkv(mem[0], mem_norm[None, :], w_mk[l].astype(BF16), w_mv[l].astype(BF16))
        wr, br = _router_weights(w_group[l], b_group[l], w_router[l], b_router[l])
        x2, t, rec, cnt = _mixpost(xs, z, ya, yb, yc, yd, w_branch[l].astype(BF16), w_out[l].astype(BF16),
                                   norm_mem[l][None, :], km, vm, w_mq[l].astype(BF16), w_mo[l].astype(BF16),
                                   norm_ffn[l][None, :], wr, br)
        xs = _moe(x2, t, rec, cnt, w1, w3, w2, l, final_norm[None, :], final=(l == DEPTH - 1))
    return xs[None]
```

```python
import functools
import math

import numpy as np
import jax
import jax.numpy as jnp
from jax import lax
from jax.experimental import pallas as pl
from jax.experimental.pallas import tpu as pltpu

F32 = jnp.float32
BF16 = jnp.bfloat16

D_MODEL = 1024
DEPTH = 2
CHUNK = 64
NORM_EPS = 1e-6
GN_EPS = 1e-5
LRU_W = 512
LRU_BLOCKS = 8
LRU_BW = LRU_W // LRU_BLOCKS
CONV_W = 4
LRU_C = 8.0
CA_HEADS = 8
CA_HD = 64
CA_PREV = 8
REL_CLIP = 128
RET_HEADS = 8
RET_HD = 64
DIFF_HEADS = 4
DIFF_HD = 64
DIFF_VD = 2 * DIFF_HD
MX_HEADS = 4
MX_HD = 128
N_GROUPS = 4
EXP_PER_GROUP = 8
N_EXPERTS = N_GROUPS * EXP_PER_GROUP
D_EXPERT = 512
MOE_BLOCK = 256
N_BRANCH = 4
BRANCH_W = 512

COL_AX, COL_AG = 0, 512
COL_BQ, COL_BK, COL_BV = 1024, 1536, 2048
COL_CQ, COL_CK, COL_CV, COL_CG = 2560, 3072, 3584, 4096
COL_DQ, COL_DK, COL_DV = 4608, 5120, 5632
COL_GATE = 6144
IN_COLS = 10240

LANES = 128
NEG_BIG = -1e30
LOG2E = 1.0 / math.log(2.0)
VMEM_LIMIT = 56 * 1024 * 1024


def _cparams(n_axes):
    return pltpu.CompilerParams(dimension_semantics=("arbitrary",) * n_axes, vmem_limit_bytes=VMEM_LIMIT)


def _rms(x, g):
    return x * lax.rsqrt(jnp.mean(x * x, axis=-1, keepdims=True) + NORM_EPS) * g


def _dot(a, b):
    return jnp.dot(a, b, preferred_element_type=F32)


def _dot_nt(a, b):
    return lax.dot_general(a, b, (((1,), (1,)), ((), ())), preferred_element_type=F32)


def _dot_tn(a, b):
    return lax.dot_general(a, b, (((0,), (0,)), ((), ())), preferred_element_type=F32)


PACK_W = D_MODEL // 2


def _pack_rows(x):
    as_bits = lambda v: lax.bitcast_convert_type(v.astype(BF16).astype(F32), jnp.uint32)
    return (as_bits(x[:, :PACK_W]) >> 16) | (as_bits(x[:, PACK_W:]) & jnp.uint32(0xFFFF0000))


def _unpack_rows(w):
    lo = lax.bitcast_convert_type(w << 16, F32)
    hi = lax.bitcast_convert_type(w & jnp.uint32(0xFFFF0000), F32)
    return jnp.concatenate([lo, hi], axis=1)


def _inproj_kernel(x_ref, g_ref, w_ref, o_ref, vtb_ref, vtd_ref, h_scr):
    j = pl.program_id(1)

    @pl.when(j == 0)
    def _():
        h_scr[...] = _rms(x_ref[...], g_ref[...]).astype(BF16)

    tn = o_ref.shape[1]
    w = w_ref[:, pl.ds(pl.multiple_of(j * tn, tn), tn)]
    acc = _dot(h_scr[...], w)
    o_ref[...] = acc.astype(o_ref.dtype)
    for col, vt_ref in ((COL_BV, vtb_ref), (COL_DV, vtd_ref)):
        off = col % tn

        @pl.when(j == col // tn)
        def _():
            vt_ref[...] = acc[:, off:off + vt_ref.shape[0]].T.astype(vt_ref.dtype)


def _inproj(x, g, w, tm=512, tn=2560):
    n = x.shape[0]
    tm = min(tm, n)
    vw = CA_HEADS * CA_HD
    assert vw == DIFF_HEADS * DIFF_VD and all(c % tn + vw <= tn for c in (COL_BV, COL_DV))
    vt_spec = pl.BlockSpec((vw, tm), lambda i, j: (0, i))
    vt_shape = jax.ShapeDtypeStruct((vw, n), BF16)
    return pl.pallas_call(
        _inproj_kernel,
        grid=(n // tm, IN_COLS // tn),
        in_specs=[
            pl.BlockSpec((tm, D_MODEL), lambda i, j: (i, 0)),
            pl.BlockSpec((1, D_MODEL), lambda i, j: (0, 0)),
            pl.BlockSpec((D_MODEL, IN_COLS), lambda i, j: (0, 0), pipeline_mode=pl.Buffered(1)),
        ],
        out_specs=[pl.BlockSpec((tm, tn), lambda i, j: (i, j)), vt_spec, vt_spec],
        out_shape=[jax.ShapeDtypeStruct((n, IN_COLS), BF16), vt_shape, vt_shape],
        scratch_shapes=[pltpu.VMEM((tm, D_MODEL), BF16)],
        compiler_params=_cparams(2),
        name="inproj",
    )(x, g, w)


def _gelu_tanh(x):
    return 0.5 * x * (1.0 + jnp.tanh(math.sqrt(2.0 / math.pi) * (x + 0.044715 * x * x * x)))


def _lru_kernel(ax_ref, ag_ref, cw_ref, cb_ref, w_ref, b_ref, lam_ref, o_ref, xbuf, a_scr, u_scr, h_scr, hc):
    t = ax_ref.shape[0]

    @pl.when(pl.program_id(0) == 0)
    def _():
        xbuf[0:8, :] = jnp.zeros((8, LRU_W), F32)
        hc[...] = jnp.zeros_like(hc)

    x = ax_ref[...].astype(F32)
    xbuf[8:8 + t, :] = x
    xc = (cb_ref[...] + cw_ref[3:4, :] * x + cw_ref[2:3, :] * xbuf[7:7 + t, :]
          + cw_ref[1:2, :] * xbuf[6:6 + t, :] + cw_ref[0:1, :] * xbuf[5:5 + t, :])
    xbuf[0:8, :] = xbuf[t:t + 8, :]
    ri = _dot(xc.astype(BF16), w_ref[...]) + b_ref[...]
    r = jax.nn.sigmoid(ri[:, :LRU_W])
    ig = jax.nn.sigmoid(ri[:, LRU_W:])
    nl = -lam_ref[...]
    softplus = jnp.maximum(nl, 0.0) + jnp.log(1.0 + jnp.exp(-jnp.abs(nl)))
    log_a = -LRU_C * r * softplus
    a = jnp.exp(log_a)
    a_scr[...] = a
    u_scr[...] = jnp.sqrt(-jnp.tanh(log_a) * (a * a + 1.0)) * (ig * xc)

    def body(s, h):
        h = a_scr[pl.ds(s, 1), :] * h + u_scr[pl.ds(s, 1), :]
        h_scr[pl.ds(s, 1), :] = h
        return h

    hc[0:1, :] = lax.fori_loop(0, t, body, hc[0:1, :], unroll=8)
    o_ref[...] = (h_scr[...] * _gelu_tanh(ag_ref[...].astype(F32))).astype(o_ref.dtype)


def _lru(z, cw, cb, wcat, bcat, lam, t=256):
    n = z.shape[0]
    t = min(t, n)
    full = lambda shape: pl.BlockSpec(shape, lambda i: (0,) * len(shape))
    return pl.pallas_call(
        _lru_kernel,
        grid=(n // t,),
        in_specs=[
            pl.BlockSpec((t, LRU_W), lambda i: (i, COL_AX // LRU_W)),
            pl.BlockSpec((t, LRU_W), lambda i: (i, COL_AG // LRU_W)),
            full((CONV_W, LRU_W)), full((1, LRU_W)), full((LRU_W, 2 * LRU_W)), full((1, 2 * LRU_W)), full((1, LRU_W)),
        ],
        out_specs=pl.BlockSpec((t, LRU_W), lambda i: (i, 0)),
        out_shape=jax.ShapeDtypeStruct((n, LRU_W), BF16),
        scratch_shapes=[pltpu.VMEM((t + 8, LRU_W), F32), pltpu.VMEM((t, LRU_W), F32), pltpu.VMEM((t, LRU_W), F32),
                        pltpu.VMEM((t, LRU_W), F32), pltpu.VMEM((8, LRU_W), F32)],
        compiler_params=_cparams(1),
        name="lru",
    )(z, z, cw, cb, wcat, bcat, lam)


CA_QB = 256
CA_KW = CA_PREV * CHUNK + CA_QB


CA_ONES = 16
CA_PAIR = 2 * CA_HD


def _chunkattn_kernel(q_ref, k0_ref, k1_ref, k2_ref, vt0_ref, vt1_ref, vt2_ref, bias_ref, o_ref):
    qt = (q_ref[...].astype(F32) * (LOG2E * CA_HD ** -0.5)).T
    k = jnp.concatenate([k0_ref[...], k1_ref[...], k2_ref[...]], axis=0)
    feat = lax.broadcasted_iota(jnp.int32, (CA_PAIR, CA_QB), 0)
    for hp in range(CA_HEADS // 2):
        lanes = slice(hp * CA_PAIR, (hp + 1) * CA_PAIR)
        qtp = qt[lanes, :]
        qq = jnp.concatenate([jnp.where(feat < CA_HD, qtp, 0.0), jnp.where(feat >= CA_HD, qtp, 0.0)], axis=1)
        t = _dot(k[:, lanes], qq.astype(BF16))
        t = t + jnp.concatenate([bias_ref[0, 2 * hp], bias_ref[0, 2 * hp + 1]], axis=1)
        p = jnp.exp2(t - jnp.max(t, axis=0, keepdims=True)).astype(BF16)
        vt = jnp.concatenate([vt0_ref[hp], vt1_ref[hp], vt2_ref[hp]], axis=1)
        vt = jnp.concatenate([vt, jnp.ones((CA_ONES, CA_KW), BF16)], axis=0)
        acc = _dot(vt, p)
        o1 = acc[0:CA_HD, 0:CA_QB] / acc[CA_PAIR:CA_PAIR + 1, 0:CA_QB]
        o2 = acc[CA_HD:CA_PAIR, CA_QB:] / acc[CA_PAIR:CA_PAIR + 1, CA_QB:]
        o_ref[:, lanes] = jnp.concatenate([o1, o2], axis=0).T.astype(o_ref.dtype)


def _chunk_bias(rel_table):
    span = CA_QB + CA_KW - 1
    n_hi = CA_KW - 1 - REL_CLIP
    n_lo = CA_QB - 1 - REL_CLIP
    e = jnp.concatenate([jnp.broadcast_to(rel_table[:, -1:], (CA_HEADS, n_hi)), rel_table[:, ::-1],
                         jnp.broadcast_to(rel_table[:, :1], (CA_HEADS, n_lo + 1))], axis=1)
    w = jnp.tile(e, (1, CA_QB))[:, :CA_QB * span].reshape(CA_HEADS, CA_QB, span)
    toeplitz = w[:, :, CA_QB - 1:CA_QB - 1 + CA_KW]
    cq = np.arange(CA_QB)[:, None] // CHUNK
    ck = np.arange(CA_KW)[None, :] // CHUNK
    band = (ck >= cq) & (ck <= cq + CA_PREV)
    first_valid = np.maximum((CA_KW - CA_QB) - CA_QB * np.arange(3), 0)
    ok = band[None] & (np.arange(CA_KW)[None, None, :] >= first_valid[:, None, None])
    bias = jnp.where(ok[:, None], (LOG2E * toeplitz.astype(F32))[None], NEG_BIG)
    return bias.transpose(0, 1, 3, 2)


def _chunkattn(z, vt, bias):
    n = z.shape[0]
    w = CA_HEADS * CA_HD
    nb = n // CA_QB
    n_pair = CA_HEADS // 2
    vt = vt.reshape(n_pair, CA_PAIR, n)
    kspec = lambda back: pl.BlockSpec((CA_QB, w), lambda i: (jnp.maximum(i - back, 0), COL_BK // w))
    vspec = lambda back: pl.BlockSpec((n_pair, CA_PAIR, CA_QB), lambda i: (0, 0, jnp.maximum(i - back, 0)))
    return pl.pallas_call(
        _chunkattn_kernel,
        grid=(nb,),
        in_specs=[
            pl.BlockSpec((CA_QB, w), lambda i: (i, COL_BQ // w)),
            kspec(2), kspec(1), kspec(0), vspec(2), vspec(1), vspec(0),
            pl.BlockSpec((1, CA_HEADS, CA_KW, CA_QB), lambda i: (jnp.minimum(i, 2), 0, 0, 0),
                         pipeline_mode=pl.Buffered(1)),
        ],
        out_specs=pl.BlockSpec((CA_QB, w), lambda i: (i, 0)),
        out_shape=jax.ShapeDtypeStruct((n, w), BF16),
        compiler_params=_cparams(1),
        name="chunkattn",
    )(z, z, z, z, vt, vt, vt, bias)


RET_T = 256
_RET_LOG_G = np.log(1.0 - 2.0 ** (-5.0 - np.arange(RET_HEADS)))


def _retention_consts(t):
    pos = np.arange(t)
    diff = pos[:, None] - pos[None, :]
    dmat = np.where(diff[None] >= 0, np.exp(_RET_LOG_G[:, None, None] * np.maximum(diff, 0)[None]), 0.0)
    dmat = dmat * (RET_HD ** -0.5)
    zeta = np.exp(_RET_LOG_G[None, :] * (t - 1 - pos)[:, None]) * (RET_HD ** -0.5)
    xi = np.exp(_RET_LOG_G[None, :] * (pos + 1)[:, None])
    rep = lambda m: np.repeat(m, RET_HD, axis=1)
    return (jnp.asarray(dmat, F32), jnp.asarray(rep(zeta), F32), jnp.asarray(rep(xi), F32))


def _retention_kernel(q_ref, k_ref, v_ref, g_ref, dmat_ref, zeta_ref, xi_ref, avg_ref, o_ref, s_scr):
    t = q_ref.shape[0]

    @pl.when(pl.program_id(0) == 0)
    def _():
        s_scr[...] = jnp.zeros_like(s_scr)

    q = q_ref[...]
    k = k_ref[...]
    v = v_ref[...]
    kz = (k.astype(F32) * zeta_ref[...]).astype(BF16)
    xi = xi_ref[...]
    g = g_ref[...].astype(F32)
    outs = []
    for h in range(RET_HEADS):
        sl = slice(h * RET_HD, (h + 1) * RET_HD)
        qh, vh = q[:, sl], v[:, sl]
        inner = _dot_nt(qh, k[:, sl]) * dmat_ref[h]
        state = s_scr[h]
        o = _dot(inner.astype(BF16), vh) + _dot(qh, state.astype(BF16)) * xi[:, sl]
        s_scr[h] = float(np.exp(_RET_LOG_G[h] * t)) * state + _dot_tn(kz[:, sl], vh)
        outs.append(o)
    o = jnp.concatenate(outs, axis=1)
    o_hi = o.astype(BF16)
    o_lo = (o - o_hi.astype(F32)).astype(BF16)
    d = o - (_dot(o_hi, avg_ref[...]) + _dot(o_lo, avg_ref[...]))
    var = _dot((d * d).astype(BF16), avg_ref[...])
    o_ref[...] = (jax.nn.silu(g) * d * lax.rsqrt(var + GN_EPS)).astype(o_ref.dtype)


def _retention(z):
    n = z.shape[0]
    t = min(RET_T, n)
    w = RET_HEADS * RET_HD
    dmat, zeta, xi = _retention_consts(t)
    avg = jnp.asarray(np.kron(np.eye(RET_HEADS), np.full((RET_HD, RET_HD), 1.0 / RET_HD)), BF16)
    col = lambda c: pl.BlockSpec((t, w), lambda i: (i, c // w))
    return pl.pallas_call(
        _retention_kernel,
        grid=(n // t,),
        in_specs=[col(COL_CQ), col(COL_CK), col(COL_CV), col(COL_CG),
                  pl.BlockSpec((RET_HEADS, t, t), lambda i: (0, 0, 0)),
                  pl.BlockSpec((t, w), lambda i: (0, 0)), pl.BlockSpec((t, w), lambda i: (0, 0)),
                  pl.BlockSpec((w, w), lambda i: (0, 0))],
        out_specs=pl.BlockSpec((t, w), lambda i: (i, 0)),
        out_shape=jax.ShapeDtypeStruct((n, w), BF16),
        scratch_shapes=[pltpu.VMEM((RET_HEADS, RET_HD, RET_HD), F32)],
        compiler_params=_cparams(1),
        name="retention",
    )(z, z, z, z, dmat, zeta, xi, avg)


DA_B = 256
DA_UNROLLS = (8, 2)
DA_HG = 2
DA_ONES = 16
_DA_SLOPES = (2.0 ** (-8.0 * np.arange(1, DIFF_HEADS + 1) / DIFF_HEADS)).astype(np.float32)


def _bf16_parts(x, n):
    parts = []
    for _ in range(n):
        p = float(np.asarray(x, np.float32).astype(jnp.bfloat16).astype(np.float32))
        parts.append(p)
        x = x - p
    return parts


_LOG2E_PARTS = _bf16_parts(LOG2E, 3)
_DA_DIGITS = -(-(DA_B - 1).bit_length() // 8)


def _diffattn_consts():
    kloc = np.arange(DA_B)[:, None]
    qloc = (np.arange(2 * DA_B) % DA_B)[None, :]
    slopes = _DA_SLOPES[:, None, None].astype(np.float64)
    kfeat = np.zeros((DIFF_HEADS, DA_B, DIFF_VD))
    for d in range(_DA_DIGITS):
        cols = slice(d * len(_LOG2E_PARTS), (d + 1) * len(_LOG2E_PARTS))
        kfeat[:, :, cols] = slopes * (((kloc >> (8 * d)) & 255) << (8 * d))
    diag = np.where((kloc // CHUNK) <= (qloc // CHUNK), LOG2E * slopes * (qloc - np.abs(qloc - kloc) - kloc), NEG_BIG)
    return jnp.asarray(kfeat, BF16), jnp.asarray(diag, F32)


def _diffattn_kernel(slope_ref, q_ref, k_ref, vt_ref, kfeat_ref, bias_ref, lam_ref, g_ref, o_ref,
                     qq_scr, m_scr, acc_scr, s_scr, p_scr, a_scr, *, lam_init):
    i = pl.program_id(1)
    blk = DA_B
    heads = range(DA_HG)
    feat = lax.broadcasted_iota(jnp.int32, (DIFF_VD, blk), 0)
    for hh in heads:
        qt = (q_ref[:, hh * DIFF_VD:(hh + 1) * DIFF_VD].astype(F32) * (LOG2E * DIFF_HD ** -0.5)).T
        qq_scr[hh, 0:DIFF_VD, 0:blk] = jnp.where(feat < DIFF_HD, qt, 0.0).astype(BF16)
        qq_scr[hh, 0:DIFF_VD, blk:] = jnp.where(feat >= DIFF_HD, qt, 0.0).astype(BF16)

    @pl.when(i == 0)
    def _():
        frow = lax.broadcasted_iota(jnp.int32, (DIFF_VD, 2 * blk), 0)
        qfeat = jnp.zeros((DIFF_VD, 2 * blk), F32)
        for idx, part in enumerate(_LOG2E_PARTS * _DA_DIGITS):
            qfeat = jnp.where(frow == idx, part, qfeat)
        for hh in heads:
            qq_scr[hh, DIFF_VD:, :] = qfeat.astype(BF16)

    m_scr[...] = jnp.full_like(m_scr, NEG_BIG)
    acc_scr[...] = jnp.zeros_like(acc_scr)

    def scores(hh, j):
        k = k_ref[pl.ds(pl.multiple_of(j * blk, blk), blk), hh * DIFF_VD:(hh + 1) * DIFF_VD]
        return _dot(jnp.concatenate([k, kfeat_ref[hh]], axis=1), qq_scr[hh])

    def softmax_step(hh, j, slot, diag):
        slope = slope_ref[pl.program_id(0) * DA_HG + hh]
        offset = slope * LOG2E * (j * blk).astype(F32)
        t = s_scr[hh, slot]
        if diag:
            t = t + bias_ref[hh]
        m_old = m_scr[hh]
        m_new = jnp.maximum(m_old, jnp.max(t, axis=0, keepdims=True) + offset)
        m_scr[hh] = m_new
        p_scr[hh, slot] = jnp.exp2(t - (m_new - offset)).astype(BF16)
        a_scr[hh, slot] = jnp.exp2(m_old - m_new)

    def accumulate(hh, j, slot):
        vt = vt_ref[hh, :, pl.ds(pl.multiple_of(j * blk, blk), blk)]
        vt = jnp.concatenate([vt, jnp.ones((DA_ONES, blk), BF16)], axis=0)
        acc_scr[hh] = a_scr[hh, slot] * acc_scr[hh] + _dot(vt, p_scr[hh, slot])

    def stage(j, slot, last=False):
        for hh in heads:
            if not last:
                s_scr[hh, 1 - slot] = scores(hh, j + 1)
            accumulate(hh, jnp.maximum(j - 1, 0), 1 - slot)
            softmax_step(hh, j, slot, last)

    for hh in heads:
        s_scr[hh, 0] = scores(hh, 0)
        p_scr[hh, 1] = jnp.zeros((blk, 2 * blk), BF16)
        a_scr[hh, 1] = jnp.ones((1, 2 * blk), F32)

    def run(first, trips, unroll):
        def body(jj, c):
            for u in range(unroll):
                stage(first + unroll * jj + u, u % 2)
            return c

        lax.fori_loop(0, trips, body, 0)
        return first + unroll * trips

    done = 0
    for unroll in DA_UNROLLS:
        done = run(done, (i - done) // unroll, unroll)
    odd = i - done == 1

    @pl.when(odd)
    def _():
        stage(i - 1, 0)
        stage(i, 1, last=True)
        for hh in heads:
            accumulate(hh, i, 1)

    @pl.when(jnp.logical_not(odd))
    def _():
        stage(i, 0, last=True)
        for hh in heads:
            accumulate(hh, i, 0)

    lv = lam_ref[...]
    lam = (jnp.exp(jnp.sum(lv[0:1, :] * lv[1:2, :], axis=-1, keepdims=True))
           - jnp.exp(jnp.sum(lv[2:3, :] * lv[3:4, :], axis=-1, keepdims=True)) + lam_init)
    for hh in heads:
        o = acc_scr[hh, 0:DIFF_VD, :] / acc_scr[hh, DIFF_VD:DIFF_VD + 1, :]
        od = o[:, 0:blk] - lam * o[:, blk:]
        od = od * lax.rsqrt(jnp.mean(od * od, axis=0, keepdims=True) + GN_EPS)
        o_ref[:, hh * DIFF_VD:(hh + 1) * DIFF_VD] = (od.T * g_ref[...] * (1.0 - lam_init)).astype(o_ref.dtype)


def _diffattn(z, vt, lam_vecs, subln_g, lam_init):
    n = z.shape[0]
    assert n % DA_B == 0 and DIFF_HEADS % DA_HG == 0
    w = DIFF_VD
    gw = DA_HG * w
    kfeat, bias = _diffattn_consts()
    vt = vt.reshape(DIFF_HEADS, w, n)
    return pl.pallas_call(
        functools.partial(_diffattn_kernel, lam_init=lam_init),
        grid_spec=pltpu.PrefetchScalarGridSpec(
            num_scalar_prefetch=1,
            grid=(DIFF_HEADS // DA_HG, n // DA_B),
            in_specs=[
                pl.BlockSpec((DA_B, gw), lambda h, i, s: (i, COL_DQ // gw + h)),
                pl.BlockSpec((n, gw), lambda h, i, s: (0, COL_DK // gw + h)),
                pl.BlockSpec((DA_HG, w, n), lambda h, i, s: (h, 0, 0)),
                pl.BlockSpec((DA_HG, DA_B, w), lambda h, i, s: (h, 0, 0)),
                pl.BlockSpec((DA_HG, DA_B, 2 * DA_B), lambda h, i, s: (h, 0, 0)),
                pl.BlockSpec((4, DIFF_HD), lambda h, i, s: (0, 0)),
                pl.BlockSpec((1, w), lambda h, i, s: (0, 0)),
            ],
            out_specs=pl.BlockSpec((DA_B, gw), lambda h, i, s: (i, h)),
            scratch_shapes=[pltpu.VMEM((DA_HG, 2 * w, 2 * DA_B), BF16), pltpu.VMEM((DA_HG, 1, 2 * DA_B), F32),
                            pltpu.VMEM((DA_HG, w + DA_ONES, 2 * DA_B), F32),
                            pltpu.VMEM((DA_HG, 2, DA_B, 2 * DA_B), F32), pltpu.VMEM((DA_HG, 2, DA_B, 2 * DA_B), BF16),
                            pltpu.VMEM((DA_HG, 2, 1, 2 * DA_B), F32)],
        ),
        out_shape=jax.ShapeDtypeStruct((n, DIFF_HEADS * w), BF16),
        compiler_params=_cparams(2),
        name="diffattn",
    )(jnp.asarray(_DA_SLOPES), z, z, vt, kfeat, bias, lam_vecs, subln_g)


def _memkv_kernel(mem_ref, g_ref, wk_ref, wv_ref, k_ref, v_ref):
    mn = _rms(mem_ref[...], g_ref[...]).astype(BF16)
    k_ref[...] = _dot(mn, wk_ref[...]).astype(BF16)
    v_ref[...] = _dot(mn, wv_ref[...]).astype(BF16)


def _memkv(mem, g, wk, wv):
    m = mem.shape[0]
    w = MX_HEADS * MX_HD
    out = jax.ShapeDtypeStruct((m, w), BF16)
    return pl.pallas_call(_memkv_kernel, out_shape=(out, out), name="memkv",
                          compiler_params=pltpu.CompilerParams(vmem_limit_bytes=VMEM_LIMIT))(mem, g, wk, wv)


ROUTE_LANES = LANES


def _route(logits, carry):
    t = logits.shape[0]
    lane_i = lax.broadcasted_iota(jnp.int32, (t, ROUTE_LANES), 1)
    lane = lane_i.astype(F32)
    big = float(ROUTE_LANES)
    gl = jnp.where(lane_i < N_GROUPS, logits, NEG_BIG)
    gmax = jnp.max(gl, axis=-1, keepdims=True)
    gsum = jnp.sum(jnp.exp(gl - gmax), axis=-1, keepdims=True)
    g_sel = jnp.min(jnp.where(gl == gmax, lane, big), axis=-1, keepdims=True)
    g_prob = 1.0 / gsum
    lo = N_GROUPS + EXP_PER_GROUP * g_sel
    el = jnp.where((lane >= lo) & (lane < lo + EXP_PER_GROUP), logits, NEG_BIG)
    e1 = jnp.max(el, axis=-1, keepdims=True)
    i1 = jnp.min(jnp.where(el == e1, lane, big), axis=-1, keepdims=True)
    el2 = jnp.where(lane == i1, NEG_BIG, el)
    e2 = jnp.max(el2, axis=-1, keepdims=True)
    i2 = jnp.min(jnp.where(el2 == e2, lane, big), axis=-1, keepdims=True)
    esum = jnp.sum(jnp.exp(el - e1), axis=-1, keepdims=True)
    p1 = 1.0 / esum
    p2 = jnp.exp(e2 - e1) / esum
    w1 = p1 / (p1 + p2) * g_prob
    w2 = p2 / (p1 + p2) * g_prob
    hot1 = lane == i1
    hot2 = lane == i2
    cnt = jnp.where(hot1 | hot2, 1.0, 0.0)
    r = lax.broadcasted_iota(jnp.int32, (t, t), 0)
    c = lax.broadcasted_iota(jnp.int32, (t, t), 1)
    before = jnp.where(c < r, 1.0, 0.0).astype(BF16)
    prefix = _dot(before, cnt.astype(BF16)) + carry
    rank1 = jnp.sum(jnp.where(hot1, prefix, 0.0), axis=-1, keepdims=True)
    rank2 = jnp.sum(jnp.where(hot2, prefix, 0.0), axis=-1, keepdims=True)
    rec = jnp.zeros((t, ROUTE_LANES), F32)
    for idx, val in enumerate((i1 - N_GROUPS, i2 - N_GROUPS, w1, w2, rank1, rank2)):
        rec = jnp.where(lane_i == idx, val, rec)
    return rec, carry + jnp.sum(cnt, axis=0, keepdims=True)


POST_SUB = 256


def _mixpost_kernel(x_ref, ya_ref, yb_ref, yc_ref, yd_ref, g0_ref, g1_ref, g2_ref, g3_ref, wb_ref, wo_ref,
                    gm_ref, km_ref, vm_ref, wq_ref, wmo_ref, gf_ref, wrh_ref, wrl_ref, br_ref,
                    xo_ref, t_ref, rec_ref, cnt_ref, carry):
    @pl.when(pl.program_id(0) == 0)
    def _():
        carry[...] = jnp.zeros_like(carry)

    km = km_ref[...]
    vm = vm_ref[...]
    logits = []
    for sub in range(x_ref.shape[0] // POST_SUB):
        rows = slice(sub * POST_SUB, (sub + 1) * POST_SUB)
        mixed = None
        for y_ref, g_ref, b in ((ya_ref, g0_ref, 0), (yb_ref, g1_ref, 1), (yc_ref, g2_ref, 2), (yd_ref, g3_ref, 3)):
            term = jax.nn.sigmoid(g_ref[rows, :].astype(F32)) * _dot(y_ref[rows, :], wb_ref[b])
            mixed = term if mixed is None else mixed + term
        x = x_ref[rows, :] + _dot(mixed.astype(BF16), wo_ref[...])
        q = _dot(_rms(x, gm_ref[...]).astype(BF16), wq_ref[...]).astype(BF16)
        outs = []
        for h in range(MX_HEADS):
            sl = slice(h * MX_HD, (h + 1) * MX_HD)
            s = _dot_nt(q[:, sl], km[:, sl]) * (MX_HD ** -0.5)
            p = jnp.exp(s - jnp.max(s, axis=-1, keepdims=True))
            p = p / jnp.sum(p, axis=-1, keepdims=True)
            outs.append(_dot(p.astype(BF16), vm[:, sl]))
        x = x + _dot(jnp.concatenate(outs, axis=1).astype(BF16), wmo_ref[...])
        xo_ref[rows, :] = x
        t = _rms(x, gf_ref[...])
        t_ref[rows, :] = _pack_rows(t)
        t_hi = t.astype(BF16)
        t_lo = (t - t_hi.astype(F32)).astype(BF16)
        logits.append(_dot(t_hi, wrh_ref[...]) + (_dot(t_lo, wrh_ref[...]) + _dot(t_hi, wrl_ref[...])) + br_ref[...])
    running = carry[0:1, :]
    for sub, lg in enumerate(logits):
        rec, running = _route(lg, running)
        rec_ref[sub * POST_SUB:(sub + 1) * POST_SUB, :] = rec
    carry[0:1, :] = running
    cnt_ref[...] = jnp.broadcast_to(running, cnt_ref.shape)


def _mixpost(x, z, ya, yb, yc, yd, wb, wo, gm, km, vm, wq, wmo, gf, wr, br, tm=512):
    n = x.shape[0]
    tm = min(tm, n)
    assert tm % POST_SUB == 0
    w = MX_HEADS * MX_HD
    m = km.shape[0]
    wr_hi = wr.astype(BF16)
    wr_lo = (wr - wr_hi.astype(F32)).astype(BF16)
    row = lambda wd: pl.BlockSpec((tm, wd), lambda i: (i, 0))
    gate = lambda b: pl.BlockSpec((tm, D_MODEL), lambda i: (i, COL_GATE // D_MODEL + b))
    full = lambda shape: pl.BlockSpec(shape, lambda i: (0,) * len(shape))
    return pl.pallas_call(
        _mixpost_kernel,
        grid=(n // tm,),
        in_specs=[row(D_MODEL), row(BRANCH_W), row(BRANCH_W), row(BRANCH_W), row(BRANCH_W),
                  gate(0), gate(1), gate(2), gate(3),
                  full((N_BRANCH, BRANCH_W, D_MODEL)), full((D_MODEL, D_MODEL)),
                  full((1, D_MODEL)), full((m, w)), full((m, w)), full((D_MODEL, w)), full((w, D_MODEL)),
                  full((1, D_MODEL)), full((D_MODEL, ROUTE_LANES)), full((D_MODEL, ROUTE_LANES)),
                  full((1, ROUTE_LANES))],
        out_specs=[row(D_MODEL), row(PACK_W), row(ROUTE_LANES), full((8, ROUTE_LANES))],
        out_shape=[jax.ShapeDtypeStruct((n, D_MODEL), F32), jax.ShapeDtypeStruct((n, PACK_W), jnp.uint32),
                   jax.ShapeDtypeStruct((n, ROUTE_LANES), F32), jax.ShapeDtypeStruct((8, ROUTE_LANES), F32)],
        scratch_shapes=[pltpu.VMEM((8, ROUTE_LANES), F32)],
        compiler_params=_cparams(1),
        name="mixpost",
    )(x, ya, yb, yc, yd, z, z, z, z, wb, wo, gm, km, vm, wq, wmo, gf, wr_hi, wr_lo, br)


MOE_TB = 256


def _row_copy(src, src_row, dst, dst_row, sem):
    return pltpu.make_async_copy(src.at[pl.ds(src_row, 1), :], dst.at[pl.ds(dst_row, 1), :], sem)


def _dispatch_kernel(dest_ref, t_ref, xs_in_ref, xs_ref, sem):
    del xs_in_ref
    base = pl.program_id(0) * MOE_TB

    def issue(r, c):
        for k in range(2):
            _row_copy(t_ref, r, xs_ref, dest_ref[2 * (base + r) + k], sem).start(priority=k)
        return c

    lax.fori_loop(0, MOE_TB, issue, 0, unroll=8)
    for k in range(2):
        pltpu.make_async_copy(t_ref, xs_ref.at[pl.ds(0, MOE_TB), :], sem).wait()


def _dispatch(dest, t, n_slots):
    n = t.shape[0]
    xs0 = jnp.zeros((n_slots, PACK_W), jnp.uint32)
    return pl.pallas_call(
        _dispatch_kernel,
        grid_spec=pltpu.PrefetchScalarGridSpec(
            num_scalar_prefetch=1,
            grid=(n // MOE_TB,),
            in_specs=[pl.BlockSpec((MOE_TB, PACK_W), lambda i, d: (i, 0)), pl.BlockSpec(memory_space=pl.ANY)],
            out_specs=pl.BlockSpec(memory_space=pl.ANY),
            scratch_shapes=[pltpu.SemaphoreType.DMA(())],
        ),
        out_shape=jax.ShapeDtypeStruct((n_slots, PACK_W), jnp.uint32),
        input_output_aliases={2: 0},
        compiler_params=_cparams(1),
        name="dispatch",
    )(dest, t, xs0)


def _expert_kernel(be_ref, na_ref, xs_ref, w1_ref, w3_ref, w2_ref, ys_ref, w1b, w3b, w2b):
    i = pl.program_id(0)
    active = i < na_ref[0]
    new_expert = (i == 0) | (be_ref[i] != be_ref[jnp.maximum(i - 1, 0)])

    @pl.when(active & new_expert)
    def _():
        w1b[...] = w1_ref[...].astype(BF16)
        w3b[...] = w3_ref[...].astype(BF16)
        w2b[...] = w2_ref[...].astype(BF16)

    @pl.when(active)
    def _():
        x = _unpack_rows(xs_ref[...]).astype(BF16)
        a = _dot(x, w1b[...])
        b = _dot(x, w3b[...])
        ys_ref[...] = _pack_rows(_dot((jax.nn.silu(a) * b).astype(BF16), w2b[...]))

    @pl.when(jnp.logical_not(active))
    def _():
        ys_ref[...] = jnp.zeros_like(ys_ref)


def _experts(block_e, n_active, xs, w1, w3, w2, layer):
    n_blocks = xs.shape[0] // MOE_BLOCK
    blk = lambda i, na: jnp.minimum(i, jnp.maximum(na[0] - 1, 0))
    wspec = lambda rows, cols: pl.BlockSpec((None, None, rows, cols), lambda i, be, na: (layer, be[blk(i, na)], 0, 0))
    return pl.pallas_call(
        _expert_kernel,
        grid_spec=pltpu.PrefetchScalarGridSpec(
            num_scalar_prefetch=2,
            grid=(n_blocks,),
            in_specs=[
                pl.BlockSpec((MOE_BLOCK, PACK_W), lambda i, be, na: (blk(i, na), 0)),
                wspec(D_MODEL, D_EXPERT), wspec(D_MODEL, D_EXPERT), wspec(D_EXPERT, D_MODEL),
            ],
            out_specs=pl.BlockSpec((MOE_BLOCK, PACK_W), lambda i, be, na: (i, 0)),
            scratch_shapes=[pltpu.VMEM((D_MODEL, D_EXPERT), BF16), pltpu.VMEM((D_MODEL, D_EXPERT), BF16),
                            pltpu.VMEM((D_EXPERT, D_MODEL), BF16)],
        ),
        out_shape=jax.ShapeDtypeStruct(xs.shape, jnp.uint32),
        compiler_params=_cparams(1),
        name="experts",
    )(block_e, n_active, xs, w1, w3, w2)


def _combine_kernel(dest_ref, x_ref, rec_ref, g_ref, ys_ref, o_ref, rows, sem, *, final):
    base = pl.program_id(0) * MOE_TB

    def issue(r, c):
        for k in range(2):
            _row_copy(ys_ref, dest_ref[2 * (base + r) + k], rows.at[k], r, sem).start(priority=k)
        return c

    lax.fori_loop(0, MOE_TB, issue, 0, unroll=8)
    for k in range(2):
        pltpu.make_async_copy(ys_ref.at[pl.ds(0, MOE_TB), :], rows.at[k], sem).wait()
    rec = rec_ref[...]
    x = x_ref[...] + rec[:, 2:3] * _unpack_rows(rows[0]) + rec[:, 3:4] * _unpack_rows(rows[1])
    o_ref[...] = _rms(x, g_ref[...]) if final else x


def _combine(dest, x, rec, ys, g, final):
    n = x.shape[0]
    row = lambda wd: pl.BlockSpec((MOE_TB, wd), lambda i, d: (i, 0))
    return pl.pallas_call(
        functools.partial(_combine_kernel, final=final),
        grid_spec=pltpu.PrefetchScalarGridSpec(
            num_scalar_prefetch=1,
            grid=(n // MOE_TB,),
            in_specs=[row(D_MODEL), row(ROUTE_LANES), pl.BlockSpec((1, D_MODEL), lambda i, d: (0, 0)),
                      pl.BlockSpec(memory_space=pl.ANY)],
            out_specs=row(D_MODEL),
            scratch_shapes=[pltpu.VMEM((2, MOE_TB, PACK_W), jnp.uint32), pltpu.SemaphoreType.DMA(())],
        ),
        out_shape=jax.ShapeDtypeStruct((n, D_MODEL), F32),
        compiler_params=_cparams(1),
        name="combine",
    )(dest, x, rec, g, ys)


def _moe(x, t, rec, cnt, w1, w3, w2, layer, g_final, final):
    n = x.shape[0]
    n_asg = 2 * n
    n_blocks = -(-(n_asg + N_EXPERTS * (MOE_BLOCK - 1)) // MOE_BLOCK)
    counts = cnt[0, N_GROUPS:N_GROUPS + N_EXPERTS].astype(jnp.int32)
    padded = (counts + MOE_BLOCK - 1) // MOE_BLOCK * MOE_BLOCK
    pad_end = jnp.cumsum(padded)
    pad_start = pad_end - padded
    expert = rec[:, 0:2].astype(jnp.int32)
    first = jnp.sum(jnp.where(expert[..., None] == jnp.arange(N_EXPERTS), pad_start, 0), axis=-1)
    dest = (first + rec[:, 4:6].astype(jnp.int32)).reshape(-1)
    starts = jnp.arange(n_blocks, dtype=jnp.int32) * MOE_BLOCK
    block_e = jnp.minimum(jnp.sum(pad_end[None, :] <= starts[:, None], axis=1), N_EXPERTS - 1).astype(jnp.int32)
    n_active = (pad_end[-1:] // MOE_BLOCK).astype(jnp.int32)
    xs = _dispatch(dest, t, n_blocks * MOE_BLOCK)
    ys = _experts(block_e, n_active, xs, w1, w3, w2, layer)
    return _combine(dest, x, rec, ys, g_final, final)


def _block_diag(w):
    eye = jnp.eye(LRU_BLOCKS, dtype=w.dtype)
    return jnp.einsum("kcd,kj->kcjd", w, eye).reshape(LRU_W, LRU_W)


def _router_weights(w_group, b_group, w_router, b_router):
    pad = ROUTE_LANES - N_GROUPS - N_EXPERTS
    wr = jnp.concatenate([w_group, w_router, jnp.zeros((D_MODEL, pad), F32)], axis=1)
    br = jnp.concatenate([b_group, b_router, jnp.zeros((pad,), F32)])[None, :]
    return wr, br


def kernel(x, mem, norm_mix, w_in, conv_w, conv_b, lru_wa, lru_ba, lru_wx, lru_bx, lru_lambda, ca_rel_bias, diff_lambda, diff_subln, w_branch, w_out, norm_mem, mem_norm, w_mq, w_mk, w_mv, w_mo, norm_ffn, w_group, b_group, w_router, b_router, w1, w3, w2, final_norm):
    b, n, d = x.shape
    assert b == 1 and d == D_MODEL
    xs = x[0]
    for l in range(DEPTH):
        lam_init = 0.8 - 0.6 * math.exp(-0.3 * l)
        z, vt_b, vt_d = _inproj(xs, norm_mix[l][None, :], w_in[l].astype(BF16))
        wcat = jnp.concatenate([_block_diag(lru_wa[l]), _block_diag(lru_wx[l])], axis=1).astype(BF16)
        bcat = jnp.concatenate([lru_ba[l], lru_bx[l]])[None, :]
        ya = _lru(z, conv_w[l], conv_b[l][None, :], wcat, bcat, lru_lambda[l][None, :])
        yb = _chunkattn(z, vt_b, _chunk_bias(ca_rel_bias[l]))
        yc = _retention(z)
        yd = _diffattn(z, vt_d, diff_lambda[l], diff_subln[l][None, :], lam_init)
        km, vm = _memkv(mem[0], mem_norm[None, :], w_mk[l].astype(BF16), w_mv[l].astype(BF16))
        wr, br = _router_weights(w_group[l], b_group[l], w_router[l], b_router[l])
        x2, t, rec, cnt = _mixpost(xs, z, ya, yb, yc, yd, w_branch[l].astype(BF16), w_out[l].astype(BF16),
                                   norm_mem[l][None, :], km, vm, w_mq[l].astype(BF16), w_mo[l].astype(BF16),
                                   norm_ffn[l][None, :], wr, br)
        xs = _moe(x2, t, rec, cnt, w1, w3, w2, l, final_norm[None, :], final=(l == DEPTH - 1))
    return xs[None]
```

```python
import functools
import math

import numpy as np
import jax
import jax.numpy as jnp
from jax import lax
from jax.experimental import pallas as pl
from jax.experimental.pallas import tpu as pltpu

F32 = jnp.float32
BF16 = jnp.bfloat16

D_MODEL = 1024
DEPTH = 2
CHUNK = 64
NORM_EPS = 1e-6
GN_EPS = 1e-5
LRU_W = 512
LRU_BLOCKS = 8
LRU_BW = LRU_W // LRU_BLOCKS
CONV_W = 4
LRU_C = 8.0
CA_HEADS = 8
CA_HD = 64
CA_PREV = 8
REL_CLIP = 128
RET_HEADS = 8
RET_HD = 64
DIFF_HEADS = 4
DIFF_HD = 64
DIFF_VD = 2 * DIFF_HD
MX_HEADS = 4
MX_HD = 128
N_GROUPS = 4
EXP_PER_GROUP = 8
N_EXPERTS = N_GROUPS * EXP_PER_GROUP
D_EXPERT = 512
MOE_BLOCK = 256
N_BRANCH = 4
BRANCH_W = 512

COL_AX, COL_AG = 0, 512
COL_BQ, COL_BK, COL_BV = 1024, 1536, 2048
COL_CQ, COL_CK, COL_CV, COL_CG = 2560, 3072, 3584, 4096
COL_DQ, COL_DK, COL_DV = 4608, 5120, 5632
COL_GATE = 6144
IN_COLS = 10240

LANES = 128
NEG_BIG = -1e30
LOG2E = 1.0 / math.log(2.0)
VMEM_LIMIT = 56 * 1024 * 1024


def _cparams(n_axes):
    return pltpu.CompilerParams(dimension_semantics=("arbitrary",) * n_axes, vmem_limit_bytes=VMEM_LIMIT)


def _rms(x, g):
    return x * lax.rsqrt(jnp.mean(x * x, axis=-1, keepdims=True) + NORM_EPS) * g


def _dot(a, b):
    return jnp.dot(a, b, preferred_element_type=F32)


def _dot_nt(a, b):
    return lax.dot_general(a, b, (((1,), (1,)), ((), ())), preferred_element_type=F32)


def _dot_tn(a, b):
    return lax.dot_general(a, b, (((0,), (0,)), ((), ())), preferred_element_type=F32)


PACK_W = D_MODEL // 2


def _pack_rows(x):
    as_bits = lambda v: lax.bitcast_convert_type(v.astype(BF16).astype(F32), jnp.uint32)
    return (as_bits(x[:, :PACK_W]) >> 16) | (as_bits(x[:, PACK_W:]) & jnp.uint32(0xFFFF0000))


def _unpack_rows(w):
    lo = lax.bitcast_convert_type(w << 16, F32)
    hi = lax.bitcast_convert_type(w & jnp.uint32(0xFFFF0000), F32)
    return jnp.concatenate([lo, hi], axis=1)


def _inproj_kernel(x_ref, g_ref, w_ref, o_ref, vtb_ref, vtd_ref, h_scr):
    j = pl.program_id(1)

    @pl.when(j == 0)
    def _():
        h_scr[...] = _rms(x_ref[...], g_ref[...]).astype(BF16)

    tn = o_ref.shape[1]
    w = w_ref[:, pl.ds(pl.multiple_of(j * tn, tn), tn)]
    acc = _dot(h_scr[...], w)
    o_ref[...] = acc.astype(o_ref.dtype)
    for col, vt_ref in ((COL_BV, vtb_ref), (COL_DV, vtd_ref)):
        off = col % tn

        @pl.when(j == col // tn)
        def _():
            vt_ref[...] = acc[:, off:off + vt_ref.shape[0]].T.astype(vt_ref.dtype)


def _inproj(x, g, w, tm=1024, tn=2048):
    n = x.shape[0]
    tm = min(tm, n)
    vw = CA_HEADS * CA_HD
    assert vw == DIFF_HEADS * DIFF_VD and all(c % tn + vw <= tn for c in (COL_BV, COL_DV))
    vt_spec = pl.BlockSpec((vw, tm), lambda i, j: (0, i))
    vt_shape = jax.ShapeDtypeStruct((vw, n), BF16)
    return pl.pallas_call(
        _inproj_kernel,
        grid=(n // tm, IN_COLS // tn),
        in_specs=[
            pl.BlockSpec((tm, D_MODEL), lambda i, j: (i, 0)),
            pl.BlockSpec((1, D_MODEL), lambda i, j: (0, 0)),
            pl.BlockSpec((D_MODEL, IN_COLS), lambda i, j: (0, 0), pipeline_mode=pl.Buffered(1)),
        ],
        out_specs=[pl.BlockSpec((tm, tn), lambda i, j: (i, j)), vt_spec, vt_spec],
        out_shape=[jax.ShapeDtypeStruct((n, IN_COLS), BF16), vt_shape, vt_shape],
        scratch_shapes=[pltpu.VMEM((tm, D_MODEL), BF16)],
        compiler_params=_cparams(2),
        name="inproj",
    )(x, g, w)


def _gelu_tanh(x):
    return 0.5 * x * (1.0 + jnp.tanh(math.sqrt(2.0 / math.pi) * (x + 0.044715 * x * x * x)))


def _lru_kernel(ax_ref, ag_ref, cw_ref, cb_ref, w_ref, b_ref, lam_ref, o_ref, xbuf, a_scr, u_scr, h_scr, hc):
    t = ax_ref.shape[0]

    @pl.when(pl.program_id(0) == 0)
    def _():
        xbuf[0:8, :] = jnp.zeros((8, LRU_W), F32)
        hc[...] = jnp.zeros_like(hc)

    x = ax_ref[...].astype(F32)
    xbuf[8:8 + t, :] = x
    xc = (cb_ref[...] + cw_ref[3:4, :] * x + cw_ref[2:3, :] * xbuf[7:7 + t, :]
          + cw_ref[1:2, :] * xbuf[6:6 + t, :] + cw_ref[0:1, :] * xbuf[5:5 + t, :])
    xbuf[0:8, :] = xbuf[t:t + 8, :]
    ri = _dot(xc.astype(BF16), w_ref[...]) + b_ref[...]
    r = jax.nn.sigmoid(ri[:, :LRU_W])
    ig = jax.nn.sigmoid(ri[:, LRU_W:])
    nl = -lam_ref[...]
    softplus = jnp.maximum(nl, 0.0) + jnp.log(1.0 + jnp.exp(-jnp.abs(nl)))
    log_a = -LRU_C * r * softplus
    a = jnp.exp(log_a)
    a_scr[...] = a
    u_scr[...] = jnp.sqrt(-jnp.tanh(log_a) * (a * a + 1.0)) * (ig * xc)

    def body(s, h):
        h = a_scr[pl.ds(s, 1), :] * h + u_scr[pl.ds(s, 1), :]
        h_scr[pl.ds(s, 1), :] = h
        return h

    hc[0:1, :] = lax.fori_loop(0, t, body, hc[0:1, :], unroll=8)
    o_ref[...] = (h_scr[...] * _gelu_tanh(ag_ref[...].astype(F32))).astype(o_ref.dtype)


def _lru(z, cw, cb, wcat, bcat, lam, t=256):
    n = z.shape[0]
    t = min(t, n)
    full = lambda shape: pl.BlockSpec(shape, lambda i: (0,) * len(shape))
    return pl.pallas_call(
        _lru_kernel,
        grid=(n // t,),
        in_specs=[
            pl.BlockSpec((t, LRU_W), lambda i: (i, COL_AX // LRU_W)),
            pl.BlockSpec((t, LRU_W), lambda i: (i, COL_AG // LRU_W)),
            full((CONV_W, LRU_W)), full((1, LRU_W)), full((LRU_W, 2 * LRU_W)), full((1, 2 * LRU_W)), full((1, LRU_W)),
        ],
        out_specs=pl.BlockSpec((t, LRU_W), lambda i: (i, 0)),
        out_shape=jax.ShapeDtypeStruct((n, LRU_W), BF16),
        scratch_shapes=[pltpu.VMEM((t + 8, LRU_W), F32), pltpu.VMEM((t, LRU_W), F32), pltpu.VMEM((t, LRU_W), F32),
                        pltpu.VMEM((t, LRU_W), F32), pltpu.VMEM((8, LRU_W), F32)],
        compiler_params=_cparams(1),
        name="lru",
    )(z, z, cw, cb, wcat, bcat, lam)


CA_QB = 256
CA_KW = CA_PREV * CHUNK + CA_QB


CA_ONES = 16
CA_PAIR = 2 * CA_HD


def _chunkattn_kernel(q_ref, k0_ref, k1_ref, k2_ref, vt0_ref, vt1_ref, vt2_ref, bias_ref, o_ref):
    qt = (q_ref[...].astype(F32) * (LOG2E * CA_HD ** -0.5)).T
    k = jnp.concatenate([k0_ref[...], k1_ref[...], k2_ref[...]], axis=0)
    feat = lax.broadcasted_iota(jnp.int32, (CA_PAIR, CA_QB), 0)
    for hp in range(CA_HEADS // 2):
        lanes = slice(hp * CA_PAIR, (hp + 1) * CA_PAIR)
        qtp = qt[lanes, :]
        qq = jnp.concatenate([jnp.where(feat < CA_HD, qtp, 0.0), jnp.where(feat >= CA_HD, qtp, 0.0)], axis=1)
        t = _dot(k[:, lanes], qq.astype(BF16))
        t = t + jnp.concatenate([bias_ref[0, 2 * hp], bias_ref[0, 2 * hp + 1]], axis=1)
        p = jnp.exp2(t - jnp.max(t, axis=0, keepdims=True)).astype(BF16)
        vt = jnp.concatenate([vt0_ref[hp], vt1_ref[hp], vt2_ref[hp]], axis=1)
        vt = jnp.concatenate([vt, jnp.ones((CA_ONES, CA_KW), BF16)], axis=0)
        acc = _dot(vt, p)
        o1 = acc[0:CA_HD, 0:CA_QB] / acc[CA_PAIR:CA_PAIR + 1, 0:CA_QB]
        o2 = acc[CA_HD:CA_PAIR, CA_QB:] / acc[CA_PAIR:CA_PAIR + 1, CA_QB:]
        o_ref[:, lanes] = jnp.concatenate([o1, o2], axis=0).T.astype(o_ref.dtype)


def _chunk_bias(rel_table):
    span = CA_QB + CA_KW - 1
    n_hi = CA_KW - 1 - REL_CLIP
    n_lo = CA_QB - 1 - REL_CLIP
    e = jnp.concatenate([jnp.broadcast_to(rel_table[:, -1:], (CA_HEADS, n_hi)), rel_table[:, ::-1],
                         jnp.broadcast_to(rel_table[:, :1], (CA_HEADS, n_lo + 1))], axis=1)
    w = jnp.tile(e, (1, CA_QB))[:, :CA_QB * span].reshape(CA_HEADS, CA_QB, span)
    toeplitz = w[:, :, CA_QB - 1:CA_QB - 1 + CA_KW]
    cq = np.arange(CA_QB)[:, None] // CHUNK
    ck = np.arange(CA_KW)[None, :] // CHUNK
    band = (ck >= cq) & (ck <= cq + CA_PREV)
    first_valid = np.maximum((CA_KW - CA_QB) - CA_QB * np.arange(3), 0)
    ok = band[None] & (np.arange(CA_KW)[None, None, :] >= first_valid[:, None, None])
    bias = jnp.where(ok[:, None], (LOG2E * toeplitz.astype(F32))[None], NEG_BIG)
    return bias.transpose(0, 1, 3, 2)


def _chunkattn(z, vt, bias):
    n = z.shape[0]
    w = CA_HEADS * CA_HD
    nb = n // CA_QB
    n_pair = CA_HEADS // 2
    vt = vt.reshape(n_pair, CA_PAIR, n)
    kspec = lambda back: pl.BlockSpec((CA_QB, w), lambda i: (jnp.maximum(i - back, 0), COL_BK // w))
    vspec = lambda back: pl.BlockSpec((n_pair, CA_PAIR, CA_QB), lambda i: (0, 0, jnp.maximum(i - back, 0)))
    return pl.pallas_call(
        _chunkattn_kernel,
        grid=(nb,),
        in_specs=[
            pl.BlockSpec((CA_QB, w), lambda i: (i, COL_BQ // w)),
            kspec(2), kspec(1), kspec(0), vspec(2), vspec(1), vspec(0),
            pl.BlockSpec((1, CA_HEADS, CA_KW, CA_QB), lambda i: (jnp.minimum(i, 2), 0, 0, 0),
                         pipeline_mode=pl.Buffered(1)),
        ],
        out_specs=pl.BlockSpec((CA_QB, w), lambda i: (i, 0)),
        out_shape=jax.ShapeDtypeStruct((n, w), BF16),
        compiler_params=_cparams(1),
        name="chunkattn",
    )(z, z, z, z, vt, vt, vt, bias)


RET_T = 256
_RET_LOG_G = np.log(1.0 - 2.0 ** (-5.0 - np.arange(RET_HEADS)))


def _retention_consts(t):
    pos = np.arange(t)
    diff = pos[:, None] - pos[None, :]
    dmat = np.where(diff[None] >= 0, np.exp(_RET_LOG_G[:, None, None] * np.maximum(diff, 0)[None]), 0.0)
    dmat = dmat * (RET_HD ** -0.5)
    zeta = np.exp(_RET_LOG_G[None, :] * (t - 1 - pos)[:, None]) * (RET_HD ** -0.5)
    xi = np.exp(_RET_LOG_G[None, :] * (pos + 1)[:, None])
    rep = lambda m: np.repeat(m, RET_HD, axis=1)
    return (jnp.asarray(dmat, F32), jnp.asarray(rep(zeta), F32), jnp.asarray(rep(xi), F32))


def _retention_kernel(q_ref, k_ref, v_ref, g_ref, dmat_ref, zeta_ref, xi_ref, avg_ref, o_ref, s_scr):
    t = q_ref.shape[0]

    @pl.when(pl.program_id(0) == 0)
    def _():
        s_scr[...] = jnp.zeros_like(s_scr)

    q = q_ref[...]
    k = k_ref[...]
    v = v_ref[...]
    kz = (k.astype(F32) * zeta_ref[...]).astype(BF16)
    xi = xi_ref[...]
    g = g_ref[...].astype(F32)
    outs = []
    for h in range(RET_HEADS):
        sl = slice(h * RET_HD, (h + 1) * RET_HD)
        qh, vh = q[:, sl], v[:, sl]
        inner = _dot_nt(qh, k[:, sl]) * dmat_ref[h]
        state = s_scr[h]
        o = _dot(inner.astype(BF16), vh) + _dot(qh, state.astype(BF16)) * xi[:, sl]
        s_scr[h] = float(np.exp(_RET_LOG_G[h] * t)) * state + _dot_tn(kz[:, sl], vh)
        outs.append(o)
    o = jnp.concatenate(outs, axis=1)
    o_hi = o.astype(BF16)
    o_lo = (o - o_hi.astype(F32)).astype(BF16)
    d = o - (_dot(o_hi, avg_ref[...]) + _dot(o_lo, avg_ref[...]))
    var = _dot((d * d).astype(BF16), avg_ref[...])
    o_ref[...] = (jax.nn.silu(g) * d * lax.rsqrt(var + GN_EPS)).astype(o_ref.dtype)


def _retention(z):
    n = z.shape[0]
    t = min(RET_T, n)
    w = RET_HEADS * RET_HD
    dmat, zeta, xi = _retention_consts(t)
    avg = jnp.asarray(np.kron(np.eye(RET_HEADS), np.full((RET_HD, RET_HD), 1.0 / RET_HD)), BF16)
    col = lambda c: pl.BlockSpec((t, w), lambda i: (i, c // w))
    return pl.pallas_call(
        _retention_kernel,
        grid=(n // t,),
        in_specs=[col(COL_CQ), col(COL_CK), col(COL_CV), col(COL_CG),
                  pl.BlockSpec((RET_HEADS, t, t), lambda i: (0, 0, 0)),
                  pl.BlockSpec((t, w), lambda i: (0, 0)), pl.BlockSpec((t, w), lambda i: (0, 0)),
                  pl.BlockSpec((w, w), lambda i: (0, 0))],
        out_specs=pl.BlockSpec((t, w), lambda i: (i, 0)),
        out_shape=jax.ShapeDtypeStruct((n, w), BF16),
        scratch_shapes=[pltpu.VMEM((RET_HEADS, RET_HD, RET_HD), F32)],
        compiler_params=_cparams(1),
        name="retention",
    )(z, z, z, z, dmat, zeta, xi, avg)


DA_B = 256
DA_UNROLLS = (8, 4, 2)
DA_HG = 2
DA_ONES = 16
_DA_SLOPES = (2.0 ** (-8.0 * np.arange(1, DIFF_HEADS + 1) / DIFF_HEADS)).astype(np.float32)


def _bf16_parts(x, n):
    parts = []
    for _ in range(n):
        p = float(np.asarray(x, np.float32).astype(jnp.bfloat16).astype(np.float32))
        parts.append(p)
        x = x - p
    return parts


_LOG2E_PARTS = _bf16_parts(LOG2E, 3)
_DA_DIGITS = -(-(DA_B - 1).bit_length() // 8)


def _diffattn_consts():
    kloc = np.arange(DA_B)[:, None]
    qloc = (np.arange(2 * DA_B) % DA_B)[None, :]
    slopes = _DA_SLOPES[:, None, None].astype(np.float64)
    kfeat = np.zeros((DIFF_HEADS, DA_B, DIFF_VD))
    for d in range(_DA_DIGITS):
        cols = slice(d * len(_LOG2E_PARTS), (d + 1) * len(_LOG2E_PARTS))
        kfeat[:, :, cols] = slopes * (((kloc >> (8 * d)) & 255) << (8 * d))
    diag = np.where((kloc // CHUNK) <= (qloc // CHUNK), LOG2E * slopes * (qloc - np.abs(qloc - kloc) - kloc), NEG_BIG)
    return jnp.asarray(kfeat, BF16), jnp.asarray(diag, F32)


def _diffattn_kernel(slope_ref, q_ref, k_ref, vt_ref, kfeat_ref, bias_ref, lam_ref, g_ref, o_ref,
                     qq_scr, m_scr, acc_scr, s_scr, p_scr, a_scr, *, lam_init):
    i = pl.program_id(1)
    blk = DA_B
    heads = range(DA_HG)
    feat = lax.broadcasted_iota(jnp.int32, (DIFF_VD, blk), 0)
    for hh in heads:
        qt = (q_ref[:, hh * DIFF_VD:(hh + 1) * DIFF_VD].astype(F32) * (LOG2E * DIFF_HD ** -0.5)).T
        qq_scr[hh, 0:DIFF_VD, 0:blk] = jnp.where(feat < DIFF_HD, qt, 0.0).astype(BF16)
        qq_scr[hh, 0:DIFF_VD, blk:] = jnp.where(feat >= DIFF_HD, qt, 0.0).astype(BF16)

    @pl.when(i == 0)
    def _():
        frow = lax.broadcasted_iota(jnp.int32, (DIFF_VD, 2 * blk), 0)
        qfeat = jnp.zeros((DIFF_VD, 2 * blk), F32)
        for idx, part in enumerate(_LOG2E_PARTS * _DA_DIGITS):
            qfeat = jnp.where(frow == idx, part, qfeat)
        for hh in heads:
            qq_scr[hh, DIFF_VD:, :] = qfeat.astype(BF16)

    m_scr[...] = jnp.full_like(m_scr, NEG_BIG)
    acc_scr[...] = jnp.zeros_like(acc_scr)

    def scores(hh, j):
        k = k_ref[pl.ds(pl.multiple_of(j * blk, blk), blk), hh * DIFF_VD:(hh + 1) * DIFF_VD]
        return _dot(jnp.concatenate([k, kfeat_ref[hh]], axis=1), qq_scr[hh])

    def softmax_step(hh, j, slot, diag):
        slope = slope_ref[pl.program_id(0) * DA_HG + hh]
        offset = slope * LOG2E * (j * blk).astype(F32)
        t = s_scr[hh, slot]
        if diag:
            t = t + bias_ref[hh]
        m_old = m_scr[hh]
        m_new = jnp.maximum(m_old, jnp.max(t, axis=0, keepdims=True) + offset)
        m_scr[hh] = m_new
        p_scr[hh, slot] = jnp.exp2(t - (m_new - offset)).astype(BF16)
        a_scr[hh, slot] = jnp.exp2(m_old - m_new)

    def accumulate(hh, j, slot):
        vt = vt_ref[hh, :, pl.ds(pl.multiple_of(j * blk, blk), blk)]
        vt = jnp.concatenate([vt, jnp.ones((DA_ONES, blk), BF16)], axis=0)
        acc_scr[hh] = a_scr[hh, slot] * acc_scr[hh] + _dot(vt, p_scr[hh, slot])

    def stage(j, slot, last=False):
        for hh in heads:
            if not last:
                s_scr[hh, 1 - slot] = scores(hh, j + 1)
            accumulate(hh, jnp.maximum(j - 1, 0), 1 - slot)
            softmax_step(hh, j, slot, last)

    for hh in heads:
        s_scr[hh, 0] = scores(hh, 0)
        p_scr[hh, 1] = jnp.zeros((blk, 2 * blk), BF16)
        a_scr[hh, 1] = jnp.ones((1, 2 * blk), F32)

    def run(first, trips, unroll):
        def body(jj, c):
            for u in range(unroll):
                stage(first + unroll * jj + u, u % 2)
            return c

        lax.fori_loop(0, trips, body, 0)
        return first + unroll * trips

    done = 0
    for unroll in DA_UNROLLS:
        done = run(done, (i - done) // unroll, unroll)
    odd = i - done == 1

    @pl.when(odd)
    def _():
        stage(i - 1, 0)
        stage(i, 1, last=True)
        for hh in heads:
            accumulate(hh, i, 1)

    @pl.when(jnp.logical_not(odd))
    def _():
        stage(i, 0, last=True)
        for hh in heads:
            accumulate(hh, i, 0)

    lv = lam_ref[...]
    lam = (jnp.exp(jnp.sum(lv[0:1, :] * lv[1:2, :], axis=-1, keepdims=True))
           - jnp.exp(jnp.sum(lv[2:3, :] * lv[3:4, :], axis=-1, keepdims=True)) + lam_init)
    for hh in heads:
        o = acc_scr[hh, 0:DIFF_VD, :] / acc_scr[hh, DIFF_VD:DIFF_VD + 1, :]
        od = o[:, 0:blk] - lam * o[:, blk:]
        od = od * lax.rsqrt(jnp.mean(od * od, axis=0, keepdims=True) + GN_EPS)
        o_ref[:, hh * DIFF_VD:(hh + 1) * DIFF_VD] = (od.T * g_ref[...] * (1.0 - lam_init)).astype(o_ref.dtype)


def _diffattn(z, vt, lam_vecs, subln_g, lam_init):
    n = z.shape[0]
    assert n % DA_B == 0 and DIFF_HEADS % DA_HG == 0
    w = DIFF_VD
    gw = DA_HG * w
    kfeat, bias = _diffattn_consts()
    vt = vt.reshape(DIFF_HEADS, w, n)
    return pl.pallas_call(
        functools.partial(_diffattn_kernel, lam_init=lam_init),
        grid_spec=pltpu.PrefetchScalarGridSpec(
            num_scalar_prefetch=1,
            grid=(DIFF_HEADS // DA_HG, n // DA_B),
            in_specs=[
                pl.BlockSpec((DA_B, gw), lambda h, i, s: (i, COL_DQ // gw + h)),
                pl.BlockSpec((n, gw), lambda h, i, s: (0, COL_DK // gw + h)),
                pl.BlockSpec((DA_HG, w, n), lambda h, i, s: (h, 0, 0)),
                pl.BlockSpec((DA_HG, DA_B, w), lambda h, i, s: (h, 0, 0)),
                pl.BlockSpec((DA_HG, DA_B, 2 * DA_B), lambda h, i, s: (h, 0, 0)),
                pl.BlockSpec((4, DIFF_HD), lambda h, i, s: (0, 0)),
                pl.BlockSpec((1, w), lambda h, i, s: (0, 0)),
            ],
            out_specs=pl.BlockSpec((DA_B, gw), lambda h, i, s: (i, h)),
            scratch_shapes=[pltpu.VMEM((DA_HG, 2 * w, 2 * DA_B), BF16), pltpu.VMEM((DA_HG, 1, 2 * DA_B), F32),
                            pltpu.VMEM((DA_HG, w + DA_ONES, 2 * DA_B), F32),
                            pltpu.VMEM((DA_HG, 2, DA_B, 2 * DA_B), F32), pltpu.VMEM((DA_HG, 2, DA_B, 2 * DA_B), BF16),
                            pltpu.VMEM((DA_HG, 2, 1, 2 * DA_B), F32)],
        ),
        out_shape=jax.ShapeDtypeStruct((n, DIFF_HEADS * w), BF16),
        compiler_params=_cparams(2),
        name="diffattn",
    )(jnp.asarray(_DA_SLOPES), z, z, vt, kfeat, bias, lam_vecs, subln_g)


def _memkv_kernel(mem_ref, g_ref, wk_ref, wv_ref, k_ref, v_ref):
    mn = _rms(mem_ref[...], g_ref[...]).astype(BF16)
    k_ref[...] = _dot(mn, wk_ref[...]).astype(BF16)
    v_ref[...] = _dot(mn, wv_ref[...]).astype(BF16)


def _memkv(mem, g, wk, wv):
    m = mem.shape[0]
    w = MX_HEADS * MX_HD
    out = jax.ShapeDtypeStruct((m, w), BF16)
    return pl.pallas_call(_memkv_kernel, out_shape=(out, out), name="memkv",
                          compiler_params=pltpu.CompilerParams(vmem_limit_bytes=VMEM_LIMIT))(mem, g, wk, wv)


ROUTE_LANES = LANES


def _route(logits, carry):
    t = logits.shape[0]
    lane_i = lax.broadcasted_iota(jnp.int32, (t, ROUTE_LANES), 1)
    lane = lane_i.astype(F32)
    big = float(ROUTE_LANES)
    gl = jnp.where(lane_i < N_GROUPS, logits, NEG_BIG)
    gmax = jnp.max(gl, axis=-1, keepdims=True)
    gsum = jnp.sum(jnp.exp(gl - gmax), axis=-1, keepdims=True)
    g_sel = jnp.min(jnp.where(gl == gmax, lane, big), axis=-1, keepdims=True)
    g_prob = 1.0 / gsum
    lo = N_GROUPS + EXP_PER_GROUP * g_sel
    el = jnp.where((lane >= lo) & (lane < lo + EXP_PER_GROUP), logits, NEG_BIG)
    e1 = jnp.max(el, axis=-1, keepdims=True)
    i1 = jnp.min(jnp.where(el == e1, lane, big), axis=-1, keepdims=True)
    el2 = jnp.where(lane == i1, NEG_BIG, el)
    e2 = jnp.max(el2, axis=-1, keepdims=True)
    i2 = jnp.min(jnp.where(el2 == e2, lane, big), axis=-1, keepdims=True)
    esum = jnp.sum(jnp.exp(el - e1), axis=-1, keepdims=True)
    p1 = 1.0 / esum
    p2 = jnp.exp(e2 - e1) / esum
    w1 = p1 / (p1 + p2) * g_prob
    w2 = p2 / (p1 + p2) * g_prob
    hot1 = lane == i1
    hot2 = lane == i2
    cnt = jnp.where(hot1 | hot2, 1.0, 0.0)
    r = lax.broadcasted_iota(jnp.int32, (t, t), 0)
    c = lax.broadcasted_iota(jnp.int32, (t, t), 1)
    before = jnp.where(c < r, 1.0, 0.0).astype(BF16)
    prefix = _dot(before, cnt.astype(BF16)) + carry
    rank1 = jnp.sum(jnp.where(hot1, prefix, 0.0), axis=-1, keepdims=True)
    rank2 = jnp.sum(jnp.where(hot2, prefix, 0.0), axis=-1, keepdims=True)
    rec = jnp.zeros((t, ROUTE_LANES), F32)
    for idx, val in enumerate((i1 - N_GROUPS, i2 - N_GROUPS, w1, w2, rank1, rank2)):
        rec = jnp.where(lane_i == idx, val, rec)
    return rec, carry + jnp.sum(cnt, axis=0, keepdims=True)


POST_SUB = 256


def _mixpost_kernel(x_ref, ya_ref, yb_ref, yc_ref, yd_ref, g0_ref, g1_ref, g2_ref, g3_ref, wb_ref, wo_ref,
                    gm_ref, km_ref, vm_ref, wq_ref, wmo_ref, gf_ref, wrh_ref, wrl_ref, br_ref,
                    xo_ref, t_ref, rec_ref, cnt_ref, carry):
    @pl.when(pl.program_id(0) == 0)
    def _():
        carry[...] = jnp.zeros_like(carry)

    km = km_ref[...]
    vm = vm_ref[...]
    logits = []
    for sub in range(x_ref.shape[0] // POST_SUB):
        rows = slice(sub * POST_SUB, (sub + 1) * POST_SUB)
        mixed = None
        for y_ref, g_ref, b in ((ya_ref, g0_ref, 0), (yb_ref, g1_ref, 1), (yc_ref, g2_ref, 2), (yd_ref, g3_ref, 3)):
            term = jax.nn.sigmoid(g_ref[rows, :].astype(F32)) * _dot(y_ref[rows, :], wb_ref[b])
            mixed = term if mixed is None else mixed + term
        x = x_ref[rows, :] + _dot(mixed.astype(BF16), wo_ref[...])
        q = _dot(_rms(x, gm_ref[...]).astype(BF16), wq_ref[...]).astype(BF16)
        outs = []
        for h in range(MX_HEADS):
            sl = slice(h * MX_HD, (h + 1) * MX_HD)
            s = _dot_nt(q[:, sl], km[:, sl]) * (MX_HD ** -0.5)
            p = jnp.exp(s - jnp.max(s, axis=-1, keepdims=True))
            p = p / jnp.sum(p, axis=-1, keepdims=True)
            outs.append(_dot(p.astype(BF16), vm[:, sl]))
        x = x + _dot(jnp.concatenate(outs, axis=1).astype(BF16), wmo_ref[...])
        xo_ref[rows, :] = x
        t = _rms(x, gf_ref[...])
        t_ref[rows, :] = _pack_rows(t)
        t_hi = t.astype(BF16)
        t_lo = (t - t_hi.astype(F32)).astype(BF16)
        logits.append(_dot(t_hi, wrh_ref[...]) + (_dot(t_lo, wrh_ref[...]) + _dot(t_hi, wrl_ref[...])) + br_ref[...])
    running = carry[0:1, :]
    for sub, lg in enumerate(logits):
        rec, running = _route(lg, running)
        rec_ref[sub * POST_SUB:(sub + 1) * POST_SUB, :] = rec
    carry[0:1, :] = running
    cnt_ref[...] = jnp.broadcast_to(running, cnt_ref.shape)


def _mixpost(x, z, ya, yb, yc, yd, wb, wo, gm, km, vm, wq, wmo, gf, wr, br, tm=512):
    n = x.shape[0]
    tm = min(tm, n)
    assert tm % POST_SUB == 0
    w = MX_HEADS * MX_HD
    m = km.shape[0]
    wr_hi = wr.astype(BF16)
    wr_lo = (wr - wr_hi.astype(F32)).astype(BF16)
    row = lambda wd: pl.BlockSpec((tm, wd), lambda i: (i, 0))
    gate = lambda b: pl.BlockSpec((tm, D_MODEL), lambda i: (i, COL_GATE // D_MODEL + b))
    full = lambda shape: pl.BlockSpec(shape, lambda i: (0,) * len(shape))
    return pl.pallas_call(
        _mixpost_kernel,
        grid=(n // tm,),
        in_specs=[row(D_MODEL), row(BRANCH_W), row(BRANCH_W), row(BRANCH_W), row(BRANCH_W),
                  gate(0), gate(1), gate(2), gate(3),
                  full((N_BRANCH, BRANCH_W, D_MODEL)), full((D_MODEL, D_MODEL)),
                  full((1, D_MODEL)), full((m, w)), full((m, w)), full((D_MODEL, w)), full((w, D_MODEL)),
                  full((1, D_MODEL)), full((D_MODEL, ROUTE_LANES)), full((D_MODEL, ROUTE_LANES)),
                  full((1, ROUTE_LANES))],
        out_specs=[row(D_MODEL), row(PACK_W), row(ROUTE_LANES), full((8, ROUTE_LANES))],
        out_shape=[jax.ShapeDtypeStruct((n, D_MODEL), F32), jax.ShapeDtypeStruct((n, PACK_W), jnp.uint32),
                   jax.ShapeDtypeStruct((n, ROUTE_LANES), F32), jax.ShapeDtypeStruct((8, ROUTE_LANES), F32)],
        scratch_shapes=[pltpu.VMEM((8, ROUTE_LANES), F32)],
        compiler_params=_cparams(1),
        name="mixpost",
    )(x, ya, yb, yc, yd, z, z, z, z, wb, wo, gm, km, vm, wq, wmo, gf, wr_hi, wr_lo, br)


MOE_TB = 256


def _row_copy(src, src_row, dst, dst_row, sem):
    return pltpu.make_async_copy(src.at[pl.ds(src_row, 1), :], dst.at[pl.ds(dst_row, 1), :], sem)


def _dispatch_kernel(dest_ref, t_ref, xs_in_ref, xs_ref, sem):
    del xs_in_ref
    base = pl.program_id(0) * MOE_TB

    def issue(r, c):
        for k in range(2):
            _row_copy(t_ref, r, xs_ref, dest_ref[2 * (base + r) + k], sem).start(priority=k)
        return c

    lax.fori_loop(0, MOE_TB, issue, 0, unroll=8)
    for k in range(2):
        pltpu.make_async_copy(t_ref, xs_ref.at[pl.ds(0, MOE_TB), :], sem).wait()


def _dispatch(dest, t, n_slots):
    n = t.shape[0]
    xs0 = jnp.zeros((n_slots, PACK_W), jnp.uint32)
    return pl.pallas_call(
        _dispatch_kernel,
        grid_spec=pltpu.PrefetchScalarGridSpec(
            num_scalar_prefetch=1,
            grid=(n // MOE_TB,),
            in_specs=[pl.BlockSpec((MOE_TB, PACK_W), lambda i, d: (i, 0)), pl.BlockSpec(memory_space=pl.ANY)],
            out_specs=pl.BlockSpec(memory_space=pl.ANY),
            scratch_shapes=[pltpu.SemaphoreType.DMA(())],
        ),
        out_shape=jax.ShapeDtypeStruct((n_slots, PACK_W), jnp.uint32),
        input_output_aliases={2: 0},
        compiler_params=_cparams(1),
        name="dispatch",
    )(dest, t, xs0)


def _expert_kernel(be_ref, na_ref, xs_ref, w1_ref, w3_ref, w2_ref, ys_ref, w1b, w3b, w2b):
    i = pl.program_id(0)
    active = i < na_ref[0]
    new_expert = (i == 0) | (be_ref[i] != be_ref[jnp.maximum(i - 1, 0)])

    @pl.when(active & new_expert)
    def _():
        w1b[...] = w1_ref[...].astype(BF16)
        w3b[...] = w3_ref[...].astype(BF16)
        w2b[...] = w2_ref[...].astype(BF16)

    @pl.when(active)
    def _():
        x = _unpack_rows(xs_ref[...]).astype(BF16)
        a = _dot(x, w1b[...])
        b = _dot(x, w3b[...])
        ys_ref[...] = _pack_rows(_dot((jax.nn.silu(a) * b).astype(BF16), w2b[...]))

    @pl.when(jnp.logical_not(active))
    def _():
        ys_ref[...] = jnp.zeros_like(ys_ref)


def _experts(block_e, n_active, xs, w1, w3, w2, layer):
    n_blocks = xs.shape[0] // MOE_BLOCK
    blk = lambda i, na: jnp.minimum(i, jnp.maximum(na[0] - 1, 0))
    wspec = lambda rows, cols: pl.BlockSpec((None, None, rows, cols), lambda i, be, na: (layer, be[blk(i, na)], 0, 0))
    return pl.pallas_call(
        _expert_kernel,
        grid_spec=pltpu.PrefetchScalarGridSpec(
            num_scalar_prefetch=2,
            grid=(n_blocks,),
            in_specs=[
                pl.BlockSpec((MOE_BLOCK, PACK_W), lambda i, be, na: (blk(i, na), 0)),
                wspec(D_MODEL, D_EXPERT), wspec(D_MODEL, D_EXPERT), wspec(D_EXPERT, D_MODEL),
            ],
            out_specs=pl.BlockSpec((MOE_BLOCK, PACK_W), lambda i, be, na: (i, 0)),
            scratch_shapes=[pltpu.VMEM((D_MODEL, D_EXPERT), BF16), pltpu.VMEM((D_MODEL, D_EXPERT), BF16),
                            pltpu.VMEM((D_EXPERT, D_MODEL), BF16)],
        ),
        out_shape=jax.ShapeDtypeStruct(xs.shape, jnp.uint32),
        compiler_params=_cparams(1),
        name="experts",
    )(block_e, n_active, xs, w1, w3, w2)


def _combine_kernel(dest_ref, x_ref, rec_ref, g_ref, ys_ref, o_ref, rows, sem, *, final):
    base = pl.program_id(0) * MOE_TB

    def issue(r, c):
        for k in range(2):
            _row_copy(ys_ref, dest_ref[2 * (base + r) + k], rows.at[k], r, sem).start(priority=k)
        return c

    lax.fori_loop(0, MOE_TB, issue, 0, unroll=8)
    for k in range(2):
        pltpu.make_async_copy(ys_ref.at[pl.ds(0, MOE_TB), :], rows.at[k], sem).wait()
    rec = rec_ref[...]
    x = x_ref[...] + rec[:, 2:3] * _unpack_rows(rows[0]) + rec[:, 3:4] * _unpack_rows(rows[1])
    o_ref[...] = _rms(x, g_ref[...]) if final else x


def _combine(dest, x, rec, ys, g, final):
    n = x.shape[0]
    row = lambda wd: pl.BlockSpec((MOE_TB, wd), lambda i, d: (i, 0))
    return pl.pallas_call(
        functools.partial(_combine_kernel, final=final),
        grid_spec=pltpu.PrefetchScalarGridSpec(
            num_scalar_prefetch=1,
            grid=(n // MOE_TB,),
            in_specs=[row(D_MODEL), row(ROUTE_LANES), pl.BlockSpec((1, D_MODEL), lambda i, d: (0, 0)),
                      pl.BlockSpec(memory_space=pl.ANY)],
            out_specs=row(D_MODEL),
            scratch_shapes=[pltpu.VMEM((2, MOE_TB, PACK_W), jnp.uint32), pltpu.SemaphoreType.DMA(())],
        ),
        out_shape=jax.ShapeDtypeStruct((n, D_MODEL), F32),
        compiler_params=_cparams(1),
        name="combine",
    )(dest, x, rec, g, ys)


def _moe(x, t, rec, cnt, w1, w3, w2, layer, g_final, final):
    n = x.shape[0]
    n_asg = 2 * n
    n_blocks = -(-(n_asg + N_EXPERTS * (MOE_BLOCK - 1)) // MOE_BLOCK)
    counts = cnt[0, N_GROUPS:N_GROUPS + N_EXPERTS].astype(jnp.int32)
    padded = (counts + MOE_BLOCK - 1) // MOE_BLOCK * MOE_BLOCK
    pad_end = jnp.cumsum(padded)
    pad_start = pad_end - padded
    expert = rec[:, 0:2].astype(jnp.int32)
    first = jnp.sum(jnp.where(expert[..., None] == jnp.arange(N_EXPERTS), pad_start, 0), axis=-1)
    dest = (first + rec[:, 4:6].astype(jnp.int32)).reshape(-1)
    starts = jnp.arange(n_blocks, dtype=jnp.int32) * MOE_BLOCK
    block_e = jnp.minimum(jnp.sum(pad_end[None, :] <= starts[:, None], axis=1), N_EXPERTS - 1).astype(jnp.int32)
    n_active = (pad_end[-1:] // MOE_BLOCK).astype(jnp.int32)
    xs = _dispatch(dest, t, n_blocks * MOE_BLOCK)
    ys = _experts(block_e, n_active, xs, w1, w3, w2, layer)
    return _combine(dest, x, rec, ys, g_final, final)


def _block_diag(w):
    eye = jnp.eye(LRU_BLOCKS, dtype=w.dtype)
    return jnp.einsum("kcd,kj->kcjd", w, eye).reshape(LRU_W, LRU_W)


def _router_weights(w_group, b_group, w_router, b_router):
    pad = ROUTE_LANES - N_GROUPS - N_EXPERTS
    wr = jnp.concatenate([w_group, w_router, jnp.zeros((D_MODEL, pad), F32)], axis=1)
    br = jnp.concatenate([b_group, b_router, jnp.zeros((pad,), F32)])[None, :]
    return wr, br


def kernel(x, mem, norm_mix, w_in, conv_w, conv_b, lru_wa, lru_ba, lru_wx, lru_bx, lru_lambda, ca_rel_bias, diff_lambda, diff_subln, w_branch, w_out, norm_mem, mem_norm, w_mq, w_mk, w_mv, w_mo, norm_ffn, w_group, b_group, w_router, b_router, w1, w3, w2, final_norm):
    b, n, d = x.shape
    assert b == 1 and d == D_MODEL
    xs = x[0]
    for l in range(DEPTH):
        lam_init = 0.8 - 0.6 * math.exp(-0.3 * l)
        z, vt_b, vt_d = _inproj(xs, norm_mix[l][None, :], w_in[l].astype(BF16))
        wcat = jnp.concatenate([_block_diag(lru_wa[l]), _block_diag(lru_wx[l])], axis=1).astype(BF16)
        bcat = jnp.concatenate([lru_ba[l], lru_bx[l]])[None, :]
        ya = _lru(z, conv_w[l], conv_b[l][None, :], wcat, bcat, lru_lambda[l][None, :])
        yb = _chunkattn(z, vt_b, _chunk_bias(ca_rel_bias[l]))
        yc = _retention(z)
        yd = _diffattn(z, vt_d, diff_lambda[l], diff_subln[l][None, :], lam_init)
        km, vm = _memkv(mem[0], mem_norm[None, :], w_mk[l].astype(BF16), w_mv[l].astype(BF16))
        wr, br = _router_weights(w_group[l], b_group[l], w_router[l], b_router[l])
        x2, t, rec, cnt = _mixpost(xs, z, ya, yb, yc, yd, w_branch[l].astype(BF16), w_out[l].astype(BF16),
                                   norm_mem[l][None, :], km, vm, w_mq[l].astype(BF16), w_mo[l].astype(BF16),
                                   norm_ffn[l][None, :], wr, br)
        xs = _moe(x2, t, rec, cnt, w1, w3, w2, l, final_norm[None, :], final=(l == DEPTH - 1))
    return xs[None]
```

```python
import functools
import math

import numpy as np
import jax
import jax.numpy as jnp
from jax import lax
from jax.experimental import pallas as pl
from jax.experimental.pallas import tpu as pltpu

F32 = jnp.float32
BF16 = jnp.bfloat16

D_MODEL = 1024
DEPTH = 2
CHUNK = 64
NORM_EPS = 1e-6
GN_EPS = 1e-5
LRU_W = 512
LRU_BLOCKS = 8
LRU_BW = LRU_W // LRU_BLOCKS
CONV_W = 4
LRU_C = 8.0
CA_HEADS = 8
CA_HD = 64
CA_PREV = 8
REL_CLIP = 128
RET_HEADS = 8
RET_HD = 64
DIFF_HEADS = 4
DIFF_HD = 64
DIFF_VD = 2 * DIFF_HD
MX_HEADS = 4
MX_HD = 128
N_GROUPS = 4
EXP_PER_GROUP = 8
N_EXPERTS = N_GROUPS * EXP_PER_GROUP
D_EXPERT = 512
MOE_BLOCK = 256
N_BRANCH = 4
BRANCH_W = 512

COL_AX, COL_AG = 0, 512
COL_BQ, COL_BK, COL_BV = 1024, 1536, 2048
COL_CQ, COL_CK, COL_CV, COL_CG = 2560, 3072, 3584, 4096
COL_DQ, COL_DK, COL_DV = 4608, 5120, 5632
COL_GATE = 6144
IN_COLS = 10240

LANES = 128
NEG_BIG = -1e30
LOG2E = 1.0 / math.log(2.0)
VMEM_LIMIT = 56 * 1024 * 1024


def _cparams(n_axes):
    return pltpu.CompilerParams(dimension_semantics=("arbitrary",) * n_axes, vmem_limit_bytes=VMEM_LIMIT)


def _rms(x, g):
    return x * lax.rsqrt(jnp.mean(x * x, axis=-1, keepdims=True) + NORM_EPS) * g


def _dot(a, b):
    return jnp.dot(a, b, preferred_element_type=F32)


def _dot_nt(a, b):
    return lax.dot_general(a, b, (((1,), (1,)), ((), ())), preferred_element_type=F32)


def _dot_tn(a, b):
    return lax.dot_general(a, b, (((0,), (0,)), ((), ())), preferred_element_type=F32)


PACK_W = D_MODEL // 2


def _pack_rows(x):
    as_bits = lambda v: lax.bitcast_convert_type(v.astype(BF16).astype(F32), jnp.uint32)
    return (as_bits(x[:, :PACK_W]) >> 16) | (as_bits(x[:, PACK_W:]) & jnp.uint32(0xFFFF0000))


def _unpack_rows(w):
    lo = lax.bitcast_convert_type(w << 16, F32)
    hi = lax.bitcast_convert_type(w & jnp.uint32(0xFFFF0000), F32)
    return jnp.concatenate([lo, hi], axis=1)


def _inproj_kernel(x_ref, g_ref, w_ref, o_ref, vtb_ref, vtd_ref, h_scr):
    j = pl.program_id(1)

    @pl.when(j == 0)
    def _():
        h_scr[...] = _rms(x_ref[...], g_ref[...]).astype(BF16)

    tn = o_ref.shape[1]
    w = w_ref[:, pl.ds(pl.multiple_of(j * tn, tn), tn)]
    acc = _dot(h_scr[...], w)
    o_ref[...] = acc.astype(o_ref.dtype)
    for col, vt_ref in ((COL_BV, vtb_ref), (COL_DV, vtd_ref)):
        off = col % tn

        @pl.when(j == col // tn)
        def _():
            vt_ref[...] = acc[:, off:off + vt_ref.shape[0]].T.astype(vt_ref.dtype)


def _inproj(x, g, w, tm=1024, tn=2048):
    n = x.shape[0]
    tm = min(tm, n)
    vw = CA_HEADS * CA_HD
    assert vw == DIFF_HEADS * DIFF_VD and all(c % tn + vw <= tn for c in (COL_BV, COL_DV))
    vt_spec = pl.BlockSpec((vw, tm), lambda i, j: (0, i))
    vt_shape = jax.ShapeDtypeStruct((vw, n), BF16)
    return pl.pallas_call(
        _inproj_kernel,
        grid=(n // tm, IN_COLS // tn),
        in_specs=[
            pl.BlockSpec((tm, D_MODEL), lambda i, j: (i, 0)),
            pl.BlockSpec((1, D_MODEL), lambda i, j: (0, 0)),
            pl.BlockSpec((D_MODEL, IN_COLS), lambda i, j: (0, 0), pipeline_mode=pl.Buffered(1)),
        ],
        out_specs=[pl.BlockSpec((tm, tn), lambda i, j: (i, j)), vt_spec, vt_spec],
        out_shape=[jax.ShapeDtypeStruct((n, IN_COLS), BF16), vt_shape, vt_shape],
        scratch_shapes=[pltpu.VMEM((tm, D_MODEL), BF16)],
        compiler_params=_cparams(2),
        name="inproj",
    )(x, g, w)


def _gelu_tanh(x):
    return 0.5 * x * (1.0 + jnp.tanh(math.sqrt(2.0 / math.pi) * (x + 0.044715 * x * x * x)))


def _lru_kernel(ax_ref, ag_ref, cw_ref, cb_ref, w_ref, b_ref, lam_ref, o_ref, xbuf, a_scr, u_scr, h_scr, hc):
    t = ax_ref.shape[0]

    @pl.when(pl.program_id(0) == 0)
    def _():
        xbuf[0:8, :] = jnp.zeros((8, LRU_W), F32)
        hc[...] = jnp.zeros_like(hc)

    x = ax_ref[...].astype(F32)
    xbuf[8:8 + t, :] = x
    xc = (cb_ref[...] + cw_ref[3:4, :] * x + cw_ref[2:3, :] * xbuf[7:7 + t, :]
          + cw_ref[1:2, :] * xbuf[6:6 + t, :] + cw_ref[0:1, :] * xbuf[5:5 + t, :])
    xbuf[0:8, :] = xbuf[t:t + 8, :]
    ri = _dot(xc.astype(BF16), w_ref[...]) + b_ref[...]
    r = jax.nn.sigmoid(ri[:, :LRU_W])
    ig = jax.nn.sigmoid(ri[:, LRU_W:])
    nl = -lam_ref[...]
    softplus = jnp.maximum(nl, 0.0) + jnp.log(1.0 + jnp.exp(-jnp.abs(nl)))
    log_a = -LRU_C * r * softplus
    a = jnp.exp(log_a)
    a_scr[...] = a
    u_scr[...] = jnp.sqrt(-jnp.tanh(log_a) * (a * a + 1.0)) * (ig * xc)

    def body(s, h):
        h = a_scr[pl.ds(s, 1), :] * h + u_scr[pl.ds(s, 1), :]
        h_scr[pl.ds(s, 1), :] = h
        return h

    hc[0:1, :] = lax.fori_loop(0, t, body, hc[0:1, :], unroll=8)
    o_ref[...] = (h_scr[...] * _gelu_tanh(ag_ref[...].astype(F32))).astype(o_ref.dtype)


def _lru(z, cw, cb, wcat, bcat, lam, t=256):
    n = z.shape[0]
    t = min(t, n)
    full = lambda shape: pl.BlockSpec(shape, lambda i: (0,) * len(shape))
    return pl.pallas_call(
        _lru_kernel,
        grid=(n // t,),
        in_specs=[
            pl.BlockSpec((t, LRU_W), lambda i: (i, COL_AX // LRU_W)),
            pl.BlockSpec((t, LRU_W), lambda i: (i, COL_AG // LRU_W)),
            full((CONV_W, LRU_W)), full((1, LRU_W)), full((LRU_W, 2 * LRU_W)), full((1, 2 * LRU_W)), full((1, LRU_W)),
        ],
        out_specs=pl.BlockSpec((t, LRU_W), lambda i: (i, 0)),
        out_shape=jax.ShapeDtypeStruct((n, LRU_W), BF16),
        scratch_shapes=[pltpu.VMEM((t + 8, LRU_W), F32), pltpu.VMEM((t, LRU_W), F32), pltpu.VMEM((t, LRU_W), F32),
                        pltpu.VMEM((t, LRU_W), F32), pltpu.VMEM((8, LRU_W), F32)],
        compiler_params=_cparams(1),
        name="lru",
    )(z, z, cw, cb, wcat, bcat, lam)


CA_QB = 256
CA_KW = CA_PREV * CHUNK + CA_QB


CA_ONES = 16
CA_PAIR = 2 * CA_HD


def _chunkattn_kernel(q_ref, k0_ref, k1_ref, k2_ref, vt0_ref, vt1_ref, vt2_ref, bias_ref, o_ref):
    qt = (q_ref[...].astype(F32) * (LOG2E * CA_HD ** -0.5)).T
    k = jnp.concatenate([k0_ref[...], k1_ref[...], k2_ref[...]], axis=0)
    feat = lax.broadcasted_iota(jnp.int32, (CA_PAIR, CA_QB), 0)
    for hp in range(CA_HEADS // 2):
        lanes = slice(hp * CA_PAIR, (hp + 1) * CA_PAIR)
        qtp = qt[lanes, :]
        qq = jnp.concatenate([jnp.where(feat < CA_HD, qtp, 0.0), jnp.where(feat >= CA_HD, qtp, 0.0)], axis=1)
        t = _dot(k[:, lanes], qq.astype(BF16))
        t = t + jnp.concatenate([bias_ref[0, 2 * hp], bias_ref[0, 2 * hp + 1]], axis=1)
        p = jnp.exp2(t - jnp.max(t, axis=0, keepdims=True)).astype(BF16)
        vt = jnp.concatenate([vt0_ref[hp], vt1_ref[hp], vt2_ref[hp]], axis=1)
        vt = jnp.concatenate([vt, jnp.ones((CA_ONES, CA_KW), BF16)], axis=0)
        acc = _dot(vt, p)
        o1 = acc[0:CA_HD, 0:CA_QB] / acc[CA_PAIR:CA_PAIR + 1, 0:CA_QB]
        o2 = acc[CA_HD:CA_PAIR, CA_QB:] / acc[CA_PAIR:CA_PAIR + 1, CA_QB:]
        o_ref[:, lanes] = jnp.concatenate([o1, o2], axis=0).T.astype(o_ref.dtype)


def _chunk_bias(rel_table):
    span = CA_QB + CA_KW - 1
    n_hi = CA_KW - 1 - REL_CLIP
    n_lo = CA_QB - 1 - REL_CLIP
    e = jnp.concatenate([jnp.broadcast_to(rel_table[:, -1:], (CA_HEADS, n_hi)), rel_table[:, ::-1],
                         jnp.broadcast_to(rel_table[:, :1], (CA_HEADS, n_lo + 1))], axis=1)
    w = jnp.tile(e, (1, CA_QB))[:, :CA_QB * span].reshape(CA_HEADS, CA_QB, span)
    toeplitz = w[:, :, CA_QB - 1:CA_QB - 1 + CA_KW]
    cq = np.arange(CA_QB)[:, None] // CHUNK
    ck = np.arange(CA_KW)[None, :] // CHUNK
    band = (ck >= cq) & (ck <= cq + CA_PREV)
    first_valid = np.maximum((CA_KW - CA_QB) - CA_QB * np.arange(3), 0)
    ok = band[None] & (np.arange(CA_KW)[None, None, :] >= first_valid[:, None, None])
    bias = jnp.where(ok[:, None], (LOG2E * toeplitz.astype(F32))[None], NEG_BIG)
    return bias.transpose(0, 1, 3, 2)


def _chunkattn(z, vt, bias):
    n = z.shape[0]
    w = CA_HEADS * CA_HD
    nb = n // CA_QB
    n_pair = CA_HEADS // 2
    vt = vt.reshape(n_pair, CA_PAIR, n)
    kspec = lambda back: pl.BlockSpec((CA_QB, w), lambda i: (jnp.maximum(i - back, 0), COL_BK // w))
    vspec = lambda back: pl.BlockSpec((n_pair, CA_PAIR, CA_QB), lambda i: (0, 0, jnp.maximum(i - back, 0)))
    return pl.pallas_call(
        _chunkattn_kernel,
        grid=(nb,),
        in_specs=[
            pl.BlockSpec((CA_QB, w), lambda i: (i, COL_BQ // w)),
            kspec(2), kspec(1), kspec(0), vspec(2), vspec(1), vspec(0),
            pl.BlockSpec((1, CA_HEADS, CA_KW, CA_QB), lambda i: (jnp.minimum(i, 2), 0, 0, 0),
                         pipeline_mode=pl.Buffered(1)),
        ],
        out_specs=pl.BlockSpec((CA_QB, w), lambda i: (i, 0)),
        out_shape=jax.ShapeDtypeStruct((n, w), BF16),
        compiler_params=_cparams(1),
        name="chunkattn",
    )(z, z, z, z, vt, vt, vt, bias)


RET_T = 256
_RET_LOG_G = np.log(1.0 - 2.0 ** (-5.0 - np.arange(RET_HEADS)))


def _retention_consts(t):
    pos = np.arange(t)
    diff = pos[:, None] - pos[None, :]
    dmat = np.where(diff[None] >= 0, np.exp(_RET_LOG_G[:, None, None] * np.maximum(diff, 0)[None]), 0.0)
    dmat = dmat * (RET_HD ** -0.5)
    zeta = np.exp(_RET_LOG_G[None, :] * (t - 1 - pos)[:, None]) * (RET_HD ** -0.5)
    xi = np.exp(_RET_LOG_G[None, :] * (pos + 1)[:, None])
    rep = lambda m: np.repeat(m, RET_HD, axis=1)
    return (jnp.asarray(dmat, F32), jnp.asarray(rep(zeta), F32), jnp.asarray(rep(xi), F32))


def _retention_kernel(q_ref, k_ref, v_ref, g_ref, dmat_ref, zeta_ref, xi_ref, avg_ref, o_ref, s_scr):
    t = q_ref.shape[0]

    @pl.when(pl.program_id(0) == 0)
    def _():
        s_scr[...] = jnp.zeros_like(s_scr)

    q = q_ref[...]
    k = k_ref[...]
    v = v_ref[...]
    kz = (k.astype(F32) * zeta_ref[...]).astype(BF16)
    xi = xi_ref[...]
    g = g_ref[...].astype(F32)
    outs = []
    for h in range(RET_HEADS):
        sl = slice(h * RET_HD, (h + 1) * RET_HD)
        qh, vh = q[:, sl], v[:, sl]
        inner = _dot_nt(qh, k[:, sl]) * dmat_ref[h]
        state = s_scr[h]
        o = _dot(inner.astype(BF16), vh) + _dot(qh, state.astype(BF16)) * xi[:, sl]
        s_scr[h] = float(np.exp(_RET_LOG_G[h] * t)) * state + _dot_tn(kz[:, sl], vh)
        outs.append(o)
    o = jnp.concatenate(outs, axis=1)
    o_hi = o.astype(BF16)
    o_lo = (o - o_hi.astype(F32)).astype(BF16)
    d = o - (_dot(o_hi, avg_ref[...]) + _dot(o_lo, avg_ref[...]))
    var = _dot((d * d).astype(BF16), avg_ref[...])
    o_ref[...] = (jax.nn.silu(g) * d * lax.rsqrt(var + GN_EPS)).astype(o_ref.dtype)


def _retention(z):
    n = z.shape[0]
    t = min(RET_T, n)
    w = RET_HEADS * RET_HD
    dmat, zeta, xi = _retention_consts(t)
    avg = jnp.asarray(np.kron(np.eye(RET_HEADS), np.full((RET_HD, RET_HD), 1.0 / RET_HD)), BF16)
    col = lambda c: pl.BlockSpec((t, w), lambda i: (i, c // w))
    return pl.pallas_call(
        _retention_kernel,
        grid=(n // t,),
        in_specs=[col(COL_CQ), col(COL_CK), col(COL_CV), col(COL_CG),
                  pl.BlockSpec((RET_HEADS, t, t), lambda i: (0, 0, 0)),
                  pl.BlockSpec((t, w), lambda i: (0, 0)), pl.BlockSpec((t, w), lambda i: (0, 0)),
                  pl.BlockSpec((w, w), lambda i: (0, 0))],
        out_specs=pl.BlockSpec((t, w), lambda i: (i, 0)),
        out_shape=jax.ShapeDtypeStruct((n, w), BF16),
        scratch_shapes=[pltpu.VMEM((RET_HEADS, RET_HD, RET_HD), F32)],
        compiler_params=_cparams(1),
        name="retention",
    )(z, z, z, z, dmat, zeta, xi, avg)


DA_B = 256
DA_UNROLLS = (8, 4, 2)
DA_HG = 2
DA_ONES = 16
_DA_SLOPES = (2.0 ** (-8.0 * np.arange(1, DIFF_HEADS + 1) / DIFF_HEADS)).astype(np.float32)


def _bf16_parts(x, n):
    parts = []
    for _ in range(n):
        p = float(np.asarray(x, np.float32).astype(jnp.bfloat16).astype(np.float32))
        parts.append(p)
        x = x - p
    return parts


_LOG2E_PARTS = _bf16_parts(LOG2E, 3)
_DA_DIGITS = -(-(DA_B - 1).bit_length() // 8)


def _diffattn_consts():
    kloc = np.arange(DA_B)[:, None]
    qloc = (np.arange(2 * DA_B) % DA_B)[None, :]
    slopes = _DA_SLOPES[:, None, None].astype(np.float64)
    kfeat = np.zeros((DIFF_HEADS, DA_B, DIFF_VD))
    for d in range(_DA_DIGITS):
        cols = slice(d * len(_LOG2E_PARTS), (d + 1) * len(_LOG2E_PARTS))
        kfeat[:, :, cols] = slopes * (((kloc >> (8 * d)) & 255) << (8 * d))
    diag = np.where((kloc // CHUNK) <= (qloc // CHUNK), LOG2E * slopes * (qloc - np.abs(qloc - kloc) - kloc), NEG_BIG)
    return jnp.asarray(kfeat, BF16), jnp.asarray(diag, F32)


def _diffattn_kernel(slope_ref, q_ref, k_ref, vt_ref, kfeat_ref, bias_ref, lam_ref, g_ref, o_ref,
                     qq_scr, m_scr, acc_scr, s_scr, p_scr, a_scr, *, lam_init):
    i = pl.program_id(1)
    blk = DA_B
    heads = range(DA_HG)
    feat = lax.broadcasted_iota(jnp.int32, (DIFF_VD, blk), 0)
    for hh in heads:
        qt = (q_ref[:, hh * DIFF_VD:(hh + 1) * DIFF_VD].astype(F32) * (LOG2E * DIFF_HD ** -0.5)).T
        qq_scr[hh, 0:DIFF_VD, 0:blk] = jnp.where(feat < DIFF_HD, qt, 0.0).astype(BF16)
        qq_scr[hh, 0:DIFF_VD, blk:] = jnp.where(feat >= DIFF_HD, qt, 0.0).astype(BF16)

    @pl.when(i == 0)
    def _():
        frow = lax.broadcasted_iota(jnp.int32, (DIFF_VD, 2 * blk), 0)
        qfeat = jnp.zeros((DIFF_VD, 2 * blk), F32)
        for idx, part in enumerate(_LOG2E_PARTS * _DA_DIGITS):
            qfeat = jnp.where(frow == idx, part, qfeat)
        for hh in heads:
            qq_scr[hh, DIFF_VD:, :] = qfeat.astype(BF16)

    m_scr[...] = jnp.full_like(m_scr, NEG_BIG)
    acc_scr[...] = jnp.zeros_like(acc_scr)

    def scores(hh, j):
        k = k_ref[pl.ds(pl.multiple_of(j * blk, blk), blk), hh * DIFF_VD:(hh + 1) * DIFF_VD]
        return _dot(jnp.concatenate([k, kfeat_ref[hh]], axis=1), qq_scr[hh])

    def softmax_step(hh, j, slot, diag):
        slope = slope_ref[pl.program_id(0) * DA_HG + hh]
        offset = slope * LOG2E * (j * blk).astype(F32)
        t = s_scr[hh, slot]
        if diag:
            t = t + bias_ref[hh]
        m_old = m_scr[hh]
        m_new = jnp.maximum(m_old, jnp.max(t, axis=0, keepdims=True) + offset)
        m_scr[hh] = m_new
        p_scr[hh, slot] = jnp.exp2(t - (m_new - offset)).astype(BF16)
        a_scr[hh, slot] = jnp.exp2(m_old - m_new)

    def accumulate(hh, j, slot):
        vt = vt_ref[hh, :, pl.ds(pl.multiple_of(j * blk, blk), blk)]
        vt = jnp.concatenate([vt, jnp.ones((DA_ONES, blk), BF16)], axis=0)
        acc_scr[hh] = a_scr[hh, slot] * acc_scr[hh] + _dot(vt, p_scr[hh, slot])

    def stage(j, slot, last=False):
        for hh in heads:
            if not last:
                s_scr[hh, 1 - slot] = scores(hh, j + 1)
            accumulate(hh, jnp.maximum(j - 1, 0), 1 - slot)
            softmax_step(hh, j, slot, last)

    for hh in heads:
        s_scr[hh, 0] = scores(hh, 0)
        p_scr[hh, 1] = jnp.zeros((blk, 2 * blk), BF16)
        a_scr[hh, 1] = jnp.ones((1, 2 * blk), F32)

    def run(first, trips, unroll):
        def body(jj, c):
            for u in range(unroll):
                stage(first + unroll * jj + u, u % 2)
            return c

        lax.fori_loop(0, trips, body, 0)
        return first + unroll * trips

    done = 0
    for unroll in DA_UNROLLS:
        done = run(done, (i - done) // unroll, unroll)
    odd = i - done == 1

    @pl.when(odd)
    def _():
        stage(i - 1, 0)
        stage(i, 1, last=True)
        for hh in heads:
            accumulate(hh, i, 1)

    @pl.when(jnp.logical_not(odd))
    def _():
        stage(i, 0, last=True)
        for hh in heads:
            accumulate(hh, i, 0)

    lv = lam_ref[...]
    lam = (jnp.exp(jnp.sum(lv[0:1, :] * lv[1:2, :], axis=-1, keepdims=True))
           - jnp.exp(jnp.sum(lv[2:3, :] * lv[3:4, :], axis=-1, keepdims=True)) + lam_init)
    for hh in heads:
        o = acc_scr[hh, 0:DIFF_VD, :] / acc_scr[hh, DIFF_VD:DIFF_VD + 1, :]
        od = o[:, 0:blk] - lam * o[:, blk:]
        od = od * lax.rsqrt(jnp.mean(od * od, axis=0, keepdims=True) + GN_EPS)
        o_ref[:, hh * DIFF_VD:(hh + 1) * DIFF_VD] = (od.T * g_ref[...] * (1.0 - lam_init)).astype(o_ref.dtype)


def _diffattn(z, vt, lam_vecs, subln_g, lam_init):
    n = z.shape[0]
    assert n % DA_B == 0 and DIFF_HEADS % DA_HG == 0
    w = DIFF_VD
    gw = DA_HG * w
    kfeat, bias = _diffattn_consts()
    vt = vt.reshape(DIFF_HEADS, w, n)
    return pl.pallas_call(
        functools.partial(_diffattn_kernel, lam_init=lam_init),
        grid_spec=pltpu.PrefetchScalarGridSpec(
            num_scalar_prefetch=1,
            grid=(DIFF_HEADS // DA_HG, n // DA_B),
            in_specs=[
                pl.BlockSpec((DA_B, gw), lambda h, i, s: (i, COL_DQ // gw + h)),
                pl.BlockSpec((n, gw), lambda h, i, s: (0, COL_DK // gw + h)),
                pl.BlockSpec((DA_HG, w, n), lambda h, i, s: (h, 0, 0)),
                pl.BlockSpec((DA_HG, DA_B, w), lambda h, i, s: (h, 0, 0)),
                pl.BlockSpec((DA_HG, DA_B, 2 * DA_B), lambda h, i, s: (h, 0, 0)),
                pl.BlockSpec((4, DIFF_HD), lambda h, i, s: (0, 0)),
                pl.BlockSpec((1, w), lambda h, i, s: (0, 0)),
            ],
            out_specs=pl.BlockSpec((DA_B, gw), lambda h, i, s: (i, h)),
            scratch_shapes=[pltpu.VMEM((DA_HG, 2 * w, 2 * DA_B), BF16), pltpu.VMEM((DA_HG, 1, 2 * DA_B), F32),
                            pltpu.VMEM((DA_HG, w + DA_ONES, 2 * DA_B), F32),
                            pltpu.VMEM((DA_HG, 2, DA_B, 2 * DA_B), F32), pltpu.VMEM((DA_HG, 2, DA_B, 2 * DA_B), BF16),
                            pltpu.VMEM((DA_HG, 2, 1, 2 * DA_B), F32)],
        ),
        out_shape=jax.ShapeDtypeStruct((n, DIFF_HEADS * w), BF16),
        compiler_params=_cparams(2),
        name="diffattn",
    )(jnp.asarray(_DA_SLOPES), z, z, vt, kfeat, bias, lam_vecs, subln_g)


def _memkv_kernel(mem_ref, g_ref, wk_ref, wv_ref, k_ref, v_ref):
    mn = _rms(mem_ref[...], g_ref[...]).astype(BF16)
    k_ref[...] = _dot(mn, wk_ref[...]).astype(BF16)
    v_ref[...] = _dot(mn, wv_ref[...]).astype(BF16)


def _memkv(mem, g, wk, wv):
    m = mem.shape[0]
    w = MX_HEADS * MX_HD
    out = jax.ShapeDtypeStruct((m, w), BF16)
    return pl.pallas_call(_memkv_kernel, out_shape=(out, out), name="memkv",
                          compiler_params=pltpu.CompilerParams(vmem_limit_bytes=VMEM_LIMIT))(mem, g, wk, wv)


ROUTE_LANES = LANES


def _route(logits, carry):
    t = logits.shape[0]
    lane_i = lax.broadcasted_iota(jnp.int32, (t, ROUTE_LANES), 1)
    lane = lane_i.astype(F32)
    big = float(ROUTE_LANES)
    gl = jnp.where(lane_i < N_GROUPS, logits, NEG_BIG)
    gmax = jnp.max(gl, axis=-1, keepdims=True)
    gsum = jnp.sum(jnp.exp(gl - gmax), axis=-1, keepdims=True)
    g_sel = jnp.min(jnp.where(gl == gmax, lane, big), axis=-1, keepdims=True)
    g_prob = 1.0 / gsum
    lo = N_GROUPS + EXP_PER_GROUP * g_sel
    el = jnp.where((lane >= lo) & (lane < lo + EXP_PER_GROUP), logits, NEG_BIG)
    e1 = jnp.max(el, axis=-1, keepdims=True)
    i1 = jnp.min(jnp.where(el == e1, lane, big), axis=-1, keepdims=True)
    el2 = jnp.where(lane == i1, NEG_BIG, el)
    e2 = jnp.max(el2, axis=-1, keepdims=True)
    i2 = jnp.min(jnp.where(el2 == e2, lane, big), axis=-1, keepdims=True)
    esum = jnp.sum(jnp.exp(el - e1), axis=-1, keepdims=True)
    p1 = 1.0 / esum
    p2 = jnp.exp(e2 - e1) / esum
    w1 = p1 / (p1 + p2) * g_prob
    w2 = p2 / (p1 + p2) * g_prob
    hot1 = lane == i1
    hot2 = lane == i2
    cnt = jnp.where(hot1 | hot2, 1.0, 0.0)
    r = lax.broadcasted_iota(jnp.int32, (t, t), 0)
    c = lax.broadcasted_iota(jnp.int32, (t, t), 1)
    before = jnp.where(c < r, 1.0, 0.0).astype(BF16)
    prefix = _dot(before, cnt.astype(BF16)) + carry
    rank1 = jnp.sum(jnp.where(hot1, prefix, 0.0), axis=-1, keepdims=True)
    rank2 = jnp.sum(jnp.where(hot2, prefix, 0.0), axis=-1, keepdims=True)
    rec = jnp.zeros((t, ROUTE_LANES), F32)
    for idx, val in enumerate((i1 - N_GROUPS, i2 - N_GROUPS, w1, w2, rank1, rank2)):
        rec = jnp.where(lane_i == idx, val, rec)
    return rec, carry + jnp.sum(cnt, axis=0, keepdims=True)


POST_SUB = 256


def _mixpost_kernel(x_ref, ya_ref, yb_ref, yc_ref, yd_ref, g0_ref, g1_ref, g2_ref, g3_ref, wb_ref, wo_ref,
                    gm_ref, km_ref, vm_ref, wq_ref, wmo_ref, gf_ref, wrh_ref, wrl_ref, br_ref,
                    xo_ref, t_ref, rec_ref, cnt_ref, carry):
    @pl.when(pl.program_id(0) == 0)
    def _():
        carry[...] = jnp.zeros_like(carry)

    km = km_ref[...]
    vm = vm_ref[...]
    logits = []
    for sub in range(x_ref.shape[0] // POST_SUB):
        rows = slice(sub * POST_SUB, (sub + 1) * POST_SUB)
        mixed = None
        for y_ref, g_ref, b in ((ya_ref, g0_ref, 0), (yb_ref, g1_ref, 1), (yc_ref, g2_ref, 2), (yd_ref, g3_ref, 3)):
            term = jax.nn.sigmoid(g_ref[rows, :].astype(F32)) * _dot(y_ref[rows, :], wb_ref[b])
            mixed = term if mixed is None else mixed + term
        x = x_ref[rows, :] + _dot(mixed.astype(BF16), wo_ref[...])
        q = _dot(_rms(x, gm_ref[...]).astype(BF16), wq_ref[...]).astype(BF16)
        outs = []
        for h in range(MX_HEADS):
            sl = slice(h * MX_HD, (h + 1) * MX_HD)
            s = _dot_nt(q[:, sl], km[:, sl]) * (MX_HD ** -0.5)
            p = jnp.exp(s - jnp.max(s, axis=-1, keepdims=True))
            p = p / jnp.sum(p, axis=-1, keepdims=True)
            outs.append(_dot(p.astype(BF16), vm[:, sl]))
        x = x + _dot(jnp.concatenate(outs, axis=1).astype(BF16), wmo_ref[...])
        xo_ref[rows, :] = x
        t = _rms(x, gf_ref[...])
        t_ref[rows, :] = _pack_rows(t)
        t_hi = t.astype(BF16)
        t_lo = (t - t_hi.astype(F32)).astype(BF16)
        logits.append(_dot(t_hi, wrh_ref[...]) + (_dot(t_lo, wrh_ref[...]) + _dot(t_hi, wrl_ref[...])) + br_ref[...])
    running = carry[0:1, :]
    for sub, lg in enumerate(logits):
        rec, running = _route(lg, running)
        rec_ref[sub * POST_SUB:(sub + 1) * POST_SUB, :] = rec
    carry[0:1, :] = running
    cnt_ref[...] = jnp.broadcast_to(running, cnt_ref.shape)


def _mixpost(x, z, ya, yb, yc, yd, wb, wo, gm, km, vm, wq, wmo, gf, wr, br, tm=512):
    n = x.shape[0]
    tm = min(tm, n)
    assert tm % POST_SUB == 0
    w = MX_HEADS * MX_HD
    m = km.shape[0]
    wr_hi = wr.astype(BF16)
    wr_lo = (wr - wr_hi.astype(F32)).astype(BF16)
    row = lambda wd: pl.BlockSpec((tm, wd), lambda i: (i, 0))
    gate = lambda b: pl.BlockSpec((tm, D_MODEL), lambda i: (i, COL_GATE // D_MODEL + b))
    full = lambda shape: pl.BlockSpec(shape, lambda i: (0,) * len(shape))
    return pl.pallas_call(
        _mixpost_kernel,
        grid=(n // tm,),
        in_specs=[row(D_MODEL), row(BRANCH_W), row(BRANCH_W), row(BRANCH_W), row(BRANCH_W),
                  gate(0), gate(1), gate(2), gate(3),
                  full((N_BRANCH, BRANCH_W, D_MODEL)), full((D_MODEL, D_MODEL)),
                  full((1, D_MODEL)), full((m, w)), full((m, w)), full((D_MODEL, w)), full((w, D_MODEL)),
                  full((1, D_MODEL)), full((D_MODEL, ROUTE_LANES)), full((D_MODEL, ROUTE_LANES)),
                  full((1, ROUTE_LANES))],
        out_specs=[row(D_MODEL), row(PACK_W), row(ROUTE_LANES), full((8, ROUTE_LANES))],
        out_shape=[jax.ShapeDtypeStruct((n, D_MODEL), F32), jax.ShapeDtypeStruct((n, PACK_W), jnp.uint32),
                   jax.ShapeDtypeStruct((n, ROUTE_LANES), F32), jax.ShapeDtypeStruct((8, ROUTE_LANES), F32)],
        scratch_shapes=[pltpu.VMEM((8, ROUTE_LANES), F32)],
        compiler_params=_cparams(1),
        name="mixpost",
    )(x, ya, yb, yc, yd, z, z, z, z, wb, wo, gm, km, vm, wq, wmo, gf, wr_hi, wr_lo, br)


MOE_TB = 256


def _row_copy(src, src_row, dst, dst_row, sem):
    return pltpu.make_async_copy(src.at[pl.ds(src_row, 1), :], dst.at[pl.ds(dst_row, 1), :], sem)


def _dispatch_kernel(dest_ref, t_ref, xs_in_ref, xs_ref, sem):
    del xs_in_ref
    base = pl.program_id(0) * MOE_TB

    def issue(r, c):
        for k in range(2):
            _row_copy(t_ref, r, xs_ref, dest_ref[2 * (base + r) + k], sem).start(priority=k)
        return c

    lax.fori_loop(0, MOE_TB, issue, 0, unroll=8)
    for k in range(2):
        pltpu.make_async_copy(t_ref, xs_ref.at[pl.ds(0, MOE_TB), :], sem).wait()


def _dispatch(dest, t, n_slots):
    n = t.shape[0]
    xs0 = jnp.zeros((n_slots, PACK_W), jnp.uint32)
    return pl.pallas_call(
        _dispatch_kernel,
        grid_spec=pltpu.PrefetchScalarGridSpec(
            num_scalar_prefetch=1,
            grid=(n // MOE_TB,),
            in_specs=[pl.BlockSpec((MOE_TB, PACK_W), lambda i, d: (i, 0)), pl.BlockSpec(memory_space=pl.ANY)],
            out_specs=pl.BlockSpec(memory_space=pl.ANY),
            scratch_shapes=[pltpu.SemaphoreType.DMA(())],
        ),
        out_shape=jax.ShapeDtypeStruct((n_slots, PACK_W), jnp.uint32),
        input_output_aliases={2: 0},
        compiler_params=_cparams(1),
        name="dispatch",
    )(dest, t, xs0)


def _expert_kernel(be_ref, na_ref, xs_ref, w1_ref, w3_ref, w2_ref, ys_ref, w1b, w3b, w2b):
    i = pl.program_id(0)
    active = i < na_ref[0]
    new_expert = (i == 0) | (be_ref[i] != be_ref[jnp.maximum(i - 1, 0)])

    @pl.when(active & new_expert)
    def _():
        w1b[...] = w1_ref[...].astype(BF16)
        w3b[...] = w3_ref[...].astype(BF16)
        w2b[...] = w2_ref[...].astype(BF16)

    @pl.when(active)
    def _():
        x = _unpack_rows(xs_ref[...]).astype(BF16)
        a = _dot(x, w1b[...])
        b = _dot(x, w3b[...])
        ys_ref[...] = _pack_rows(_dot((jax.nn.silu(a) * b).astype(BF16), w2b[...]))

    @pl.when(jnp.logical_not(active))
    def _():
        ys_ref[...] = jnp.zeros_like(ys_ref)


def _experts(block_e, n_active, xs, w1, w3, w2, layer):
    n_blocks = xs.shape[0] // MOE_BLOCK
    blk = lambda i, na: jnp.minimum(i, jnp.maximum(na[0] - 1, 0))
    wspec = lambda rows, cols: pl.BlockSpec((None, None, rows, cols), lambda i, be, na: (layer, be[blk(i, na)], 0, 0))
    return pl.pallas_call(
        _expert_kernel,
        grid_spec=pltpu.PrefetchScalarGridSpec(
            num_scalar_prefetch=2,
            grid=(n_blocks,),
            in_specs=[
                pl.BlockSpec((MOE_BLOCK, PACK_W), lambda i, be, na: (blk(i, na), 0)),
                wspec(D_MODEL, D_EXPERT), wspec(D_MODEL, D_EXPERT), wspec(D_EXPERT, D_MODEL),
            ],
            out_specs=pl.BlockSpec((MOE_BLOCK, PACK_W), lambda i, be, na: (i, 0)),
            scratch_shapes=[pltpu.VMEM((D_MODEL, D_EXPERT), BF16), pltpu.VMEM((D_MODEL, D_EXPERT), BF16),
                            pltpu.VMEM((D_EXPERT, D_MODEL), BF16)],
        ),
        out_shape=jax.ShapeDtypeStruct(xs.shape, jnp.uint32),
        compiler_params=_cparams(1),
        name="experts",
    )(block_e, n_active, xs, w1, w3, w2)


def _combine_kernel(dest_ref, x_ref, rec_ref, g_ref, ys_ref, o_ref, rows, sems, *, final):
    i = pl.program_id(0)

    def gather(step, slot):
        base = step * MOE_TB

        def issue(r, c):
            for k in range(2):
                _row_copy(ys_ref, dest_ref[2 * (base + r) + k], rows.at[slot, k], r, sems.at[slot]).start(priority=k)
            return c

        lax.fori_loop(0, MOE_TB, issue, 0, unroll=8)

    @pl.when(i == 0)
    def _():
        gather(0, 0)

    for slot in range(2):
        @pl.when(i % 2 == slot)
        def _():
            @pl.when(i + 1 < pl.num_programs(0))
            def _():
                gather(i + 1, 1 - slot)

            for k in range(2):
                pltpu.make_async_copy(ys_ref.at[pl.ds(0, MOE_TB), :], rows.at[slot, k], sems.at[slot]).wait()
            rec = rec_ref[...]
            x = x_ref[...] + rec[:, 2:3] * _unpack_rows(rows[slot, 0]) + rec[:, 3:4] * _unpack_rows(rows[slot, 1])
            o_ref[...] = _rms(x, g_ref[...]) if final else x


def _combine(dest, x, rec, ys, g, final):
    n = x.shape[0]
    row = lambda wd: pl.BlockSpec((MOE_TB, wd), lambda i, d: (i, 0))
    return pl.pallas_call(
        functools.partial(_combine_kernel, final=final),
        grid_spec=pltpu.PrefetchScalarGridSpec(
            num_scalar_prefetch=1,
            grid=(n // MOE_TB,),
            in_specs=[row(D_MODEL), row(ROUTE_LANES), pl.BlockSpec((1, D_MODEL), lambda i, d: (0, 0)),
                      pl.BlockSpec(memory_space=pl.ANY)],
            out_specs=row(D_MODEL),
            scratch_shapes=[pltpu.VMEM((2, 2, MOE_TB, PACK_W), jnp.uint32), pltpu.SemaphoreType.DMA((2,))],
        ),
        out_shape=jax.ShapeDtypeStruct((n, D_MODEL), F32),
        compiler_params=_cparams(1),
        name="combine",
    )(dest, x, rec, g, ys)


def _moe(x, t, rec, cnt, w1, w3, w2, layer, g_final, final):
    n = x.shape[0]
    n_asg = 2 * n
    n_blocks = -(-(n_asg + N_EXPERTS * (MOE_BLOCK - 1)) // MOE_BLOCK)
    counts = cnt[0, N_GROUPS:N_GROUPS + N_EXPERTS].astype(jnp.int32)
    padded = (counts + MOE_BLOCK - 1) // MOE_BLOCK * MOE_BLOCK
    pad_end = jnp.cumsum(padded)
    pad_start = pad_end - padded
    expert = rec[:, 0:2].astype(jnp.int32)
    first = jnp.sum(jnp.where(expert[..., None] == jnp.arange(N_EXPERTS), pad_start, 0), axis=-1)
    dest = (first + rec[:, 4:6].astype(jnp.int32)).reshape(-1)
    starts = jnp.arange(n_blocks, dtype=jnp.int32) * MOE_BLOCK
    block_e = jnp.minimum(jnp.sum(pad_end[None, :] <= starts[:, None], axis=1), N_EXPERTS - 1).astype(jnp.int32)
    n_active = (pad_end[-1:] // MOE_BLOCK).astype(jnp.int32)
    xs = _dispatch(dest, t, n_blocks * MOE_BLOCK)
    ys = _experts(block_e, n_active, xs, w1, w3, w2, layer)
    return _combine(dest, x, rec, ys, g_final, final)


def _block_diag(w):
    eye = jnp.eye(LRU_BLOCKS, dtype=w.dtype)
    return jnp.einsum("kcd,kj->kcjd", w, eye).reshape(LRU_W, LRU_W)


def _router_weights(w_group, b_group, w_router, b_router):
    pad = ROUTE_LANES - N_GROUPS - N_EXPERTS
    wr = jnp.concatenate([w_group, w_router, jnp.zeros((D_MODEL, pad), F32)], axis=1)
    br = jnp.concatenate([b_group, b_router, jnp.zeros((pad,), F32)])[None, :]
    return wr, br


def kernel(x, mem, norm_mix, w_in, conv_w, conv_b, lru_wa, lru_ba, lru_wx, lru_bx, lru_lambda, ca_rel_bias, diff_lambda, diff_subln, w_branch, w_out, norm_mem, mem_norm, w_mq, w_mk, w_mv, w_mo, norm_ffn, w_group, b_group, w_router, b_router, w1, w3, w2, final_norm):
    b, n, d = x.shape
    assert b == 1 and d == D_MODEL
    xs = x[0]
    for l in range(DEPTH):
        lam_init = 0.8 - 0.6 * math.exp(-0.3 * l)
        z, vt_b, vt_d = _inproj(xs, norm_mix[l][None, :], w_in[l].astype(BF16))
        wcat = jnp.concatenate([_block_diag(lru_wa[l]), _block_diag(lru_wx[l])], axis=1).astype(BF16)
        bcat = jnp.concatenate([lru_ba[l], lru_bx[l]])[None, :]
        ya = _lru(z, conv_w[l], conv_b[l][None, :], wcat, bcat, lru_lambda[l][None, :])
        yb = _chunkattn(z, vt_b, _chunk_bias(ca_rel_bias[l]))
        yc = _retention(z)
        yd = _diffattn(z, vt_d, diff_lambda[l], diff_subln[l][None, :], lam_init)
        km, vm = _memkv(mem[0], mem_norm[None, :], w_mk[l].astype(BF16), w_mv[l].astype(BF16))
        wr, br = _router_weights(w_group[l], b_group[l], w_router[l], b_router[l])
        x2, t, rec, cnt = _mixpost(xs, z, ya, yb, yc, yd, w_branch[l].astype(BF16), w_out[l].astype(BF16),
                                   norm_mem[l][None, :], km, vm, w_mq[l].astype(BF16), w_mo[l].astype(BF16),
                                   norm_ffn[l][None, :], wr, br)
        xs = _moe(x2, t, rec, cnt, w1, w3, w2, l, final_norm[None, :], final=(l == DEPTH - 1))
    return xs[None]
```
